```python
import jax, jax.numpy as jnp
from jax import lax
import numpy as np

D_MODEL = 1024
BATCH = 32
SEQ = 2048
DEPTH = 1
DEC_BATCH = 8
DEC_SEQ = 8192
PAST_LEN = 128

HEAD_DIM = 64
N_Q_HEADS = 8
N_KV_HEADS = 2
Q_PER_KV = N_Q_HEADS // N_KV_HEADS
ATTN_WIDTH = N_Q_HEADS * HEAD_DIM
KV_WIDTH = N_KV_HEADS * HEAD_DIM
WINDOW = 128
BLOCK = 128
RNN_WIDTH = D_MODEL // 2
RNN_BLOCKS = 8
RNN_BLOCK_W = RNN_WIDTH // RNN_BLOCKS
CONV_WIDTH = 4
CONV_LEFT = CONV_WIDTH // 2
LRU_C = 8.0
LRU_A_MIN = 0.9
LRU_A_MAX = 0.999
MIX_WIDTH = ATTN_WIDTH + RNN_WIDTH
IN_WIDTH = ATTN_WIDTH + 2 * KV_WIDTH + 2 * RNN_WIDTH
SPLITS = (ATTN_WIDTH, ATTN_WIDTH + KV_WIDTH, ATTN_WIDTH + 2 * KV_WIDTH,
          ATTN_WIDTH + 2 * KV_WIDTH + RNN_WIDTH)
N_EXPERTS = 16
EXPERT_FF = 2 * D_MODEL
CAPACITY_FACTOR = 2
ALPHA = (2.0 * DEPTH) ** 0.25
BETA = (8.0 * DEPTH) ** -0.25
LN_EPS = 1e-5
RMS_EPS = 1e-6

kernel_name = "hybrid_swa_rglru_ec_moe_encoder"


def layer_norm(x, g, b):
    xf = x.astype(jnp.float32)
    mu = xf.mean(-1, keepdims=True)
    var = jnp.square(xf - mu).mean(-1, keepdims=True)
    y = (xf - mu) * lax.rsqrt(var + LN_EPS) * g.astype(jnp.float32) + b.astype(jnp.float32)
    return y.astype(x.dtype)


def rms_norm(x, g):
    xf = x.astype(jnp.float32)
    y = xf * lax.rsqrt(jnp.square(xf).mean(-1, keepdims=True) + RMS_EPS) * g.astype(jnp.float32)
    return y.astype(x.dtype)


def alibi_slopes():
    return jnp.asarray([2.0 ** (-8.0 * (h + 1) / N_Q_HEADS) for h in range(N_Q_HEADS)], dtype=jnp.float32)


def banded_window_attention(q, k, v, sink):
    B, L = q.shape[:2]
    nb = L // BLOCK
    qb = q.reshape(B, nb, BLOCK, N_KV_HEADS, Q_PER_KV, HEAD_DIM)

    def key_blocks(t):
        tp = jnp.pad(t, ((0, 0), (BLOCK, BLOCK), (0, 0), (0, 0)))
        tp = tp.reshape(B, nb + 2, BLOCK, N_KV_HEADS, HEAD_DIM)
        return jnp.concatenate([tp[:, :-2], tp[:, 1:-1], tp[:, 2:]], axis=2)

    kb = key_blocks(k)
    vb = key_blocks(v)
    s = jnp.einsum('bnqkgd,bnskd->bnkgqs', qb, kb,
                   preferred_element_type=jnp.float32) * (HEAD_DIM ** -0.5)
    qi = jnp.arange(BLOCK)[:, None]
    sj = jnp.arange(3 * BLOCK)[None, :]
    rel = sj - BLOCK - qi
    dist = jnp.abs(rel).astype(jnp.float32)
    kpos = (jnp.arange(nb)[:, None, None] - 1) * BLOCK + sj[None]
    valid = (jnp.abs(rel) <= WINDOW)[None] & (kpos >= 0) & (kpos < L)
    slopes = alibi_slopes().reshape(N_KV_HEADS, Q_PER_KV, 1, 1)
    s = s - slopes * dist
    s = jnp.where(valid[None, :, None, None], s, -jnp.inf)
    sink_b = sink.astype(jnp.float32).reshape(N_KV_HEADS, Q_PER_KV, 1)
    m = jnp.maximum(s.max(-1), sink_b)
    p = jnp.exp(s - m[..., None])
    denom = p.sum(-1) + jnp.exp(sink_b - m)
    o = jnp.einsum('bnkgqs,bnskd->bnqkgd', p, vb.astype(jnp.float32))
    o = o / jnp.moveaxis(denom, -1, 2)[..., None]
    return o.reshape(B, L, ATTN_WIDTH).astype(q.dtype)


def centred_depthwise_conv(x, w, b):
    L = x.shape[1]
    xp = jnp.pad(x, ((0, 0), (CONV_LEFT, CONV_WIDTH - 1 - CONV_LEFT), (0, 0)))
    out = xp[:, 0:L] * w[0]
    for j in range(1, CONV_WIDTH):
        out = out + xp[:, j:j + L] * w[j]
    return out + b


def _linear_combine(left, right):
    a_l, b_l = left
    a_r, b_r = right
    return a_l * a_r, a_r * b_l + b_r


def rg_lru_scan(x, w_a, b_a, w_x, b_x, lam):
    B, L, _ = x.shape
    xf = x.astype(jnp.float32)
    xb = xf.reshape(B, L, RNN_BLOCKS, RNN_BLOCK_W)
    r = jax.nn.sigmoid(jnp.einsum('blhi,hij->blhj', xb, w_a.astype(jnp.float32)).reshape(B, L, RNN_WIDTH)
                       + b_a.astype(jnp.float32))
    i = jax.nn.sigmoid(jnp.einsum('blhi,hij->blhj', xb, w_x.astype(jnp.float32)).reshape(B, L, RNN_WIDTH)
                       + b_x.astype(jnp.float32))
    log_a = -LRU_C * r * jax.nn.softplus(-lam.astype(jnp.float32))
    a = jnp.exp(log_a)
    u = jnp.sqrt(-jnp.expm1(2.0 * log_a)) * (i * xf)
    _, h = lax.associative_scan(_linear_combine, (a, u), axis=1)
    return h


def hybrid_mixer(x, w_in, sink, attn_g, rnn_g, conv_w, conv_b, w_a, b_a, w_x, b_x, lam, w_out):
    B, L, _ = x.shape
    proj = x @ w_in
    q, k, v, xr, gate = jnp.split(proj, SPLITS, axis=-1)
    q = q.reshape(B, L, N_Q_HEADS, HEAD_DIM)
    k = k.reshape(B, L, N_KV_HEADS, HEAD_DIM)
    v = v.reshape(B, L, N_KV_HEADS, HEAD_DIM)
    y_attn = banded_window_attention(q, k, v, sink)
    xc = centred_depthwise_conv(xr, conv_w, conv_b)
    h_fwd = rg_lru_scan(xc, w_a[0], b_a[0], w_x[0], b_x[0], lam[0])
    h_bwd = jnp.flip(rg_lru_scan(jnp.flip(xc, 1), w_a[1], b_a[1], w_x[1], b_x[1], lam[1]), 1)
    y_rnn = ((h_fwd + h_bwd) * jax.nn.gelu(gate.astype(jnp.float32))).astype(x.dtype)
    merged = jnp.concatenate([rms_norm(y_attn, attn_g), rms_norm(y_rnn, rnn_g)], axis=-1)
    return merged @ w_out


def expert_choice_ffn(x, w_router, w_gate, w_up, w_down):
    B, L, D = x.shape
    T = B * L
    cap = CAPACITY_FACTOR * T // N_EXPERTS
    xt = x.reshape(T, D)
    aff = jax.nn.softmax((xt @ w_router).astype(jnp.float32), axis=-1)
    g, idx = lax.top_k(aff.T, cap)
    xe = xt[idx]
    hid = jax.nn.silu(jnp.einsum('ecd,edf->ecf', xe, w_gate)) * jnp.einsum('ecd,edf->ecf', xe, w_up)
    ye = jnp.einsum('ecf,efd->ecd', hid, w_down) * g[..., None].astype(x.dtype)
    y = jnp.zeros_like(xt).at[idx.reshape(-1)].add(ye.reshape(-1, D))
    return y.reshape(B, L, D)


def setup_inputs(seed: int = 0) -> dict:
    key = jax.random.key(seed)
    ks = jax.random.split(key, 24)
    f32 = jnp.float32

    def nrm(k, shape, scale):
        return jax.random.normal(k, shape, f32) * scale

    u = jax.random.uniform(ks[12], (DEPTH, 2, RNN_WIDTH), f32, LRU_A_MIN, LRU_A_MAX)
    a0 = u ** (1.0 / LRU_C)
    lru_lambda = jnp.log(a0) - jnp.log1p(-a0)
    return {
        "x_prompt": nrm(ks[0], (BATCH, SEQ, D_MODEL), 1.0),
        "x_sample": nrm(ks[1], (DEC_BATCH, DEC_SEQ, D_MODEL), 1.0),
        "w_in": nrm(ks[2], (DEPTH, D_MODEL, IN_WIDTH), D_MODEL ** -0.5),
        "attn_sink": nrm(ks[3], (DEPTH, N_Q_HEADS), 0.5),
        "attn_norm_g": 1.0 + nrm(ks[4], (DEPTH, ATTN_WIDTH), 0.02),
        "rnn_norm_g": 1.0 + nrm(ks[5], (DEPTH, RNN_WIDTH), 0.02),
        "conv_w": nrm(ks[6], (DEPTH, CONV_WIDTH, RNN_WIDTH), CONV_WIDTH ** -0.5),
        "conv_b": nrm(ks[7], (DEPTH, RNN_WIDTH), 0.01),
        "lru_w_a": nrm(ks[8], (DEPTH, 2, RNN_BLOCKS, RNN_BLOCK_W, RNN_BLOCK_W), RNN_BLOCK_W ** -0.5),
        "lru_b_a": nrm(ks[9], (DEPTH, 2, RNN_WIDTH), 0.01),
        "lru_w_x": nrm(ks[10], (DEPTH, 2, RNN_BLOCKS, RNN_BLOCK_W, RNN_BLOCK_W), RNN_BLOCK_W ** -0.5),
        "lru_b_x": nrm(ks[11], (DEPTH, 2, RNN_WIDTH), 0.01),
        "lru_lambda": lru_lambda,
        "w_out": nrm(ks[13], (DEPTH, MIX_WIDTH, D_MODEL), MIX_WIDTH ** -0.5 * BETA),
        "ln1_g": 1.0 + nrm(ks[14], (DEPTH, D_MODEL), 0.02),
        "ln1_b": nrm(ks[15], (DEPTH, D_MODEL), 0.01),
        "w_router": nrm(ks[16], (DEPTH, D_MODEL, N_EXPERTS), D_MODEL ** -0.5),
        "w_gate": nrm(ks[17], (DEPTH, N_EXPERTS, D_MODEL, EXPERT_FF), D_MODEL ** -0.5),
        "w_up": nrm(ks[18], (DEPTH, N_EXPERTS, D_MODEL, EXPERT_FF), D_MODEL ** -0.5),
        "w_down": nrm(ks[19], (DEPTH, N_EXPERTS, EXPERT_FF, D_MODEL), EXPERT_FF ** -0.5 * BETA),
        "ln2_g": 1.0 + nrm(ks[20], (DEPTH, D_MODEL), 0.02),
        "ln2_b": nrm(ks[21], (DEPTH, D_MODEL), 0.01),
    }


def reference(x_prompt, x_sample, w_in, attn_sink, attn_norm_g, rnn_norm_g, conv_w, conv_b,
              lru_w_a, lru_b_a, lru_w_x, lru_b_x, lru_lambda, w_out, ln1_g, ln1_b,
              w_router, w_gate, w_up, w_down, ln2_g, ln2_b):
    def trunk(x):
        for l in range(DEPTH):
            mix = hybrid_mixer(x, w_in[l], attn_sink[l], attn_norm_g[l], rnn_norm_g[l],
                               conv_w[l], conv_b[l], lru_w_a[l], lru_b_a[l], lru_w_x[l],
                               lru_b_x[l], lru_lambda[l], w_out[l])
            x = layer_norm(ALPHA * x + mix, ln1_g[l], ln1_b[l])
            ffn = expert_choice_ffn(x, w_router[l], w_gate[l], w_up[l], w_down[l])
            x = layer_norm(ALPHA * x + ffn, ln2_g[l], ln2_b[l])
        return x

    y_prompt = trunk(x_prompt)
    y_sample = trunk(x_sample)
    return (y_prompt, y_sample)
```

```python
import functools
import math

import jax
import jax.numpy as jnp
from jax import lax
from jax.experimental import pallas as pl
from jax.experimental.pallas import tpu as pltpu

D_MODEL = 1024
HEAD_DIM = 64
N_Q_HEADS = 8
N_KV_HEADS = 2
Q_PER_KV = N_Q_HEADS // N_KV_HEADS
ATTN_WIDTH = N_Q_HEADS * HEAD_DIM
KV_WIDTH = N_KV_HEADS * HEAD_DIM
BLOCK = 128
RNN_WIDTH = 512
RNN_BLOCKS = 8
RNN_BLOCK_W = RNN_WIDTH // RNN_BLOCKS
CONV_WIDTH = 4
LRU_C = 8.0
N_EXPERTS = 16
EXPERT_FF = 2048
CAPACITY_FACTOR = 2
ALPHA = 2.0 ** 0.25
LN_EPS = 1e-5
RMS_EPS = 1e-6
MASKED = -1e30

SUBLANES = 8
VMEM_LIMIT = 56 * 1024 * 1024

ROW_TILE = 512
LRU_CHUNK = 256
FFN_ROWS = 512
FFN_FCHUNK = 512

BF16 = jnp.bfloat16
F32 = jnp.float32


def _params(*sem):
    return pltpu.CompilerParams(dimension_semantics=sem, vmem_limit_bytes=VMEM_LIMIT)


def _in_proj_kernel(x_ref, w_ref, q_ref, k_ref, v_ref, xr_ref, gate_ref):
    xb = x_ref[...].astype(BF16)

    def proj(lo, hi):
        return jnp.dot(xb, w_ref[:, lo:hi], preferred_element_type=F32)

    o = 0
    q_ref[...] = (proj(o, o + ATTN_WIDTH) * (HEAD_DIM ** -0.5)).astype(BF16)
    o += ATTN_WIDTH
    k_ref[...] = proj(o, o + KV_WIDTH).astype(BF16)
    o += KV_WIDTH
    v_ref[...] = proj(o, o + KV_WIDTH).astype(BF16)
    o += KV_WIDTH
    xr_ref[...] = proj(o, o + RNN_WIDTH)
    o += RNN_WIDTH
    gate_ref[...] = proj(o, o + RNN_WIDTH)


def _in_proj(x2, w_bf16):
    t = x2.shape[0]
    in_w = w_bf16.shape[1]
    row = lambda w: pl.BlockSpec((ROW_TILE, w), lambda i: (i, 0))
    return pl.pallas_call(
        _in_proj_kernel,
        grid=(t // ROW_TILE,),
        in_specs=[row(D_MODEL), pl.BlockSpec((D_MODEL, in_w), lambda i: (0, 0))],
        out_specs=[row(ATTN_WIDTH), row(KV_WIDTH), row(KV_WIDTH), row(RNN_WIDTH), row(RNN_WIDTH)],
        out_shape=[
            jax.ShapeDtypeStruct((t, ATTN_WIDTH), BF16),
            jax.ShapeDtypeStruct((t, KV_WIDTH), BF16),
            jax.ShapeDtypeStruct((t, KV_WIDTH), BF16),
            jax.ShapeDtypeStruct((t, RNN_WIDTH), F32),
            jax.ShapeDtypeStruct((t, RNN_WIDTH), F32),
        ],
        compiler_params=_params("parallel"),
        name="in_proj",
    )(x2, w_bf16)


def _attn_kernel(q_ref, kp_ref, kc_ref, kn_ref, vp_ref, vc_ref, vn_ref, bias_ref, sink_ref, g_ref,
                 o_ref, *, nb):
    i = pl.program_id(1)
    q = q_ref[0]
    col = lax.broadcasted_iota(jnp.int32, (1, 3 * BLOCK), 1)
    outside = ((col < BLOCK) & (i == 0)) | ((col >= 2 * BLOCK) & (i == nb - 1))
    edge = jnp.where(outside, MASKED, 0.0)
    heads = []
    for g in range(N_KV_HEADS):
        kv = slice(g * HEAD_DIM, (g + 1) * HEAD_DIM)
        qs = jnp.concatenate(
            [q[:, (Q_PER_KV * g + j) * HEAD_DIM:(Q_PER_KV * g + j + 1) * HEAD_DIM]
             for j in range(Q_PER_KV)], axis=0)
        kcat = jnp.concatenate([kp_ref[0][:, kv], kc_ref[0][:, kv], kn_ref[0][:, kv]], axis=0)
        vcat = jnp.concatenate([vp_ref[0][:, kv], vc_ref[0][:, kv], vn_ref[0][:, kv]], axis=0)
        s = lax.dot_general(qs, kcat, (((1,), (1,)), ((), ())), preferred_element_type=F32)
        s = s + bias_ref[g] + edge
        sink = sink_ref[g]
        m = jnp.maximum(jnp.max(s, axis=-1, keepdims=True), sink)
        p = jnp.exp(s - m)
        denom = jnp.sum(p, axis=-1, keepdims=True) + jnp.exp(sink - m)
        o = jnp.dot(p.astype(BF16), vcat, preferred_element_type=F32) / denom
        heads += [o[j * BLOCK:(j + 1) * BLOCK] for j in range(Q_PER_KV)]
    y = jnp.concatenate(heads, axis=1)
    ms = jnp.mean(y * y, axis=-1, keepdims=True)
    o_ref[0] = (y * lax.rsqrt(ms + RMS_EPS) * g_ref[...]).astype(BF16)


def _attention(q, k, v, bias, sink_col, attn_g):
    b, l, _ = q.shape
    nb = l // BLOCK
    kv_spec = lambda f: pl.BlockSpec((1, BLOCK, KV_WIDTH), f)
    prev = lambda bi, i: (bi, jnp.maximum(i - 1, 0), 0)
    cur = lambda bi, i: (bi, i, 0)
    nxt = lambda bi, i: (bi, jnp.minimum(i + 1, nb - 1), 0)
    const3 = lambda bi, i: (0, 0, 0)
    return pl.pallas_call(
        functools.partial(_attn_kernel, nb=nb),
        grid=(b, nb),
        in_specs=[
            pl.BlockSpec((1, BLOCK, ATTN_WIDTH), cur),
            kv_spec(prev), kv_spec(cur), kv_spec(nxt),
            kv_spec(prev), kv_spec(cur), kv_spec(nxt),
            pl.BlockSpec(bias.shape, const3),
            pl.BlockSpec(sink_col.shape, const3),
            pl.BlockSpec((1, ATTN_WIDTH), lambda bi, i: (0, 0)),
        ],
        out_specs=pl.BlockSpec((1, BLOCK, ATTN_WIDTH), cur),
        out_shape=jax.ShapeDtypeStruct((b, l, ATTN_WIDTH), BF16),
        compiler_params=_params("parallel", "parallel"),
        name="banded_attention",
    )(q, k, k, k, v, v, v, bias, sink_col, attn_g)


def _alibi_bias():
    qi = jnp.arange(BLOCK)[:, None]
    sj = jnp.arange(3 * BLOCK)[None, :]
    rel = sj - BLOCK - qi
    dist = jnp.abs(rel).astype(F32)
    slopes = jnp.asarray([2.0 ** (-8.0 * (h + 1) / N_Q_HEADS) for h in range(N_Q_HEADS)], F32)
    bias = jnp.where((jnp.abs(rel) <= BLOCK)[None], -slopes[:, None, None] * dist[None], MASKED)
    return bias.reshape(N_KV_HEADS, Q_PER_KV * BLOCK, 3 * BLOCK)


def _lru_kernel(x_ref, xp_ref, xn_ref, cw_ref, cb_ref, wa_ref, wx_ref, ba_ref, bx_ref, lam_ref,
                h_ref, carry_ref, *, nc, lc, reverse):
    c = pl.program_id(1)
    cc = nc - 1 - c if reverse else c

    @pl.when(c == 0)
    def _():
        carry_ref[...] = jnp.zeros_like(carry_ref)

    x = x_ref[0]
    row8 = lax.broadcasted_iota(jnp.int32, (SUBLANES, 1), 0)
    prev8 = jnp.where(cc > 0, xp_ref[0], 0.0)
    next8 = jnp.where(cc < nc - 1, xn_ref[0], 0.0)

    def shifted_down(k):
        r = pltpu.roll(x, k, 0)
        top = jnp.where(row8 < k, pltpu.roll(prev8, k, 0), r[:SUBLANES])
        return jnp.concatenate([top, r[SUBLANES:]], axis=0)

    r1 = pltpu.roll(x, lc - 1, 0)
    bot = jnp.where(row8 == SUBLANES - 1, pltpu.roll(next8, SUBLANES - 1, 0), r1[lc - SUBLANES:])
    x_p1 = jnp.concatenate([r1[:lc - SUBLANES], bot], axis=0)

    w = cw_ref[...]
    xc = shifted_down(2) * w[0:1] + shifted_down(1) * w[1:2]
    xc = xc + x * w[2:3]
    xc = xc + x_p1 * w[3:4]
    xc = xc + cb_ref[...]

    xcb = xc.astype(BF16)
    r = jax.nn.sigmoid(jnp.dot(xcb, wa_ref[...], preferred_element_type=F32) + ba_ref[...])
    ig = jax.nn.sigmoid(jnp.dot(xcb, wx_ref[...], preferred_element_type=F32) + bx_ref[...])
    lam = lam_ref[...]
    softplus_neg_lam = jnp.maximum(-lam, 0.0) + jnp.log1p(jnp.exp(-jnp.abs(lam)))
    log_a = (-LRU_C * r) * softplus_neg_lam
    a = jnp.exp(log_a)
    u = jnp.sqrt(jnp.tanh(-log_a) * (1.0 + a * a)) * (ig * xc)

    row = lax.broadcasted_iota(jnp.int32, (lc, 1), 0)
    d = 1
    while d < lc:
        shift = lc - d if reverse else d
        valid = (row < lc - d) if reverse else (row >= d)
        u = u + a * jnp.where(valid, pltpu.roll(u, shift, 0), 0.0)
        a = a * jnp.where(valid, pltpu.roll(a, shift, 0), 1.0)
        d *= 2
    h = u + a * carry_ref[...]
    h_ref[0] = h
    carry_ref[...] = h[0:1] if reverse else h[lc - 1:lc]


def _lru(xr, conv_w, conv_b, wa, wx, ba, bx, lam, *, reverse):
    b, l, _ = xr.shape
    lc = LRU_CHUNK
    nc = l // lc
    per8 = lc // SUBLANES
    n8 = l // SUBLANES
    pos = (lambda c: nc - 1 - c) if reverse else (lambda c: c)
    const = lambda bi, c: (0, 0)
    vec = pl.BlockSpec((1, RNN_WIDTH), const)
    mat = pl.BlockSpec((RNN_WIDTH, RNN_WIDTH), const)
    return pl.pallas_call(
        functools.partial(_lru_kernel, nc=nc, lc=lc, reverse=reverse),
        grid=(b, nc),
        in_specs=[
            pl.BlockSpec((1, lc, RNN_WIDTH), lambda bi, c: (bi, pos(c), 0)),
            pl.BlockSpec((1, SUBLANES, RNN_WIDTH),
                         lambda bi, c: (bi, jnp.maximum(pos(c) * per8 - 1, 0), 0)),
            pl.BlockSpec((1, SUBLANES, RNN_WIDTH),
                         lambda bi, c: (bi, jnp.minimum((pos(c) + 1) * per8, n8 - 1), 0)),
            pl.BlockSpec((CONV_WIDTH, RNN_WIDTH), const), vec, mat, mat, vec, vec, vec,
        ],
        out_specs=pl.BlockSpec((1, lc, RNN_WIDTH), lambda bi, c: (bi, pos(c), 0)),
        out_shape=jax.ShapeDtypeStruct((b, l, RNN_WIDTH), F32),
        scratch_shapes=[pltpu.VMEM((1, RNN_WIDTH), F32)],
        compiler_params=_params("parallel", "arbitrary"),
        name="rg_lru_bwd" if reverse else "rg_lru_fwd",
    )(xr, xr, xr, conv_w, conv_b, wa, wx, ba, bx, lam)


def _block_diag(w):
    eye = jnp.eye(RNN_BLOCKS, dtype=w.dtype)
    return jnp.einsum('hij,hk->hikj', w, eye).reshape(RNN_WIDTH, RNN_WIDTH)


def _mix_out_kernel(attn_ref, hf_ref, hb_ref, gate_ref, x_ref, wo_ref, rg_ref, g1_ref, b1_ref,
                    wr_ref, h_ref, hb16_ref, aff_ref):
    gt = gate_ref[...]
    gelu = 0.5 * gt * (1.0 + jnp.tanh(math.sqrt(2.0 / math.pi) * (gt + 0.044715 * (gt * gt * gt))))
    yr = (hf_ref[...] + hb_ref[...]) * gelu
    ms = jnp.mean(yr * yr, axis=-1, keepdims=True)
    yn = (yr * lax.rsqrt(ms + RMS_EPS) * rg_ref[...]).astype(BF16)
    mix = jnp.dot(attn_ref[...], wo_ref[:ATTN_WIDTH, :], preferred_element_type=F32)
    mix = mix + jnp.dot(yn, wo_ref[ATTN_WIDTH:, :], preferred_element_type=F32)
    z = ALPHA * x_ref[...] + mix
    mu = jnp.mean(z, axis=-1, keepdims=True)
    zc = z - mu
    var = jnp.mean(zc * zc, axis=-1, keepdims=True)
    h = zc * lax.rsqrt(var + LN_EPS) * g1_ref[...] + b1_ref[...]
    h_ref[...] = h
    hb = h.astype(BF16)
    hb16_ref[...] = hb
    logits = lax.dot_general(wr_ref[...], hb, (((1,), (1,)), ((), ())),
                             preferred_element_type=F32)
    mx = jnp.max(logits, axis=0, keepdims=True)
    e = jnp.exp(logits - mx)
    aff_ref[...] = e / jnp.sum(e, axis=0, keepdims=True)


def _mix_out(attn, hf, hb, gate, x2, wo, rnn_g, g1, b1, wr_t):
    t = x2.shape[0]
    row = lambda w: pl.BlockSpec((ROW_TILE, w), lambda i: (i, 0))
    const = lambda i: (0, 0)
    return pl.pallas_call(
        _mix_out_kernel,
        grid=(t // ROW_TILE,),
        in_specs=[
            row(ATTN_WIDTH), row(RNN_WIDTH), row(RNN_WIDTH), row(RNN_WIDTH), row(D_MODEL),
            pl.BlockSpec((D_MODEL, D_MODEL), const),
            pl.BlockSpec((1, RNN_WIDTH), const),
            pl.BlockSpec((1, D_MODEL), const),
            pl.BlockSpec((1, D_MODEL), const),
            pl.BlockSpec((N_EXPERTS, D_MODEL), const),
        ],
        out_specs=[row(D_MODEL), row(D_MODEL), pl.BlockSpec((N_EXPERTS, ROW_TILE), lambda i: (0, i))],
        out_shape=[
            jax.ShapeDtypeStruct((t, D_MODEL), F32),
            jax.ShapeDtypeStruct((t, D_MODEL), BF16),
            jax.ShapeDtypeStruct((N_EXPERTS, t), F32),
        ],
        compiler_params=_params("parallel"),
        name="mix_out_ln1_router",
    )(attn, hf, hb, gate, x2, wo, rnn_g, g1, b1, wr_t)


def _ffn_kernel(x_ref, wg_ref, wu_ref, wd_ref, g_ref, o_ref):
    x = x_ref[0]
    acc = jnp.zeros((FFN_ROWS, D_MODEL), F32)
    for f in range(EXPERT_FF // FFN_FCHUNK):
        cols = slice(f * FFN_FCHUNK, (f + 1) * FFN_FCHUNK)
        gt = jnp.dot(x, wg_ref[0, :, cols], preferred_element_type=F32)
        up = jnp.dot(x, wu_ref[0, :, cols], preferred_element_type=F32)
        hid = (gt * jax.nn.sigmoid(gt) * up).astype(BF16)
        acc = acc + jnp.dot(hid, wd_ref[0, cols, :], preferred_element_type=F32)
    o_ref[0] = acc * g_ref[0]


def _ffn(xe, wg, wu, wd, g):
    e, cap, _ = xe.shape
    slots = lambda w: pl.BlockSpec((1, FFN_ROWS, w), lambda ei, j: (ei, j, 0))
    wspec = lambda a, b: pl.BlockSpec((1, a, b), lambda ei, j: (ei, 0, 0))
    return pl.pallas_call(
        _ffn_kernel,
        grid=(e, cap // FFN_ROWS),
        in_specs=[slots(D_MODEL), wspec(D_MODEL, EXPERT_FF), wspec(D_MODEL, EXPERT_FF),
                  wspec(EXPERT_FF, D_MODEL), slots(1)],
        out_specs=slots(D_MODEL),
        out_shape=jax.ShapeDtypeStruct((e, cap, D_MODEL), F32),
        compiler_params=_params("parallel", "arbitrary"),
        name="expert_ffn",
    )(xe, wg, wu, wd, g)


def _ln2_kernel(h_ref, f_ref, g_ref, b_ref, o_ref):
    z = ALPHA * h_ref[...] + f_ref[...]
    mu = jnp.mean(z, axis=-1, keepdims=True)
    zc = z - mu
    var = jnp.mean(zc * zc, axis=-1, keepdims=True)
    o_ref[...] = zc * lax.rsqrt(var + LN_EPS) * g_ref[...] + b_ref[...]


def _ln2(h, f, g2, b2):
    t = h.shape[0]
    row = pl.BlockSpec((ROW_TILE, D_MODEL), lambda i: (i, 0))
    vec = pl.BlockSpec((1, D_MODEL), lambda i: (0, 0))
    return pl.pallas_call(
        _ln2_kernel,
        grid=(t // ROW_TILE,),
        in_specs=[row, row, vec, vec],
        out_specs=row,
        out_shape=jax.ShapeDtypeStruct((t, D_MODEL), F32),
        compiler_params=_params("parallel"),
        name="residual_ln2",
    )(h, f, g2, b2)


def _layer(x, p):
    b, l, _ = x.shape
    t = b * l
    cap = CAPACITY_FACTOR * t // N_EXPERTS
    x2 = x.reshape(t, D_MODEL)
    q, k, v, xr, gate = _in_proj(x2, p["w_in"])
    attn = _attention(q.reshape(b, l, -1), k.reshape(b, l, -1), v.reshape(b, l, -1),
                      p["bias"], p["sink_col"], p["attn_g"])
    xr3 = xr.reshape(b, l, RNN_WIDTH)
    hs = [_lru(xr3, p["conv_w"], p["conv_b"], p["wa"][d], p["wx"][d], p["ba"][d], p["bx"][d],
               p["lam"][d], reverse=bool(d)) for d in range(2)]
    h1, h1b, aff_t = _mix_out(attn.reshape(t, -1), hs[0].reshape(t, -1), hs[1].reshape(t, -1), gate,
                              x2, p["w_out"], p["rnn_g"], p["ln1_g"], p["ln1_b"], p["wr_t"])
    g, idx = lax.top_k(aff_t, cap)
    xe = jnp.take(h1b, idx, axis=0)
    ye = _ffn(xe, p["wg"], p["wu"], p["wd"], g[..., None])
    ffn = jnp.zeros((t, D_MODEL), F32).at[idx.reshape(-1)].add(ye.reshape(-1, D_MODEL))
    return _ln2(h1, ffn, p["ln2_g"], p["ln2_b"]).reshape(b, l, D_MODEL)


def kernel(x_prompt, x_sample, w_in, attn_sink, attn_norm_g, rnn_norm_g, conv_w, conv_b, lru_w_a,
           lru_b_a, lru_w_x, lru_b_x, lru_lambda, w_out, ln1_g, ln1_b, w_router, w_gate, w_up,
           w_down, ln2_g, ln2_b):
    depth = w_in.shape[0]
    bias = _alibi_bias()
    layers = []
    for li in range(depth):
            vec = lambda a, li=li: a[li].reshape(1, -1)
            layers.append(dict(
                w_in=w_in[li].astype(BF16),
                bias=bias,
                sink_col=jnp.broadcast_to(
                    attn_sink[li].astype(F32).reshape(N_KV_HEADS, Q_PER_KV, 1, 1),
                    (N_KV_HEADS, Q_PER_KV, BLOCK, 1)).reshape(N_KV_HEADS, Q_PER_KV * BLOCK, 1),
                attn_g=vec(attn_norm_g), rnn_g=vec(rnn_norm_g),
                conv_w=conv_w[li], conv_b=vec(conv_b),
                wa=[_block_diag(lru_w_a[li, d]).astype(BF16) for d in range(2)],
                wx=[_block_diag(lru_w_x[li, d]).astype(BF16) for d in range(2)],
                ba=[lru_b_a[li, d].reshape(1, -1) for d in range(2)],
                bx=[lru_b_x[li, d].reshape(1, -1) for d in range(2)],
                lam=[lru_lambda[li, d].reshape(1, -1) for d in range(2)],
                w_out=w_out[li].astype(BF16),
                ln1_g=vec(ln1_g), ln1_b=vec(ln1_b),
                wr_t=w_router[li].T.astype(BF16),
                wg=w_gate[li].astype(BF16), wu=w_up[li].astype(BF16), wd=w_down[li].astype(BF16),
                ln2_g=vec(ln2_g), ln2_b=vec(ln2_b),
            ))
    ys = []
    for x in (x_prompt, x_sample):
        for p in layers:
            x = _layer(x, p)
        ys.append(x)
    return tuple(ys)
```

```python
import functools
import math

import jax
import jax.numpy as jnp
from jax import lax
from jax.experimental import pallas as pl
from jax.experimental.pallas import tpu as pltpu

D_MODEL = 1024
HEAD_DIM = 64
N_Q_HEADS = 8
N_KV_HEADS = 2
Q_PER_KV = N_Q_HEADS // N_KV_HEADS
ATTN_WIDTH = N_Q_HEADS * HEAD_DIM
KV_WIDTH = N_KV_HEADS * HEAD_DIM
BLOCK = 128
RNN_WIDTH = 512
RNN_BLOCKS = 8
RNN_BLOCK_W = RNN_WIDTH // RNN_BLOCKS
CONV_WIDTH = 4
LRU_C = 8.0
N_EXPERTS = 16
EXPERT_FF = 2048
CAPACITY_FACTOR = 2
ALPHA = 2.0 ** 0.25
LN_EPS = 1e-5
RMS_EPS = 1e-6
MASKED = -1e30

SUBLANES = 8
LANES = 128
VMEM_LIMIT = 56 * 1024 * 1024

ROW_TILE = 512
LRU_CHUNK = 256
FFN_ROWS = 512
FFN_FCHUNK = 512
POS_BLOCK = 2048
ROUTE_TILE = 256
STAGE_CHUNK = 256

BF16 = jnp.bfloat16
F32 = jnp.float32


def _params(*sem):
    return pltpu.CompilerParams(dimension_semantics=sem, vmem_limit_bytes=VMEM_LIMIT)


def _in_proj_kernel(x_ref, w_ref, q_ref, k_ref, v_ref, xr_ref, gate_ref):
    xb = x_ref[...].astype(BF16)

    def proj(lo, hi):
        return jnp.dot(xb, w_ref[:, lo:hi], preferred_element_type=F32)

    o = 0
    q_ref[...] = (proj(o, o + ATTN_WIDTH) * (HEAD_DIM ** -0.5)).astype(BF16)
    o += ATTN_WIDTH
    k_ref[...] = proj(o, o + KV_WIDTH).astype(BF16)
    o += KV_WIDTH
    v_ref[...] = proj(o, o + KV_WIDTH).astype(BF16)
    o += KV_WIDTH
    xr_ref[...] = proj(o, o + RNN_WIDTH)
    o += RNN_WIDTH
    gate_ref[...] = proj(o, o + RNN_WIDTH)


def _in_proj(x2, w_bf16):
    t = x2.shape[0]
    in_w = w_bf16.shape[1]
    row = lambda w: pl.BlockSpec((ROW_TILE, w), lambda i: (i, 0))
    return pl.pallas_call(
        _in_proj_kernel,
        grid=(t // ROW_TILE,),
        in_specs=[row(D_MODEL), pl.BlockSpec((D_MODEL, in_w), lambda i: (0, 0))],
        out_specs=[row(ATTN_WIDTH), row(KV_WIDTH), row(KV_WIDTH), row(RNN_WIDTH), row(RNN_WIDTH)],
        out_shape=[
            jax.ShapeDtypeStruct((t, ATTN_WIDTH), BF16),
            jax.ShapeDtypeStruct((t, KV_WIDTH), BF16),
            jax.ShapeDtypeStruct((t, KV_WIDTH), BF16),
            jax.ShapeDtypeStruct((t, RNN_WIDTH), F32),
            jax.ShapeDtypeStruct((t, RNN_WIDTH), F32),
        ],
        compiler_params=_params("parallel"),
        name="in_proj",
    )(x2, w_bf16)


def _attn_kernel(q_ref, kp_ref, kc_ref, kn_ref, vp_ref, vc_ref, vn_ref, bias_ref, sink_ref, g_ref,
                 o_ref, *, nb):
    i = pl.program_id(1)
    q = q_ref[0]
    col = lax.broadcasted_iota(jnp.int32, (1, 3 * BLOCK), 1)
    outside = ((col < BLOCK) & (i == 0)) | ((col >= 2 * BLOCK) & (i == nb - 1))
    edge = jnp.where(outside, MASKED, 0.0)
    heads = []
    for g in range(N_KV_HEADS):
        kv = slice(g * HEAD_DIM, (g + 1) * HEAD_DIM)
        qs = jnp.concatenate(
            [q[:, (Q_PER_KV * g + j) * HEAD_DIM:(Q_PER_KV * g + j + 1) * HEAD_DIM]
             for j in range(Q_PER_KV)], axis=0)
        kcat = jnp.concatenate([kp_ref[0][:, kv], kc_ref[0][:, kv], kn_ref[0][:, kv]], axis=0)
        vcat = jnp.concatenate([vp_ref[0][:, kv], vc_ref[0][:, kv], vn_ref[0][:, kv]], axis=0)
        s = lax.dot_general(qs, kcat, (((1,), (1,)), ((), ())), preferred_element_type=F32)
        s = s + bias_ref[g] + edge
        sink = sink_ref[g]
        m = jnp.maximum(jnp.max(s, axis=-1, keepdims=True), sink)
        p = jnp.exp(s - m)
        denom = jnp.sum(p, axis=-1, keepdims=True) + jnp.exp(sink - m)
        o = jnp.dot(p.astype(BF16), vcat, preferred_element_type=F32) / denom
        heads += [o[j * BLOCK:(j + 1) * BLOCK] for j in range(Q_PER_KV)]
    y = jnp.concatenate(heads, axis=1)
    ms = jnp.mean(y * y, axis=-1, keepdims=True)
    o_ref[0] = (y * lax.rsqrt(ms + RMS_EPS) * g_ref[...]).astype(BF16)


def _attention(q, k, v, bias, sink_col, attn_g):
    b, l, _ = q.shape
    nb = l // BLOCK
    kv_spec = lambda f: pl.BlockSpec((1, BLOCK, KV_WIDTH), f)
    prev = lambda bi, i: (bi, jnp.maximum(i - 1, 0), 0)
    cur = lambda bi, i: (bi, i, 0)
    nxt = lambda bi, i: (bi, jnp.minimum(i + 1, nb - 1), 0)
    const3 = lambda bi, i: (0, 0, 0)
    return pl.pallas_call(
        functools.partial(_attn_kernel, nb=nb),
        grid=(b, nb),
        in_specs=[
            pl.BlockSpec((1, BLOCK, ATTN_WIDTH), cur),
            kv_spec(prev), kv_spec(cur), kv_spec(nxt),
            kv_spec(prev), kv_spec(cur), kv_spec(nxt),
            pl.BlockSpec(bias.shape, const3),
            pl.BlockSpec(sink_col.shape, const3),
            pl.BlockSpec((1, ATTN_WIDTH), lambda bi, i: (0, 0)),
        ],
        out_specs=pl.BlockSpec((1, BLOCK, ATTN_WIDTH), cur),
        out_shape=jax.ShapeDtypeStruct((b, l, ATTN_WIDTH), BF16),
        compiler_params=_params("parallel", "parallel"),
        name="banded_attention",
    )(q, k, k, k, v, v, v, bias, sink_col, attn_g)


def _alibi_bias():
    qi = jnp.arange(BLOCK)[:, None]
    sj = jnp.arange(3 * BLOCK)[None, :]
    rel = sj - BLOCK - qi
    dist = jnp.abs(rel).astype(F32)
    slopes = jnp.asarray([2.0 ** (-8.0 * (h + 1) / N_Q_HEADS) for h in range(N_Q_HEADS)], F32)
    bias = jnp.where((jnp.abs(rel) <= BLOCK)[None], -slopes[:, None, None] * dist[None], MASKED)
    return bias.reshape(N_KV_HEADS, Q_PER_KV * BLOCK, 3 * BLOCK)


def _lru_kernel(x_ref, xp_ref, xn_ref, cw_ref, cb_ref, wa_ref, wx_ref, ba_ref, bx_ref, lam_ref,
                h_ref, carry_ref, *, nc, lc, reverse):
    c = pl.program_id(1)
    cc = nc - 1 - c if reverse else c

    @pl.when(c == 0)
    def _():
        carry_ref[...] = jnp.zeros_like(carry_ref)

    x = x_ref[0]
    row8 = lax.broadcasted_iota(jnp.int32, (SUBLANES, 1), 0)
    prev8 = jnp.where(cc > 0, xp_ref[0], 0.0)
    next8 = jnp.where(cc < nc - 1, xn_ref[0], 0.0)

    def shifted_down(k):
        r = pltpu.roll(x, k, 0)
        top = jnp.where(row8 < k, pltpu.roll(prev8, k, 0), r[:SUBLANES])
        return jnp.concatenate([top, r[SUBLANES:]], axis=0)

    r1 = pltpu.roll(x, lc - 1, 0)
    bot = jnp.where(row8 == SUBLANES - 1, pltpu.roll(next8, SUBLANES - 1, 0), r1[lc - SUBLANES:])
    x_p1 = jnp.concatenate([r1[:lc - SUBLANES], bot], axis=0)

    w = cw_ref[...]
    xc = shifted_down(2) * w[0:1] + shifted_down(1) * w[1:2]
    xc = xc + x * w[2:3]
    xc = xc + x_p1 * w[3:4]
    xc = xc + cb_ref[...]

    xcb = xc.astype(BF16)
    r = jax.nn.sigmoid(jnp.dot(xcb, wa_ref[...], preferred_element_type=F32) + ba_ref[...])
    ig = jax.nn.sigmoid(jnp.dot(xcb, wx_ref[...], preferred_element_type=F32) + bx_ref[...])
    lam = lam_ref[...]
    softplus_neg_lam = jnp.maximum(-lam, 0.0) + jnp.log1p(jnp.exp(-jnp.abs(lam)))
    log_a = (-LRU_C * r) * softplus_neg_lam
    a = jnp.exp(log_a)
    u = jnp.sqrt(jnp.tanh(-log_a) * (1.0 + a * a)) * (ig * xc)

    row = lax.broadcasted_iota(jnp.int32, (lc, 1), 0)
    d = 1
    while d < lc:
        shift = lc - d if reverse else d
        valid = (row < lc - d) if reverse else (row >= d)
        u = u + a * jnp.where(valid, pltpu.roll(u, shift, 0), 0.0)
        a = a * jnp.where(valid, pltpu.roll(a, shift, 0), 1.0)
        d *= 2
    h = u + a * carry_ref[...]
    h_ref[0] = h
    carry_ref[...] = h[0:1] if reverse else h[lc - 1:lc]


def _lru(xr, conv_w, conv_b, wa, wx, ba, bx, lam, *, reverse):
    b, l, _ = xr.shape
    lc = LRU_CHUNK
    nc = l // lc
    per8 = lc // SUBLANES
    n8 = l // SUBLANES
    pos = (lambda c: nc - 1 - c) if reverse else (lambda c: c)
    const = lambda bi, c: (0, 0)
    vec = pl.BlockSpec((1, RNN_WIDTH), const)
    mat = pl.BlockSpec((RNN_WIDTH, RNN_WIDTH), const)
    return pl.pallas_call(
        functools.partial(_lru_kernel, nc=nc, lc=lc, reverse=reverse),
        grid=(b, nc),
        in_specs=[
            pl.BlockSpec((1, lc, RNN_WIDTH), lambda bi, c: (bi, pos(c), 0)),
            pl.BlockSpec((1, SUBLANES, RNN_WIDTH),
                         lambda bi, c: (bi, jnp.maximum(pos(c) * per8 - 1, 0), 0)),
            pl.BlockSpec((1, SUBLANES, RNN_WIDTH),
                         lambda bi, c: (bi, jnp.minimum((pos(c) + 1) * per8, n8 - 1), 0)),
            pl.BlockSpec((CONV_WIDTH, RNN_WIDTH), const), vec, mat, mat, vec, vec, vec,
        ],
        out_specs=pl.BlockSpec((1, lc, RNN_WIDTH), lambda bi, c: (bi, pos(c), 0)),
        out_shape=jax.ShapeDtypeStruct((b, l, RNN_WIDTH), F32),
        scratch_shapes=[pltpu.VMEM((1, RNN_WIDTH), F32)],
        compiler_params=_params("parallel", "arbitrary"),
        name="rg_lru_bwd" if reverse else "rg_lru_fwd",
    )(xr, xr, xr, conv_w, conv_b, wa, wx, ba, bx, lam)


def _block_diag(w):
    eye = jnp.eye(RNN_BLOCKS, dtype=w.dtype)
    return jnp.einsum('hij,hk->hikj', w, eye).reshape(RNN_WIDTH, RNN_WIDTH)


def _mix_out_kernel(attn_ref, hf_ref, hb_ref, gate_ref, x_ref, wo_ref, rg_ref, g1_ref, b1_ref,
                    wr_ref, h_ref, hb16_ref, aff_ref):
    gt = gate_ref[...]
    gelu = 0.5 * gt * (1.0 + jnp.tanh(math.sqrt(2.0 / math.pi) * (gt + 0.044715 * (gt * gt * gt))))
    yr = (hf_ref[...] + hb_ref[...]) * gelu
    ms = jnp.mean(yr * yr, axis=-1, keepdims=True)
    yn = (yr * lax.rsqrt(ms + RMS_EPS) * rg_ref[...]).astype(BF16)
    mix = jnp.dot(attn_ref[...], wo_ref[:ATTN_WIDTH, :], preferred_element_type=F32)
    mix = mix + jnp.dot(yn, wo_ref[ATTN_WIDTH:, :], preferred_element_type=F32)
    z = ALPHA * x_ref[...] + mix
    mu = jnp.mean(z, axis=-1, keepdims=True)
    zc = z - mu
    var = jnp.mean(zc * zc, axis=-1, keepdims=True)
    h = zc * lax.rsqrt(var + LN_EPS) * g1_ref[...] + b1_ref[...]
    h_ref[...] = h
    hb = h.astype(BF16)
    hb16_ref[...] = hb
    logits = lax.dot_general(wr_ref[...], hb, (((1,), (1,)), ((), ())),
                             preferred_element_type=F32)
    mx = jnp.max(logits, axis=0, keepdims=True)
    e = jnp.exp(logits - mx)
    aff_ref[...] = e / jnp.sum(e, axis=0, keepdims=True)


def _mix_out(attn, hf, hb, gate, x2, wo, rnn_g, g1, b1, wr_t):
    t = x2.shape[0]
    row = lambda w: pl.BlockSpec((ROW_TILE, w), lambda i: (i, 0))
    const = lambda i: (0, 0)
    return pl.pallas_call(
        _mix_out_kernel,
        grid=(t // ROW_TILE,),
        in_specs=[
            row(ATTN_WIDTH), row(RNN_WIDTH), row(RNN_WIDTH), row(RNN_WIDTH), row(D_MODEL),
            pl.BlockSpec((D_MODEL, D_MODEL), const),
            pl.BlockSpec((1, RNN_WIDTH), const),
            pl.BlockSpec((1, D_MODEL), const),
            pl.BlockSpec((1, D_MODEL), const),
            pl.BlockSpec((N_EXPERTS, D_MODEL), const),
        ],
        out_specs=[row(D_MODEL), row(D_MODEL), pl.BlockSpec((N_EXPERTS, ROW_TILE), lambda i: (0, i))],
        out_shape=[
            jax.ShapeDtypeStruct((t, D_MODEL), F32),
            jax.ShapeDtypeStruct((t, D_MODEL), BF16),
            jax.ShapeDtypeStruct((N_EXPERTS, t), F32),
        ],
        compiler_params=_params("parallel"),
        name="mix_out_ln1_router",
    )(attn, hf, hb, gate, x2, wo, rnn_g, g1, b1, wr_t)


def _load_rows(ref, start, n, lead=()):
    return jnp.concatenate(
        [ref[lead + (pl.ds(start * SUBLANES + j, n, stride=SUBLANES), slice(None))]
         for j in range(D_MODEL // LANES)], axis=1)


def _store_rows(ref, start, rows, lead=()):
    n = rows.shape[0]
    for j in range(D_MODEL // LANES):
        ref[lead + (pl.ds(start * SUBLANES + j, n, stride=SUBLANES), slice(None))] = (
            rows[:, j * LANES:(j + 1) * LANES])


def _ffn_kernel(x_ref, wg_ref, wu_ref, wd_ref, o_ref):
    x = _load_rows(x_ref, 0, FFN_ROWS).astype(BF16)
    acc = jnp.zeros((FFN_ROWS, D_MODEL), F32)
    for f in range(EXPERT_FF // FFN_FCHUNK):
        cols = slice(f * FFN_FCHUNK, (f + 1) * FFN_FCHUNK)
        gt = jnp.dot(x, wg_ref[0, :, cols], preferred_element_type=F32)
        up = jnp.dot(x, wu_ref[0, :, cols], preferred_element_type=F32)
        hid = (gt * jax.nn.sigmoid(gt) * up).astype(BF16)
        acc = acc + jnp.dot(hid, wd_ref[0, cols, :], preferred_element_type=F32)
    _store_rows(o_ref, 0, acc)


def _ffn(xe, wg, wu, wd, cap):
    e = wg.shape[0]
    per_e = cap // FFN_ROWS
    slots = pl.BlockSpec((FFN_ROWS * SUBLANES, LANES), lambda ei, j: (ei * per_e + j, 0))
    wspec = lambda a, b: pl.BlockSpec((1, a, b), lambda ei, j: (ei, 0, 0))
    return pl.pallas_call(
        _ffn_kernel,
        grid=(e, per_e),
        in_specs=[slots, wspec(D_MODEL, EXPERT_FF), wspec(D_MODEL, EXPERT_FF),
                  wspec(EXPERT_FF, D_MODEL)],
        out_specs=slots,
        out_shape=jax.ShapeDtypeStruct(xe.shape, F32),
        compiler_params=_params("parallel", "arbitrary"),
        name="expert_ffn",
    )(xe, wg, wu, wd)


def _threshold_kernel(aff_ref, thr_ref, need_ref, *, cap):
    def count(mask):
        return jnp.sum(jnp.where(mask, 1.0, 0.0), axis=1, keepdims=True)

    def body(b, thr):
        cand = thr | jnp.left_shift(jnp.int32(1), 30 - b)
        bits = pltpu.bitcast(aff_ref[...], jnp.int32)
        return jnp.where(count(bits >= cand) >= cap, cand, thr)

    thr = lax.fori_loop(0, 31, body, jnp.zeros((N_EXPERTS, 1), jnp.int32))
    bits = pltpu.bitcast(aff_ref[...], jnp.int32)
    thr_ref[...] = thr
    need_ref[...] = cap - count(bits > thr)


def _positions_kernel(aff_ref, thr_ref, need_ref, pos_ref, tie_run_ref, sel_run_ref, *, block):
    @pl.when(pl.program_id(0) == 0)
    def _():
        tie_run_ref[...] = jnp.zeros_like(tie_run_ref)
        sel_run_ref[...] = jnp.zeros_like(sel_run_ref)

    thr = thr_ref[...]
    need = need_ref[...]
    r = lax.broadcasted_iota(jnp.int32, (LANES, LANES), 0)
    c = lax.broadcasted_iota(jnp.int32, (LANES, LANES), 1)
    before = jnp.where(r < c, 1.0, 0.0).astype(BF16)
    tie_run = tie_run_ref[...]
    sel_run = sel_run_ref[...]
    for s in range(block // LANES):
        lanes = slice(s * LANES, (s + 1) * LANES)
        bits = pltpu.bitcast(aff_ref[:, lanes], jnp.int32)
        tie = jnp.where(bits == thr, 1.0, 0.0)
        tie_before = jnp.dot(tie.astype(BF16), before, preferred_element_type=F32) + tie_run
        sel = (bits > thr) | ((bits == thr) & (tie_before < need))
        picked = jnp.where(sel, 1.0, 0.0)
        sel_before = jnp.dot(picked.astype(BF16), before, preferred_element_type=F32) + sel_run
        pos_ref[:, lanes] = jnp.where(sel, sel_before.astype(jnp.int32), -1)
        tie_run = tie_run + jnp.sum(tie, axis=1, keepdims=True)
        sel_run = sel_run + jnp.sum(picked, axis=1, keepdims=True)
    tie_run_ref[...] = tie_run
    sel_run_ref[...] = sel_run


def _route(aff_t, cap):
    e, t = aff_t.shape
    col = jax.ShapeDtypeStruct((e, 1), jnp.int32)
    thr, need = pl.pallas_call(
        functools.partial(_threshold_kernel, cap=cap),
        out_shape=[col, jax.ShapeDtypeStruct((e, 1), F32)],
        compiler_params=_params(),
        name="route_threshold",
    )(aff_t)
    block = min(POS_BLOCK, t)
    cspec = pl.BlockSpec((e, 1), lambda i: (0, 0))
    return pl.pallas_call(
        functools.partial(_positions_kernel, block=block),
        grid=(t // block,),
        in_specs=[pl.BlockSpec((e, block), lambda i: (0, i)), cspec, cspec],
        out_specs=pl.BlockSpec((e, block), lambda i: (0, i)),
        out_shape=jax.ShapeDtypeStruct((e, t), jnp.int32),
        scratch_shapes=[pltpu.VMEM((e, 1), F32), pltpu.VMEM((e, 1), F32)],
        compiler_params=_params("arbitrary"),
        name="route_positions",
    )(aff_t, thr, need)


def _tile_tables(pos, tile):
    e, t = pos.shape
    n = jnp.sum((pos >= 0).reshape(e, t // tile, tile), axis=-1, dtype=jnp.int32)
    s0 = jnp.cumsum(n, axis=1) - n
    off = jnp.cumsum(n, axis=0) - n
    shift = (off - s0).T.reshape(t // tile, e, 1)
    return n.reshape(-1), s0.reshape(-1), shift


def _tile_counts(n_ref, tile, ntile):
    counts = [n_ref[e * ntile + tile] for e in range(N_EXPERTS)]
    offs = [jnp.int32(0)]
    for c in counts:
        offs.append(offs[-1] + c)
    return counts, offs


def _one_hot_t(stagepos, kbase, values=None):
    k = lax.broadcasted_iota(jnp.int32, (STAGE_CHUNK, stagepos.shape[1]), 0) + kbase
    pt = jnp.zeros(k.shape, F32)
    for e in range(N_EXPERTS):
        hit = stagepos[e:e + 1, :] == k
        pt = jnp.where(hit, 1.0 if values is None else values[e:e + 1, :], pt)
    return pt.astype(BF16)


def _rows_copy(src, dst, sem, src_row, dst_row, n):
    return pltpu.make_async_copy(
        src.at[pl.ds(pl.multiple_of(src_row * SUBLANES, SUBLANES), n * SUBLANES)],
        dst.at[pl.ds(pl.multiple_of(dst_row * SUBLANES, SUBLANES), n * SUBLANES)], sem)


def _dispatch_kernel(n_ref, s0_ref, pos_ref, shift_ref, h_ref, xe_hbm, stage_ref, sem, *, ntile, cap):
    i = pl.program_id(0)
    slot = i % 2

    def wait_writes(tile, slot):
        _, offs = _tile_counts(n_ref, tile, ntile)

        @pl.when(offs[-1] > 0)
        def _():
            _rows_copy(stage_ref.at[slot], xe_hbm, sem.at[slot], 0, 0, offs[-1]).wait()

    @pl.when(i >= 2)
    def _():
        wait_writes(i - 2, slot)

    counts, offs = _tile_counts(n_ref, i, ntile)
    pos = pos_ref[...]
    stagepos = jnp.where(pos >= 0, pos + shift_ref[0], -1)

    def chunk(c, carry):
        kbase = pl.multiple_of(c * STAGE_CHUNK, STAGE_CHUNK)
        rows = jnp.dot(_one_hot_t(stagepos, kbase), h_ref[...], preferred_element_type=F32)
        _store_rows(stage_ref, kbase, rows, lead=(slot,))
        return carry

    lax.fori_loop(0, (offs[-1] + STAGE_CHUNK - 1) // STAGE_CHUNK, chunk, 0)

    for e in range(N_EXPERTS):
        @pl.when(counts[e] > 0)
        def _(e=e):
            _rows_copy(stage_ref.at[slot], xe_hbm, sem.at[slot], offs[e],
                       e * cap + s0_ref[e * ntile + i], counts[e]).start()

    @pl.when(i == ntile - 1)
    def _():
        wait_writes(i, slot)
        if ntile > 1:
            wait_writes(i - 1, 1 - slot)


def _dispatch(n, s0, pos, shift, h1b, cap):
    e, t = pos.shape
    tile = ROUTE_TILE
    ntile = t // tile
    return pl.pallas_call(
        functools.partial(_dispatch_kernel, ntile=ntile, cap=cap),
        grid_spec=pltpu.PrefetchScalarGridSpec(
            num_scalar_prefetch=2,
            grid=(ntile,),
            in_specs=[
                pl.BlockSpec((e, tile), lambda i, *_: (0, i)),
                pl.BlockSpec((1, e, 1), lambda i, *_: (i, 0, 0)),
                pl.BlockSpec((tile, D_MODEL), lambda i, *_: (i, 0)),
            ],
            out_specs=pl.BlockSpec(memory_space=pl.ANY),
            scratch_shapes=[pltpu.VMEM((2, e * tile * SUBLANES, LANES), F32),
                            pltpu.SemaphoreType.DMA((2,))],
        ),
        out_shape=jax.ShapeDtypeStruct((e * cap * SUBLANES, LANES), F32),
        compiler_params=_params("arbitrary"),
        name="route_dispatch",
    )(n, s0, pos, shift, h1b)


def _combine_kernel(n_ref, s0_ref, pos_ref, aff_ref, shift_ref, h_ref, g_ref, b_ref, ye_hbm, o_ref,
                    stage_ref, sem, *, ntile, cap):
    i = pl.program_id(0)
    slot = i % 2

    def fetch(tile, slot):
        counts, offs = _tile_counts(n_ref, tile, ntile)
        for e in range(N_EXPERTS):
            @pl.when(counts[e] > 0)
            def _(e=e):
                _rows_copy(ye_hbm, stage_ref.at[slot], sem.at[slot],
                           e * cap + s0_ref[e * ntile + tile], offs[e], counts[e]).start()

    @pl.when(i == 0)
    def _():
        stage_ref[...] = jnp.zeros_like(stage_ref)
        fetch(0, 0)

    @pl.when(i + 1 < ntile)
    def _():
        fetch(i + 1, 1 - slot)

    _, offs = _tile_counts(n_ref, i, ntile)

    @pl.when(offs[-1] > 0)
    def _():
        _rows_copy(ye_hbm, stage_ref.at[slot], sem.at[slot], 0, 0, offs[-1]).wait()

    pos = pos_ref[...]
    stagepos = jnp.where(pos >= 0, pos + shift_ref[0], -1)
    gates = aff_ref[...]

    def chunk(c, acc):
        kbase = pl.multiple_of(c * STAGE_CHUNK, STAGE_CHUNK)
        ye = _load_rows(stage_ref, kbase, STAGE_CHUNK, lead=(slot,)).astype(BF16)
        return acc + lax.dot_general(_one_hot_t(stagepos, kbase, gates), ye,
                                     (((0,), (0,)), ((), ())), preferred_element_type=F32)

    ffn = lax.fori_loop(0, (offs[-1] + STAGE_CHUNK - 1) // STAGE_CHUNK, chunk,
                        jnp.zeros(h_ref.shape, F32))
    z = ALPHA * h_ref[...] + ffn
    mu = jnp.mean(z, axis=-1, keepdims=True)
    zc = z - mu
    var = jnp.mean(zc * zc, axis=-1, keepdims=True)
    o_ref[...] = zc * lax.rsqrt(var + LN_EPS) * g_ref[...] + b_ref[...]


def _combine(n, s0, pos, aff_t, shift, h1, g2, b2, ye, cap):
    e, t = pos.shape
    tile = ROUTE_TILE
    ntile = t // tile
    lanes = pl.BlockSpec((e, tile), lambda i, *_: (0, i))
    vec = pl.BlockSpec((1, D_MODEL), lambda i, *_: (0, 0))
    row = pl.BlockSpec((tile, D_MODEL), lambda i, *_: (i, 0))
    return pl.pallas_call(
        functools.partial(_combine_kernel, ntile=ntile, cap=cap),
        grid_spec=pltpu.PrefetchScalarGridSpec(
            num_scalar_prefetch=2,
            grid=(ntile,),
            in_specs=[lanes, lanes, pl.BlockSpec((1, e, 1), lambda i, *_: (i, 0, 0)), row, vec, vec,
                      pl.BlockSpec(memory_space=pl.ANY)],
            out_specs=row,
            scratch_shapes=[pltpu.VMEM((2, e * tile * SUBLANES, LANES), F32),
                            pltpu.SemaphoreType.DMA((2,))],
        ),
        out_shape=jax.ShapeDtypeStruct((t, D_MODEL), F32),
        compiler_params=_params("arbitrary"),
        name="route_combine_ln2",
    )(n, s0, pos, aff_t, shift, h1, g2, b2, ye)


def _layer(x, p):
    b, l, _ = x.shape
    t = b * l
    cap = CAPACITY_FACTOR * t // N_EXPERTS
    x2 = x.reshape(t, D_MODEL)
    q, k, v, xr, gate = _in_proj(x2, p["w_in"])
    attn = _attention(q.reshape(b, l, -1), k.reshape(b, l, -1), v.reshape(b, l, -1),
                      p["bias"], p["sink_col"], p["attn_g"])
    xr3 = xr.reshape(b, l, RNN_WIDTH)
    hs = [_lru(xr3, p["conv_w"], p["conv_b"], p["wa"][d], p["wx"][d], p["ba"][d], p["bx"][d],
               p["lam"][d], reverse=bool(d)) for d in range(2)]
    h1, h1b, aff_t = _mix_out(attn.reshape(t, -1), hs[0].reshape(t, -1), hs[1].reshape(t, -1), gate,
                              x2, p["w_out"], p["rnn_g"], p["ln1_g"], p["ln1_b"], p["wr_t"])
    pos = _route(aff_t, cap)
    n, s0, shift = _tile_tables(pos, ROUTE_TILE)
    xe = _dispatch(n, s0, pos, shift, h1b, cap)
    ye = _ffn(xe, p["wg"], p["wu"], p["wd"], cap)
    out = _combine(n, s0, pos, aff_t, shift, h1, p["ln2_g"], p["ln2_b"], ye, cap)
    return out.reshape(b, l, D_MODEL)


def kernel(x_prompt, x_sample, w_in, attn_sink, attn_norm_g, rnn_norm_g, conv_w, conv_b, lru_w_a,
           lru_b_a, lru_w_x, lru_b_x, lru_lambda, w_out, ln1_g, ln1_b, w_router, w_gate, w_up,
           w_down, ln2_g, ln2_b):
    depth = w_in.shape[0]
    bias = _alibi_bias()
    layers = []
    for li in range(depth):
            vec = lambda a, li=li: a[li].reshape(1, -1)
            layers.append(dict(
                w_in=w_in[li].astype(BF16),
                bias=bias,
                sink_col=jnp.broadcast_to(
                    attn_sink[li].astype(F32).reshape(N_KV_HEADS, Q_PER_KV, 1, 1),
                    (N_KV_HEADS, Q_PER_KV, BLOCK, 1)).reshape(N_KV_HEADS, Q_PER_KV * BLOCK, 1),
                attn_g=vec(attn_norm_g), rnn_g=vec(rnn_norm_g),
                conv_w=conv_w[li], conv_b=vec(conv_b),
                wa=[_block_diag(lru_w_a[li, d]).astype(BF16) for d in range(2)],
                wx=[_block_diag(lru_w_x[li, d]).astype(BF16) for d in range(2)],
                ba=[lru_b_a[li, d].reshape(1, -1) for d in range(2)],
                bx=[lru_b_x[li, d].reshape(1, -1) for d in range(2)],
                lam=[lru_lambda[li, d].reshape(1, -1) for d in range(2)],
                w_out=w_out[li].astype(BF16),
                ln1_g=vec(ln1_g), ln1_b=vec(ln1_b),
                wr_t=w_router[li].T.astype(BF16),
                wg=w_gate[li].astype(BF16), wu=w_up[li].astype(BF16), wd=w_down[li].astype(BF16),
                ln2_g=vec(ln2_g), ln2_b=vec(ln2_b),
            ))
    ys = []
    for x in (x_prompt, x_sample):
        for p in layers:
            x = _layer(x, p)
        ys.append(x)
    return tuple(ys)
```

```python
import functools
import math

import jax
import jax.numpy as jnp
from jax import lax
from jax.experimental import pallas as pl
from jax.experimental.pallas import tpu as pltpu

D_MODEL = 1024
HEAD_DIM = 64
N_Q_HEADS = 8
N_KV_HEADS = 2
Q_PER_KV = N_Q_HEADS // N_KV_HEADS
ATTN_WIDTH = N_Q_HEADS * HEAD_DIM
KV_WIDTH = N_KV_HEADS * HEAD_DIM
BLOCK = 128
RNN_WIDTH = 512
RNN_BLOCKS = 8
RNN_BLOCK_W = RNN_WIDTH // RNN_BLOCKS
CONV_WIDTH = 4
LRU_C = 8.0
N_EXPERTS = 16
EXPERT_FF = 2048
CAPACITY_FACTOR = 2
ALPHA = 2.0 ** 0.25
LN_EPS = 1e-5
RMS_EPS = 1e-6
MASKED = -1e30

SUBLANES = 8
LANES = 128
VMEM_LIMIT = 56 * 1024 * 1024

ROW_TILE = 512
LRU_CHUNK = 256
LRU_BATCH = SUBLANES
LRU_UNROLL = 8
FFN_ROWS = 512
FFN_FCHUNK = 512
POS_BLOCK = 2048
ROUTE_TILE = 256
STAGE_CHUNK = 256

BF16 = jnp.bfloat16
F32 = jnp.float32


def _params(*sem):
    return pltpu.CompilerParams(dimension_semantics=sem, vmem_limit_bytes=VMEM_LIMIT)


def _in_proj_kernel(x_ref, w_ref, q_ref, k_ref, v_ref, xr_ref, gate_ref):
    xb = x_ref[...].astype(BF16)

    def proj(lo, hi):
        return jnp.dot(xb, w_ref[:, lo:hi], preferred_element_type=F32)

    o = 0
    q_ref[...] = (proj(o, o + ATTN_WIDTH) * (HEAD_DIM ** -0.5)).astype(BF16)
    o += ATTN_WIDTH
    k_ref[...] = proj(o, o + KV_WIDTH).astype(BF16)
    o += KV_WIDTH
    v_ref[...] = proj(o, o + KV_WIDTH).astype(BF16)
    o += KV_WIDTH
    xr_ref[...] = proj(o, o + RNN_WIDTH)
    o += RNN_WIDTH
    gate_ref[...] = proj(o, o + RNN_WIDTH)


def _in_proj(x2, w_bf16):
    t = x2.shape[0]
    in_w = w_bf16.shape[1]
    row = lambda w: pl.BlockSpec((ROW_TILE, w), lambda i: (i, 0))
    return pl.pallas_call(
        _in_proj_kernel,
        grid=(t // ROW_TILE,),
        in_specs=[row(D_MODEL), pl.BlockSpec((D_MODEL, in_w), lambda i: (0, 0))],
        out_specs=[row(ATTN_WIDTH), row(KV_WIDTH), row(KV_WIDTH), row(RNN_WIDTH), row(RNN_WIDTH)],
        out_shape=[
            jax.ShapeDtypeStruct((t, ATTN_WIDTH), BF16),
            jax.ShapeDtypeStruct((t, KV_WIDTH), BF16),
            jax.ShapeDtypeStruct((t, KV_WIDTH), BF16),
            jax.ShapeDtypeStruct((t, RNN_WIDTH), F32),
            jax.ShapeDtypeStruct((t, RNN_WIDTH), F32),
        ],
        compiler_params=_params("parallel"),
        name="in_proj",
    )(x2, w_bf16)


def _attn_kernel(q_ref, kp_ref, kc_ref, kn_ref, vp_ref, vc_ref, vn_ref, bias_ref, sink_ref, g_ref,
                 o_ref):
    q = q_ref[0]
    k = jnp.concatenate([kp_ref[0], kc_ref[0], kn_ref[0]], axis=0)
    v = jnp.concatenate([vp_ref[0], vc_ref[0], vn_ref[0]], axis=0)
    v_t = v.astype(F32).T
    pad_rows = 2 * SUBLANES
    ones_row = jnp.where(
        lax.broadcasted_iota(jnp.int32, (pad_rows, 3 * BLOCK), 0) == 0, 1.0, 0.0)
    heads = []
    for g in range(N_KV_HEADS):
        qs = jnp.concatenate(
            [q[:, (Q_PER_KV * g + j) * HEAD_DIM:(Q_PER_KV * g + j + 1) * HEAD_DIM]
             for j in range(Q_PER_KV)], axis=0)
        kg = k[:, g * HEAD_DIM:(g + 1) * HEAD_DIM]
        s = lax.dot_general(kg, qs, (((1,), (1,)), ((), ())), preferred_element_type=F32)
        s = s + bias_ref[0, g]
        sink = sink_ref[g]
        m = jnp.maximum(jnp.max(s, axis=0, keepdims=True), sink)
        p = jnp.exp(s - m).astype(BF16)
        lhs = jnp.concatenate([v_t[g * HEAD_DIM:(g + 1) * HEAD_DIM], ones_row], axis=0).astype(BF16)
        o_aug = jnp.dot(lhs, p, preferred_element_type=F32)
        denom = o_aug[HEAD_DIM:HEAD_DIM + 1] + jnp.exp(sink - m)
        o = o_aug[:HEAD_DIM] * (1.0 / denom)
        heads += [o[:, j * BLOCK:(j + 1) * BLOCK] for j in range(Q_PER_KV)]
    y_t = jnp.concatenate(heads, axis=0)
    ms = jnp.mean(y_t * y_t, axis=0, keepdims=True)
    o_ref[0] = (y_t * lax.rsqrt(ms + RMS_EPS) * g_ref[...]).T.astype(BF16)


def _attention(q, k, v, bias, sink_row, attn_g_col):
    b, l, _ = q.shape
    nb = l // BLOCK
    kv_spec = lambda f: pl.BlockSpec((1, BLOCK, KV_WIDTH), f)
    prev = lambda bi, i: (bi, jnp.maximum(i - 1, 0), 0)
    cur = lambda bi, i: (bi, i, 0)
    nxt = lambda bi, i: (bi, jnp.minimum(i + 1, nb - 1), 0)
    variant = lambda bi, i: ((i == 0).astype(jnp.int32) + 2 * (i == nb - 1).astype(jnp.int32), 0, 0, 0)
    return pl.pallas_call(
        _attn_kernel,
        grid=(b, nb),
        in_specs=[
            pl.BlockSpec((1, BLOCK, ATTN_WIDTH), cur),
            kv_spec(prev), kv_spec(cur), kv_spec(nxt),
            kv_spec(prev), kv_spec(cur), kv_spec(nxt),
            pl.BlockSpec((1,) + bias.shape[1:], variant),
            pl.BlockSpec(sink_row.shape, lambda bi, i: (0, 0, 0)),
            pl.BlockSpec(attn_g_col.shape, lambda bi, i: (0, 0)),
        ],
        out_specs=pl.BlockSpec((1, BLOCK, ATTN_WIDTH), cur),
        out_shape=jax.ShapeDtypeStruct((b, l, ATTN_WIDTH), BF16),
        compiler_params=_params("parallel", "parallel"),
        name="banded_attention",
    )(q, k, k, k, v, v, v, bias, sink_row, attn_g_col)


def _alibi_bias():
    qi = jnp.arange(BLOCK)[None, :]
    sj = jnp.arange(3 * BLOCK)[:, None]
    rel = sj - BLOCK - qi
    dist = jnp.abs(rel).astype(F32)
    slopes = jnp.asarray([2.0 ** (-8.0 * (h + 1) / N_Q_HEADS) for h in range(N_Q_HEADS)], F32)
    bias = jnp.where((jnp.abs(rel) <= BLOCK)[None], -slopes[:, None, None] * dist[None], MASKED)
    bias = bias.reshape(N_KV_HEADS, Q_PER_KV, 3 * BLOCK, BLOCK).transpose(0, 2, 1, 3)
    bias = bias.reshape(N_KV_HEADS, 3 * BLOCK, Q_PER_KV * BLOCK)
    no_prev = (sj < BLOCK)[None]
    no_next = (sj >= 2 * BLOCK)[None]
    return jnp.stack([
        bias,
        jnp.where(no_prev, MASKED, bias),
        jnp.where(no_next, MASKED, bias),
        jnp.where(no_prev | no_next, MASKED, bias),
    ])


def _sigmoid(x):
    return 0.5 * jnp.tanh(0.5 * x) + 0.5


def _lru_kernel(x_ref, xp_ref, xn_ref, cw_ref, cb_ref, wa_ref, wx_ref, ba_ref, bx_ref, lam_ref,
                h_ref, a_scr, u_scr, carry_ref, *, nc, lc, reverse):
    c = pl.program_id(1)
    cc = nc - 1 - c if reverse else c
    pitch = lc + SUBLANES
    slabs = RNN_WIDTH // LANES

    @pl.when(c == 0)
    def _():
        carry_ref[...] = jnp.zeros_like(carry_ref)

    row8 = lax.broadcasted_iota(jnp.int32, (SUBLANES, 1), 0)
    w = cw_ref[...]
    lam = lam_ref[...]
    softplus_neg_lam = jnp.maximum(-lam, 0.0) + jnp.log1p(jnp.exp(-jnp.abs(lam)))
    for b in range(LRU_BATCH):
        x = x_ref[b]
        prev8 = jnp.where(cc > 0, xp_ref[b], 0.0)
        next8 = jnp.where(cc < nc - 1, xn_ref[b], 0.0)

        def shifted_down(k):
            r = pltpu.roll(x, k, 0)
            top = jnp.where(row8 < k, pltpu.roll(prev8, k, 0), r[:SUBLANES])
            return jnp.concatenate([top, r[SUBLANES:]], axis=0)

        r1 = pltpu.roll(x, lc - 1, 0)
        bot = jnp.where(row8 == SUBLANES - 1, pltpu.roll(next8, SUBLANES - 1, 0),
                        r1[lc - SUBLANES:])
        x_p1 = jnp.concatenate([r1[:lc - SUBLANES], bot], axis=0)

        xc = shifted_down(2) * w[0:1] + shifted_down(1) * w[1:2]
        xc = xc + x * w[2:3]
        xc = xc + x_p1 * w[3:4]
        xc = xc + cb_ref[...]

        xcb = xc.astype(BF16)
        r = _sigmoid(jnp.dot(xcb, wa_ref[...], preferred_element_type=F32) + ba_ref[...])
        ig = _sigmoid(jnp.dot(xcb, wx_ref[...], preferred_element_type=F32) + bx_ref[...])
        log_a = (-LRU_C * r) * softplus_neg_lam
        a = jnp.exp(log_a)
        u = jnp.sqrt(jnp.tanh(-log_a) * (1.0 + a * a)) * (ig * xc)
        for j in range(slabs):
            a_scr[j, b * pitch:b * pitch + lc, :] = a[:, j * LANES:(j + 1) * LANES]
            u_scr[j, b * pitch:b * pitch + lc, :] = u[:, j * LANES:(j + 1) * LANES]

    def step(i, hs):
        t = lc - 1 - i if reverse else i
        out = []
        for j in range(slabs):
            rows = pl.ds(t, LRU_BATCH, stride=pitch)
            h = a_scr[j, rows, :] * hs[j] + u_scr[j, rows, :]
            u_scr[j, rows, :] = h
            out.append(h)
        return tuple(out)

    hs = lax.fori_loop(0, lc, step, tuple(carry_ref[j] for j in range(slabs)), unroll=LRU_UNROLL)
    for j in range(slabs):
        carry_ref[j] = hs[j]
        for b in range(LRU_BATCH):
            h_ref[b, :, j * LANES:(j + 1) * LANES] = u_scr[j, b * pitch:b * pitch + lc, :]


def _lru(xr, conv_w, conv_b, wa, wx, ba, bx, lam, *, reverse):
    b, l, _ = xr.shape
    lc = LRU_CHUNK
    nc = l // lc
    per8 = lc // SUBLANES
    n8 = l // SUBLANES
    pos = (lambda c: nc - 1 - c) if reverse else (lambda c: c)
    const = lambda bi, c: (0, 0)
    vec = pl.BlockSpec((1, RNN_WIDTH), const)
    mat = pl.BlockSpec((RNN_WIDTH, RNN_WIDTH), const)
    slabs = RNN_WIDTH // LANES
    scratch = pltpu.VMEM((slabs, LRU_BATCH * (lc + SUBLANES), LANES), F32)
    return pl.pallas_call(
        functools.partial(_lru_kernel, nc=nc, lc=lc, reverse=reverse),
        grid=(b // LRU_BATCH, nc),
        in_specs=[
            pl.BlockSpec((LRU_BATCH, lc, RNN_WIDTH), lambda bi, c: (bi, pos(c), 0)),
            pl.BlockSpec((LRU_BATCH, SUBLANES, RNN_WIDTH),
                         lambda bi, c: (bi, jnp.maximum(pos(c) * per8 - 1, 0), 0)),
            pl.BlockSpec((LRU_BATCH, SUBLANES, RNN_WIDTH),
                         lambda bi, c: (bi, jnp.minimum((pos(c) + 1) * per8, n8 - 1), 0)),
            pl.BlockSpec((CONV_WIDTH, RNN_WIDTH), const), vec, mat, mat, vec, vec, vec,
        ],
        out_specs=pl.BlockSpec((LRU_BATCH, lc, RNN_WIDTH), lambda bi, c: (bi, pos(c), 0)),
        out_shape=jax.ShapeDtypeStruct((b, l, RNN_WIDTH), F32),
        scratch_shapes=[scratch, scratch, pltpu.VMEM((slabs, LRU_BATCH, LANES), F32)],
        compiler_params=_params("parallel", "arbitrary"),
        name="rg_lru_bwd" if reverse else "rg_lru_fwd",
    )(xr, xr, xr, conv_w, conv_b, wa, wx, ba, bx, lam)


def _block_diag(w):
    eye = jnp.eye(RNN_BLOCKS, dtype=w.dtype)
    return jnp.einsum('hij,hk->hikj', w, eye).reshape(RNN_WIDTH, RNN_WIDTH)


def _mix_out_kernel(attn_ref, hf_ref, hb_ref, gate_ref, x_ref, wo_ref, rg_ref, g1_ref, b1_ref,
                    wr_ref, h_ref, hb16_ref, aff_ref):
    gt = gate_ref[...]
    gelu = 0.5 * gt * (1.0 + jnp.tanh(math.sqrt(2.0 / math.pi) * (gt + 0.044715 * (gt * gt * gt))))
    yr = (hf_ref[...] + hb_ref[...]) * gelu
    ms = jnp.mean(yr * yr, axis=-1, keepdims=True)
    yn = (yr * lax.rsqrt(ms + RMS_EPS) * rg_ref[...]).astype(BF16)
    mix = jnp.dot(attn_ref[...], wo_ref[:ATTN_WIDTH, :], preferred_element_type=F32)
    mix = mix + jnp.dot(yn, wo_ref[ATTN_WIDTH:, :], preferred_element_type=F32)
    z = ALPHA * x_ref[...] + mix
    mu = jnp.mean(z, axis=-1, keepdims=True)
    zc = z - mu
    var = jnp.mean(zc * zc, axis=-1, keepdims=True)
    h = zc * lax.rsqrt(var + LN_EPS) * g1_ref[...] + b1_ref[...]
    h_ref[...] = h
    hb = h.astype(BF16)
    hb16_ref[...] = hb
    logits = lax.dot_general(wr_ref[...], hb, (((1,), (1,)), ((), ())),
                             preferred_element_type=F32)
    mx = jnp.max(logits, axis=0, keepdims=True)
    e = jnp.exp(logits - mx)
    aff_ref[...] = e / jnp.sum(e, axis=0, keepdims=True)


def _mix_out(attn, hf, hb, gate, x2, wo, rnn_g, g1, b1, wr_t):
    t = x2.shape[0]
    row = lambda w: pl.BlockSpec((ROW_TILE, w), lambda i: (i, 0))
    const = lambda i: (0, 0)
    return pl.pallas_call(
        _mix_out_kernel,
        grid=(t // ROW_TILE,),
        in_specs=[
            row(ATTN_WIDTH), row(RNN_WIDTH), row(RNN_WIDTH), row(RNN_WIDTH), row(D_MODEL),
            pl.BlockSpec((D_MODEL, D_MODEL), const),
            pl.BlockSpec((1, RNN_WIDTH), const),
            pl.BlockSpec((1, D_MODEL), const),
            pl.BlockSpec((1, D_MODEL), const),
            pl.BlockSpec((N_EXPERTS, D_MODEL), const),
        ],
        out_specs=[row(D_MODEL), row(D_MODEL), pl.BlockSpec((N_EXPERTS, ROW_TILE), lambda i: (0, i))],
        out_shape=[
            jax.ShapeDtypeStruct((t, D_MODEL), F32),
            jax.ShapeDtypeStruct((t, D_MODEL), BF16),
            jax.ShapeDtypeStruct((N_EXPERTS, t), F32),
        ],
        compiler_params=_params("parallel"),
        name="mix_out_ln1_router",
    )(attn, hf, hb, gate, x2, wo, rnn_g, g1, b1, wr_t)


def _load_rows(ref, start, n, lead=()):
    return jnp.concatenate(
        [ref[lead + (pl.ds(start * SUBLANES + j, n, stride=SUBLANES), slice(None))]
         for j in range(D_MODEL // LANES)], axis=1)


def _store_rows(ref, start, rows, lead=()):
    n = rows.shape[0]
    for j in range(D_MODEL // LANES):
        ref[lead + (pl.ds(start * SUBLANES + j, n, stride=SUBLANES), slice(None))] = (
            rows[:, j * LANES:(j + 1) * LANES])


def _ffn_kernel(x_ref, wg_ref, wu_ref, wd_ref, o_ref):
    x = _load_rows(x_ref, 0, FFN_ROWS).astype(BF16)
    acc = jnp.zeros((FFN_ROWS, D_MODEL), F32)
    for f in range(EXPERT_FF // FFN_FCHUNK):
        cols = slice(f * FFN_FCHUNK, (f + 1) * FFN_FCHUNK)
        gt = jnp.dot(x, wg_ref[0, :, cols], preferred_element_type=F32)
        up = jnp.dot(x, wu_ref[0, :, cols], preferred_element_type=F32)
        hid = (gt * jax.nn.sigmoid(gt) * up).astype(BF16)
        acc = acc + jnp.dot(hid, wd_ref[0, cols, :], preferred_element_type=F32)
    _store_rows(o_ref, 0, acc)


def _ffn(xe, wg, wu, wd, cap):
    e = wg.shape[0]
    per_e = cap // FFN_ROWS
    slots = pl.BlockSpec((FFN_ROWS * SUBLANES, LANES), lambda ei, j: (ei * per_e + j, 0))
    wspec = lambda a, b: pl.BlockSpec((1, a, b), lambda ei, j: (ei, 0, 0))
    return pl.pallas_call(
        _ffn_kernel,
        grid=(e, per_e),
        in_specs=[slots, wspec(D_MODEL, EXPERT_FF), wspec(D_MODEL, EXPERT_FF),
                  wspec(EXPERT_FF, D_MODEL)],
        out_specs=slots,
        out_shape=jax.ShapeDtypeStruct(xe.shape, F32),
        compiler_params=_params("parallel", "arbitrary"),
        name="expert_ffn",
    )(xe, wg, wu, wd)


def _threshold_kernel(aff_ref, thr_ref, need_ref, *, cap):
    def count(mask):
        return jnp.sum(jnp.where(mask, 1.0, 0.0), axis=1, keepdims=True)

    def body(b, thr):
        cand = thr | jnp.left_shift(jnp.int32(1), 30 - b)
        bits = pltpu.bitcast(aff_ref[...], jnp.int32)
        return jnp.where(count(bits >= cand) >= cap, cand, thr)

    thr = lax.fori_loop(0, 31, body, jnp.zeros((N_EXPERTS, 1), jnp.int32))
    bits = pltpu.bitcast(aff_ref[...], jnp.int32)
    thr_ref[...] = thr
    need_ref[...] = cap - count(bits > thr)


def _positions_kernel(aff_ref, thr_ref, need_ref, pos_ref, tie_run_ref, sel_run_ref, *, block):
    @pl.when(pl.program_id(0) == 0)
    def _():
        tie_run_ref[...] = jnp.zeros_like(tie_run_ref)
        sel_run_ref[...] = jnp.zeros_like(sel_run_ref)

    thr = thr_ref[...]
    need = need_ref[...]
    r = lax.broadcasted_iota(jnp.int32, (LANES, LANES), 0)
    c = lax.broadcasted_iota(jnp.int32, (LANES, LANES), 1)
    before = jnp.where(r < c, 1.0, 0.0).astype(BF16)
    tie_run = tie_run_ref[...]
    sel_run = sel_run_ref[...]
    for s in range(block // LANES):
        lanes = slice(s * LANES, (s + 1) * LANES)
        bits = pltpu.bitcast(aff_ref[:, lanes], jnp.int32)
        tie = jnp.where(bits == thr, 1.0, 0.0)
        tie_before = jnp.dot(tie.astype(BF16), before, preferred_element_type=F32) + tie_run
        sel = (bits > thr) | ((bits == thr) & (tie_before < need))
        picked = jnp.where(sel, 1.0, 0.0)
        sel_before = jnp.dot(picked.astype(BF16), before, preferred_element_type=F32) + sel_run
        pos_ref[:, lanes] = jnp.where(sel, sel_before.astype(jnp.int32), -1)
        tie_run = tie_run + jnp.sum(tie, axis=1, keepdims=True)
        sel_run = sel_run + jnp.sum(picked, axis=1, keepdims=True)
    tie_run_ref[...] = tie_run
    sel_run_ref[...] = sel_run


def _route(aff_t, cap):
    e, t = aff_t.shape
    col = jax.ShapeDtypeStruct((e, 1), jnp.int32)
    thr, need = pl.pallas_call(
        functools.partial(_threshold_kernel, cap=cap),
        out_shape=[col, jax.ShapeDtypeStruct((e, 1), F32)],
        compiler_params=_params(),
        name="route_threshold",
    )(aff_t)
    block = min(POS_BLOCK, t)
    cspec = pl.BlockSpec((e, 1), lambda i: (0, 0))
    return pl.pallas_call(
        functools.partial(_positions_kernel, block=block),
        grid=(t // block,),
        in_specs=[pl.BlockSpec((e, block), lambda i: (0, i)), cspec, cspec],
        out_specs=pl.BlockSpec((e, block), lambda i: (0, i)),
        out_shape=jax.ShapeDtypeStruct((e, t), jnp.int32),
        scratch_shapes=[pltpu.VMEM((e, 1), F32), pltpu.VMEM((e, 1), F32)],
        compiler_params=_params("arbitrary"),
        name="route_positions",
    )(aff_t, thr, need)


def _tile_tables(pos, tile):
    e, t = pos.shape
    n = jnp.sum((pos >= 0).reshape(e, t // tile, tile), axis=-1, dtype=jnp.int32)
    s0 = jnp.cumsum(n, axis=1) - n
    off = jnp.cumsum(n, axis=0) - n
    shift = (off - s0).T.reshape(t // tile, e, 1)
    return n.reshape(-1), s0.reshape(-1), shift


def _tile_counts(n_ref, tile, ntile):
    counts = [n_ref[e * ntile + tile] for e in range(N_EXPERTS)]
    offs = [jnp.int32(0)]
    for c in counts:
        offs.append(offs[-1] + c)
    return counts, offs


def _one_hot_t(stagepos, kbase, values=None):
    k = lax.broadcasted_iota(jnp.int32, (STAGE_CHUNK, stagepos.shape[1]), 0) + kbase
    pt = jnp.zeros(k.shape, F32)
    for e in range(N_EXPERTS):
        hit = stagepos[e:e + 1, :] == k
        pt = jnp.where(hit, 1.0 if values is None else values[e:e + 1, :], pt)
    return pt.astype(BF16)


def _rows_copy(src, dst, sem, src_row, dst_row, n):
    return pltpu.make_async_copy(
        src.at[pl.ds(pl.multiple_of(src_row * SUBLANES, SUBLANES), n * SUBLANES)],
        dst.at[pl.ds(pl.multiple_of(dst_row * SUBLANES, SUBLANES), n * SUBLANES)], sem)


def _dispatch_kernel(n_ref, s0_ref, pos_ref, shift_ref, h_ref, xe_hbm, stage_ref, sem, *, ntile, cap):
    i = pl.program_id(0)
    slot = i % 2

    def wait_writes(tile, slot):
        _, offs = _tile_counts(n_ref, tile, ntile)

        @pl.when(offs[-1] > 0)
        def _():
            _rows_copy(stage_ref.at[slot], xe_hbm, sem.at[slot], 0, 0, offs[-1]).wait()

    @pl.when(i >= 2)
    def _():
        wait_writes(i - 2, slot)

    counts, offs = _tile_counts(n_ref, i, ntile)
    pos = pos_ref[...]
    stagepos = jnp.where(pos >= 0, pos + shift_ref[0], -1)

    def chunk(c, carry):
        kbase = pl.multiple_of(c * STAGE_CHUNK, STAGE_CHUNK)
        rows = jnp.dot(_one_hot_t(stagepos, kbase), h_ref[...], preferred_element_type=F32)
        _store_rows(stage_ref, kbase, rows, lead=(slot,))
        return carry

    lax.fori_loop(0, (offs[-1] + STAGE_CHUNK - 1) // STAGE_CHUNK, chunk, 0)

    for e in range(N_EXPERTS):
        @pl.when(counts[e] > 0)
        def _(e=e):
            _rows_copy(stage_ref.at[slot], xe_hbm, sem.at[slot], offs[e],
                       e * cap + s0_ref[e * ntile + i], counts[e]).start()

    @pl.when(i == ntile - 1)
    def _():
        wait_writes(i, slot)
        if ntile > 1:
            wait_writes(i - 1, 1 - slot)


def _dispatch(n, s0, pos, shift, h1b, cap):
    e, t = pos.shape
    tile = ROUTE_TILE
    ntile = t // tile
    return pl.pallas_call(
        functools.partial(_dispatch_kernel, ntile=ntile, cap=cap),
        grid_spec=pltpu.PrefetchScalarGridSpec(
            num_scalar_prefetch=2,
            grid=(ntile,),
            in_specs=[
                pl.BlockSpec((e, tile), lambda i, *_: (0, i)),
                pl.BlockSpec((1, e, 1), lambda i, *_: (i, 0, 0)),
                pl.BlockSpec((tile, D_MODEL), lambda i, *_: (i, 0)),
            ],
            out_specs=pl.BlockSpec(memory_space=pl.ANY),
            scratch_shapes=[pltpu.VMEM((2, e * tile * SUBLANES, LANES), F32),
                            pltpu.SemaphoreType.DMA((2,))],
        ),
        out_shape=jax.ShapeDtypeStruct((e * cap * SUBLANES, LANES), F32),
        compiler_params=_params("arbitrary"),
        name="route_dispatch",
    )(n, s0, pos, shift, h1b)


def _combine_kernel(n_ref, s0_ref, pos_ref, aff_ref, shift_ref, h_ref, g_ref, b_ref, ye_hbm, o_ref,
                    stage_ref, sem, *, ntile, cap):
    i = pl.program_id(0)
    slot = i % 2

    def fetch(tile, slot):
        counts, offs = _tile_counts(n_ref, tile, ntile)
        for e in range(N_EXPERTS):
            @pl.when(counts[e] > 0)
            def _(e=e):
                _rows_copy(ye_hbm, stage_ref.at[slot], sem.at[slot],
                           e * cap + s0_ref[e * ntile + tile], offs[e], counts[e]).start()

    @pl.when(i == 0)
    def _():
        stage_ref[...] = jnp.zeros_like(stage_ref)
        fetch(0, 0)

    @pl.when(i + 1 < ntile)
    def _():
        fetch(i + 1, 1 - slot)

    _, offs = _tile_counts(n_ref, i, ntile)

    @pl.when(offs[-1] > 0)
    def _():
        _rows_copy(ye_hbm, stage_ref.at[slot], sem.at[slot], 0, 0, offs[-1]).wait()

    pos = pos_ref[...]
    stagepos = jnp.where(pos >= 0, pos + shift_ref[0], -1)
    gates = aff_ref[...]

    def chunk(c, acc):
        kbase = pl.multiple_of(c * STAGE_CHUNK, STAGE_CHUNK)
        ye = _load_rows(stage_ref, kbase, STAGE_CHUNK, lead=(slot,)).astype(BF16)
        return acc + lax.dot_general(_one_hot_t(stagepos, kbase, gates), ye,
                                     (((0,), (0,)), ((), ())), preferred_element_type=F32)

    ffn = lax.fori_loop(0, (offs[-1] + STAGE_CHUNK - 1) // STAGE_CHUNK, chunk,
                        jnp.zeros(h_ref.shape, F32))
    z = ALPHA * h_ref[...] + ffn
    mu = jnp.mean(z, axis=-1, keepdims=True)
    zc = z - mu
    var = jnp.mean(zc * zc, axis=-1, keepdims=True)
    o_ref[...] = zc * lax.rsqrt(var + LN_EPS) * g_ref[...] + b_ref[...]


def _combine(n, s0, pos, aff_t, shift, h1, g2, b2, ye, cap):
    e, t = pos.shape
    tile = ROUTE_TILE
    ntile = t // tile
    lanes = pl.BlockSpec((e, tile), lambda i, *_: (0, i))
    vec = pl.BlockSpec((1, D_MODEL), lambda i, *_: (0, 0))
    row = pl.BlockSpec((tile, D_MODEL), lambda i, *_: (i, 0))
    return pl.pallas_call(
        functools.partial(_combine_kernel, ntile=ntile, cap=cap),
        grid_spec=pltpu.PrefetchScalarGridSpec(
            num_scalar_prefetch=2,
            grid=(ntile,),
            in_specs=[lanes, lanes, pl.BlockSpec((1, e, 1), lambda i, *_: (i, 0, 0)), row, vec, vec,
                      pl.BlockSpec(memory_space=pl.ANY)],
            out_specs=row,
            scratch_shapes=[pltpu.VMEM((2, e * tile * SUBLANES, LANES), F32),
                            pltpu.SemaphoreType.DMA((2,))],
        ),
        out_shape=jax.ShapeDtypeStruct((t, D_MODEL), F32),
        compiler_params=_params("arbitrary"),
        name="route_combine_ln2",
    )(n, s0, pos, aff_t, shift, h1, g2, b2, ye)


def _layer(x, p):
    b, l, _ = x.shape
    t = b * l
    cap = CAPACITY_FACTOR * t // N_EXPERTS
    x2 = x.reshape(t, D_MODEL)
    q, k, v, xr, gate = _in_proj(x2, p["w_in"])
    attn = _attention(q.reshape(b, l, -1), k.reshape(b, l, -1), v.reshape(b, l, -1),
                      p["bias"], p["sink_row"], p["attn_g_col"])
    xr3 = xr.reshape(b, l, RNN_WIDTH)
    hs = [_lru(xr3, p["conv_w"], p["conv_b"], p["wa"][d], p["wx"][d], p["ba"][d], p["bx"][d],
               p["lam"][d], reverse=bool(d)) for d in range(2)]
    h1, h1b, aff_t = _mix_out(attn.reshape(t, -1), hs[0].reshape(t, -1), hs[1].reshape(t, -1), gate,
                              x2, p["w_out"], p["rnn_g"], p["ln1_g"], p["ln1_b"], p["wr_t"])
    pos = _route(aff_t, cap)
    n, s0, shift = _tile_tables(pos, ROUTE_TILE)
    xe = _dispatch(n, s0, pos, shift, h1b, cap)
    ye = _ffn(xe, p["wg"], p["wu"], p["wd"], cap)
    out = _combine(n, s0, pos, aff_t, shift, h1, p["ln2_g"], p["ln2_b"], ye, cap)
    return out.reshape(b, l, D_MODEL)


def _layer_params(li, w_in, attn_sink, attn_norm_g, rnn_norm_g, conv_w, conv_b, lru_w_a, lru_b_a,
                  lru_w_x, lru_b_x, lru_lambda, w_out, ln1_g, ln1_b, w_router, w_gate, w_up, w_down,
                  ln2_g, ln2_b):
    vec = lambda a: a[li].reshape(1, -1)
    per_dir = lambda f: [f(d) for d in range(2)]
    return dict(
        w_in=w_in[li].astype(BF16),
        bias=_alibi_bias(),
        sink_row=jnp.repeat(attn_sink[li].astype(F32), BLOCK).reshape(N_KV_HEADS, 1, Q_PER_KV * BLOCK),
        attn_g_col=jnp.broadcast_to(attn_norm_g[li].astype(F32)[:, None], (ATTN_WIDTH, BLOCK)),
        rnn_g=vec(rnn_norm_g),
        conv_w=conv_w[li], conv_b=vec(conv_b),
        wa=per_dir(lambda d: _block_diag(lru_w_a[li, d]).astype(BF16)),
        wx=per_dir(lambda d: _block_diag(lru_w_x[li, d]).astype(BF16)),
        ba=per_dir(lambda d: lru_b_a[li, d].reshape(1, -1)),
        bx=per_dir(lambda d: lru_b_x[li, d].reshape(1, -1)),
        lam=per_dir(lambda d: lru_lambda[li, d].reshape(1, -1)),
        w_out=w_out[li].astype(BF16),
        ln1_g=vec(ln1_g), ln1_b=vec(ln1_b),
        wr_t=w_router[li].T.astype(BF16),
        wg=w_gate[li].astype(BF16), wu=w_up[li].astype(BF16), wd=w_down[li].astype(BF16),
        ln2_g=vec(ln2_g), ln2_b=vec(ln2_b),
    )


def kernel(x_prompt, x_sample, w_in, attn_sink, attn_norm_g, rnn_norm_g, conv_w, conv_b, lru_w_a,
           lru_b_a, lru_w_x, lru_b_x, lru_lambda, w_out, ln1_g, ln1_b, w_router, w_gate, w_up,
           w_down, ln2_g, ln2_b):
    layers = [
        _layer_params(li, w_in, attn_sink, attn_norm_g, rnn_norm_g, conv_w, conv_b, lru_w_a, lru_b_a,
                      lru_w_x, lru_b_x, lru_lambda, w_out, ln1_g, ln1_b, w_router, w_gate, w_up,
                      w_down, ln2_g, ln2_b)
        for li in range(w_in.shape[0])]
    ys = []
    for x in (x_prompt, x_sample):
        for p in layers:
            x = _layer(x, p)
        ys.append(x)
    return tuple(ys)
```

```python
import functools
import math

import jax
import jax.numpy as jnp
from jax import lax
from jax.experimental import pallas as pl
from jax.experimental.pallas import tpu as pltpu

D_MODEL = 1024
HEAD_DIM = 64
N_Q_HEADS = 8
N_KV_HEADS = 2
Q_PER_KV = N_Q_HEADS // N_KV_HEADS
ATTN_WIDTH = N_Q_HEADS * HEAD_DIM
KV_WIDTH = N_KV_HEADS * HEAD_DIM
BLOCK = 128
RNN_WIDTH = 512
RNN_BLOCKS = 8
RNN_BLOCK_W = RNN_WIDTH // RNN_BLOCKS
CONV_WIDTH = 4
LRU_C = 8.0
N_EXPERTS = 16
EXPERT_FF = 2048
CAPACITY_FACTOR = 2
ALPHA = 2.0 ** 0.25
LN_EPS = 1e-5
RMS_EPS = 1e-6
MASKED = -1e30
TINY = 1e-37

SUBLANES = 8
LANES = 128
VMEM_LIMIT = 56 * 1024 * 1024

ROW_TILE = 512
ATTN_QBLOCKS = 2
LRU_CHUNK = 256
LRU_BATCH = SUBLANES
LRU_UNROLL = 8
FFN_ROWS = 512
FFN_FCHUNK = 512
POS_BLOCK = 2048
ROUTE_TILE = 256
STAGE_CHUNK = 256

BF16 = jnp.bfloat16
F32 = jnp.float32


def _params(*sem):
    return pltpu.CompilerParams(dimension_semantics=sem, vmem_limit_bytes=VMEM_LIMIT)


def _in_proj_kernel(x_ref, w_ref, q_ref, k_ref, v_ref, xr_ref, gate_ref):
    xb = x_ref[...].astype(BF16)

    def proj(lo, hi):
        return jnp.dot(xb, w_ref[:, lo:hi], preferred_element_type=F32)

    o = 0
    q_ref[...] = (proj(o, o + ATTN_WIDTH) * (HEAD_DIM ** -0.5)).astype(BF16)
    o += ATTN_WIDTH
    k_ref[...] = proj(o, o + KV_WIDTH).astype(BF16)
    o += KV_WIDTH
    v_ref[...] = proj(o, o + KV_WIDTH).astype(BF16)
    o += KV_WIDTH
    xr_ref[...] = proj(o, o + RNN_WIDTH)
    o += RNN_WIDTH
    gate_ref[...] = proj(o, o + RNN_WIDTH)


def _in_proj(x2, w_bf16):
    t = x2.shape[0]
    in_w = w_bf16.shape[1]
    row = lambda w: pl.BlockSpec((ROW_TILE, w), lambda i: (i, 0))
    return pl.pallas_call(
        _in_proj_kernel,
        grid=(t // ROW_TILE,),
        in_specs=[row(D_MODEL), pl.BlockSpec((D_MODEL, in_w), lambda i: (0, 0))],
        out_specs=[row(ATTN_WIDTH), row(KV_WIDTH), row(KV_WIDTH), row(RNN_WIDTH), row(RNN_WIDTH)],
        out_shape=[
            jax.ShapeDtypeStruct((t, ATTN_WIDTH), BF16),
            jax.ShapeDtypeStruct((t, KV_WIDTH), BF16),
            jax.ShapeDtypeStruct((t, KV_WIDTH), BF16),
            jax.ShapeDtypeStruct((t, RNN_WIDTH), F32),
            jax.ShapeDtypeStruct((t, RNN_WIDTH), F32),
        ],
        compiler_params=_params("parallel"),
        name="in_proj",
    )(x2, w_bf16)


def _attn_kernel(q_ref, kp_ref, kc_ref, kn_ref, vp_ref, vc_ref, vn_ref, bias_first_ref, bias_mid_ref,
                 bias_last_ref, sink_ref, g_ref, o_ref):
    k = jnp.concatenate([kp_ref[0], kc_ref[0], kn_ref[0]], axis=0)
    v = jnp.concatenate([vp_ref[0], vc_ref[0], vn_ref[0]], axis=0)
    v_t = v.astype(F32).T
    pad_rows = 2 * SUBLANES
    ones_row = jnp.where(
        lax.broadcasted_iota(jnp.int32, (pad_rows, 3 * BLOCK), 0) == 0, 1.0, 0.0)
    for qb in range(ATTN_QBLOCKS):
        q = q_ref[0, qb * BLOCK:(qb + 1) * BLOCK, :]
        keys = slice(qb * BLOCK, (qb + 3) * BLOCK)
        bias_ref = (bias_first_ref if qb == 0 else
                    bias_last_ref if qb == ATTN_QBLOCKS - 1 else bias_mid_ref)
        heads = []
        for g in range(N_KV_HEADS):
            qs = jnp.concatenate(
                [q[:, (Q_PER_KV * g + j) * HEAD_DIM:(Q_PER_KV * g + j + 1) * HEAD_DIM]
                 for j in range(Q_PER_KV)], axis=0)
            kg = k[keys, g * HEAD_DIM:(g + 1) * HEAD_DIM]
            s = lax.dot_general(kg, qs, (((1,), (1,)), ((), ())), preferred_element_type=F32)
            s = s + bias_ref[0, g]
            sink = sink_ref[g]
            m = jnp.maximum(jnp.max(s, axis=0, keepdims=True), sink)
            p = jnp.exp(s - m).astype(BF16)
            lhs = jnp.concatenate([v_t[g * HEAD_DIM:(g + 1) * HEAD_DIM, keys], ones_row],
                                  axis=0).astype(BF16)
            o_aug = jnp.dot(lhs, p, preferred_element_type=F32)
            denom = o_aug[HEAD_DIM:HEAD_DIM + 1] + jnp.exp(sink - m)
            o = o_aug[:HEAD_DIM] * (1.0 / denom)
            heads += [o[:, j * BLOCK:(j + 1) * BLOCK] for j in range(Q_PER_KV)]
        y_t = jnp.concatenate(heads, axis=0)
        ms = jnp.mean(y_t * y_t, axis=0, keepdims=True)
        o_ref[0, qb * BLOCK:(qb + 1) * BLOCK, :] = (
            y_t * lax.rsqrt(ms + RMS_EPS) * g_ref[...]).T.astype(BF16)


def _attention(q, k, v, bias, sink_row, attn_g_col):
    b, l, _ = q.shape
    span = ATTN_QBLOCKS * BLOCK
    assert ATTN_QBLOCKS >= 2 and l % span == 0
    ns = l // span
    nb = l // BLOCK
    edge = lambda f: pl.BlockSpec((1, BLOCK, KV_WIDTH), f)
    prev = lambda bi, i: (bi, jnp.maximum(i * ATTN_QBLOCKS - 1, 0), 0)
    cur = lambda bi, i: (bi, i, 0)
    nxt = lambda bi, i: (bi, jnp.minimum((i + 1) * ATTN_QBLOCKS, nb - 1), 0)
    mid = pl.BlockSpec((1, span, KV_WIDTH), cur)
    bias_spec = lambda f: pl.BlockSpec((1,) + bias.shape[1:], f)
    first = lambda bi, i: ((i == 0).astype(jnp.int32), 0, 0, 0)
    last = lambda bi, i: (2 * (i == ns - 1).astype(jnp.int32), 0, 0, 0)
    return pl.pallas_call(
        _attn_kernel,
        grid=(b, ns),
        in_specs=[
            pl.BlockSpec((1, span, ATTN_WIDTH), cur),
            edge(prev), mid, edge(nxt),
            edge(prev), mid, edge(nxt),
            bias_spec(first), bias_spec(lambda bi, i: (0, 0, 0, 0)), bias_spec(last),
            pl.BlockSpec(sink_row.shape, lambda bi, i: (0, 0, 0)),
            pl.BlockSpec(attn_g_col.shape, lambda bi, i: (0, 0)),
        ],
        out_specs=pl.BlockSpec((1, span, ATTN_WIDTH), cur),
        out_shape=jax.ShapeDtypeStruct((b, l, ATTN_WIDTH), BF16),
        compiler_params=_params("parallel", "parallel"),
        name="banded_attention",
    )(q, k, k, k, v, v, v, bias, bias, bias, sink_row, attn_g_col)


def _alibi_bias():
    qi = jnp.arange(BLOCK)[None, :]
    sj = jnp.arange(3 * BLOCK)[:, None]
    rel = sj - BLOCK - qi
    dist = jnp.abs(rel).astype(F32)
    slopes = jnp.asarray([2.0 ** (-8.0 * (h + 1) / N_Q_HEADS) for h in range(N_Q_HEADS)], F32)
    bias = jnp.where((jnp.abs(rel) <= BLOCK)[None], -slopes[:, None, None] * dist[None], MASKED)
    bias = bias.reshape(N_KV_HEADS, Q_PER_KV, 3 * BLOCK, BLOCK).transpose(0, 2, 1, 3)
    bias = bias.reshape(N_KV_HEADS, 3 * BLOCK, Q_PER_KV * BLOCK)
    no_prev = (sj < BLOCK)[None]
    no_next = (sj >= 2 * BLOCK)[None]
    return jnp.stack([
        bias,
        jnp.where(no_prev, MASKED, bias),
        jnp.where(no_next, MASKED, bias),
    ])


def _lru_kernel(x_ref, xp_ref, xn_ref, cw_ref, cb_ref, wa_ref, wx_ref, ba_ref, bx_ref, lam_ref,
                h_ref, a_scr, u_scr, carry_ref, *, nc, lc, reverse):
    c = pl.program_id(1)
    cc = nc - 1 - c if reverse else c
    pitch = lc + SUBLANES
    slabs = RNN_WIDTH // LANES

    @pl.when(c == 0)
    def _():
        carry_ref[...] = jnp.zeros_like(carry_ref)

    row8 = lax.broadcasted_iota(jnp.int32, (SUBLANES, 1), 0)
    w = cw_ref[...]
    lam = lam_ref[...]
    softplus_neg_lam = jnp.maximum(-lam, 0.0) + jnp.log1p(jnp.exp(-jnp.abs(lam)))
    half_rate = (-0.5 * LRU_C) * softplus_neg_lam
    for b in range(LRU_BATCH):
        x = x_ref[b]
        prev8 = jnp.where(cc > 0, xp_ref[b], 0.0)
        next8 = jnp.where(cc < nc - 1, xn_ref[b], 0.0)

        def shifted_down(k):
            r = pltpu.roll(x, k, 0)
            top = jnp.where(row8 < k, pltpu.roll(prev8, k, 0), r[:SUBLANES])
            return jnp.concatenate([top, r[SUBLANES:]], axis=0)

        r1 = pltpu.roll(x, lc - 1, 0)
        bot = jnp.where(row8 == SUBLANES - 1, pltpu.roll(next8, SUBLANES - 1, 0),
                        r1[lc - SUBLANES:])
        x_p1 = jnp.concatenate([r1[:lc - SUBLANES], bot], axis=0)

        xc = shifted_down(2) * w[0:1] + shifted_down(1) * w[1:2]
        xc = xc + x * w[2:3]
        xc = xc + x_p1 * w[3:4]
        xc = xc + cb_ref[...]

        xcb = xc.astype(BF16)
        th_r = jnp.tanh(jnp.dot(xcb, wa_ref[...], preferred_element_type=F32) + ba_ref[...])
        th_i = jnp.tanh(jnp.dot(xcb, wx_ref[...], preferred_element_type=F32) + bx_ref[...])
        log_a = th_r * half_rate + half_rate
        a = jnp.exp(log_a)
        z = jnp.tanh(log_a) * (-1.0 - a * a)
        root = z * lax.rsqrt(jnp.maximum(z, TINY))
        u = root * ((0.5 * th_i + 0.5) * xc)
        for j in range(slabs):
            a_scr[j, b * pitch:b * pitch + lc, :] = a[:, j * LANES:(j + 1) * LANES]
            u_scr[j, b * pitch:b * pitch + lc, :] = u[:, j * LANES:(j + 1) * LANES]

    def step(i, hs):
        t = lc - 1 - i if reverse else i
        out = []
        for j in range(slabs):
            rows = pl.ds(t, LRU_BATCH, stride=pitch)
            h = a_scr[j, rows, :] * hs[j] + u_scr[j, rows, :]
            u_scr[j, rows, :] = h
            out.append(h)
        return tuple(out)

    hs = lax.fori_loop(0, lc, step, tuple(carry_ref[j] for j in range(slabs)), unroll=LRU_UNROLL)
    for j in range(slabs):
        carry_ref[j] = hs[j]
        for b in range(LRU_BATCH):
            h_ref[b, :, j * LANES:(j + 1) * LANES] = u_scr[j, b * pitch:b * pitch + lc, :]


def _lru(xr, conv_w, conv_b, wa, wx, ba, bx, lam, *, reverse):
    b, l, _ = xr.shape
    lc = LRU_CHUNK
    assert b % LRU_BATCH == 0 and l % lc == 0
    nc = l // lc
    per8 = lc // SUBLANES
    n8 = l // SUBLANES
    pos = (lambda c: nc - 1 - c) if reverse else (lambda c: c)
    const = lambda bi, c: (0, 0)
    vec = pl.BlockSpec((1, RNN_WIDTH), const)
    mat = pl.BlockSpec((RNN_WIDTH, RNN_WIDTH), const)
    slabs = RNN_WIDTH // LANES
    scratch = pltpu.VMEM((slabs, LRU_BATCH * (lc + SUBLANES), LANES), F32)
    return pl.pallas_call(
        functools.partial(_lru_kernel, nc=nc, lc=lc, reverse=reverse),
        grid=(b // LRU_BATCH, nc),
        in_specs=[
            pl.BlockSpec((LRU_BATCH, lc, RNN_WIDTH), lambda bi, c: (bi, pos(c), 0)),
            pl.BlockSpec((LRU_BATCH, SUBLANES, RNN_WIDTH),
                         lambda bi, c: (bi, jnp.maximum(pos(c) * per8 - 1, 0), 0)),
            pl.BlockSpec((LRU_BATCH, SUBLANES, RNN_WIDTH),
                         lambda bi, c: (bi, jnp.minimum((pos(c) + 1) * per8, n8 - 1), 0)),
            pl.BlockSpec((CONV_WIDTH, RNN_WIDTH), const), vec, mat, mat, vec, vec, vec,
        ],
        out_specs=pl.BlockSpec((LRU_BATCH, lc, RNN_WIDTH), lambda bi, c: (bi, pos(c), 0)),
        out_shape=jax.ShapeDtypeStruct((b, l, RNN_WIDTH), F32),
        scratch_shapes=[scratch, scratch, pltpu.VMEM((slabs, LRU_BATCH, LANES), F32)],
        compiler_params=_params("parallel", "arbitrary"),
        name="rg_lru_bwd" if reverse else "rg_lru_fwd",
    )(xr, xr, xr, conv_w, conv_b, wa, wx, ba, bx, lam)


def _block_diag(w):
    eye = jnp.eye(RNN_BLOCKS, dtype=w.dtype)
    return jnp.einsum('hij,hk->hikj', w, eye).reshape(RNN_WIDTH, RNN_WIDTH)


def _mix_out_kernel(attn_ref, hf_ref, hb_ref, gate_ref, x_ref, wo_ref, rg_ref, g1_ref, b1_ref,
                    wr_ref, h_ref, hb16_ref, aff_ref):
    gt = gate_ref[...]
    gelu = 0.5 * gt * (1.0 + jnp.tanh(math.sqrt(2.0 / math.pi) * (gt + 0.044715 * (gt * gt * gt))))
    yr = (hf_ref[...] + hb_ref[...]) * gelu
    ms = jnp.mean(yr * yr, axis=-1, keepdims=True)
    yn = (yr * lax.rsqrt(ms + RMS_EPS) * rg_ref[...]).astype(BF16)
    mix = jnp.dot(attn_ref[...], wo_ref[:ATTN_WIDTH, :], preferred_element_type=F32)
    mix = mix + jnp.dot(yn, wo_ref[ATTN_WIDTH:, :], preferred_element_type=F32)
    z = ALPHA * x_ref[...] + mix
    mu = jnp.mean(z, axis=-1, keepdims=True)
    zc = z - mu
    var = jnp.mean(zc * zc, axis=-1, keepdims=True)
    h = zc * lax.rsqrt(var + LN_EPS) * g1_ref[...] + b1_ref[...]
    h_ref[...] = h
    hb = h.astype(BF16)
    hb16_ref[...] = hb
    logits = lax.dot_general(wr_ref[...], hb, (((1,), (1,)), ((), ())),
                             preferred_element_type=F32)
    mx = jnp.max(logits, axis=0, keepdims=True)
    e = jnp.exp(logits - mx)
    aff_ref[...] = e / jnp.sum(e, axis=0, keepdims=True)


def _mix_out(attn, hf, hb, gate, x2, wo, rnn_g, g1, b1, wr_t):
    t = x2.shape[0]
    row = lambda w: pl.BlockSpec((ROW_TILE, w), lambda i: (i, 0))
    const = lambda i: (0, 0)
    return pl.pallas_call(
        _mix_out_kernel,
        grid=(t // ROW_TILE,),
        in_specs=[
            row(ATTN_WIDTH), row(RNN_WIDTH), row(RNN_WIDTH), row(RNN_WIDTH), row(D_MODEL),
            pl.BlockSpec((D_MODEL, D_MODEL), const),
            pl.BlockSpec((1, RNN_WIDTH), const),
            pl.BlockSpec((1, D_MODEL), const),
            pl.BlockSpec((1, D_MODEL), const),
            pl.BlockSpec((N_EXPERTS, D_MODEL), const),
        ],
        out_specs=[row(D_MODEL), row(D_MODEL), pl.BlockSpec((N_EXPERTS, ROW_TILE), lambda i: (0, i))],
        out_shape=[
            jax.ShapeDtypeStruct((t, D_MODEL), F32),
            jax.ShapeDtypeStruct((t, D_MODEL), BF16),
            jax.ShapeDtypeStruct((N_EXPERTS, t), F32),
        ],
        compiler_params=_params("parallel"),
        name="mix_out_ln1_router",
    )(attn, hf, hb, gate, x2, wo, rnn_g, g1, b1, wr_t)


def _load_rows(ref, start, n, lead=()):
    return jnp.concatenate(
        [ref[lead + (pl.ds(start * SUBLANES + j, n, stride=SUBLANES), slice(None))]
         for j in range(D_MODEL // LANES)], axis=1)


def _store_rows(ref, start, rows, lead=()):
    n = rows.shape[0]
    for j in range(D_MODEL // LANES):
        ref[lead + (pl.ds(start * SUBLANES + j, n, stride=SUBLANES), slice(None))] = (
            rows[:, j * LANES:(j + 1) * LANES])


def _ffn_kernel(x_ref, wg_ref, wu_ref, wd_ref, o_ref):
    x = _load_rows(x_ref, 0, FFN_ROWS).astype(BF16)
    acc = jnp.zeros((FFN_ROWS, D_MODEL), F32)
    for f in range(EXPERT_FF // FFN_FCHUNK):
        cols = slice(f * FFN_FCHUNK, (f + 1) * FFN_FCHUNK)
        gt = jnp.dot(x, wg_ref[0, :, cols], preferred_element_type=F32)
        up = jnp.dot(x, wu_ref[0, :, cols], preferred_element_type=F32)
        hid = (gt * jax.nn.sigmoid(gt) * up).astype(BF16)
        acc = acc + jnp.dot(hid, wd_ref[0, cols, :], preferred_element_type=F32)
    _store_rows(o_ref, 0, acc)


def _ffn(xe, wg, wu, wd, cap):
    e = wg.shape[0]
    per_e = cap // FFN_ROWS
    slots = pl.BlockSpec((FFN_ROWS * SUBLANES, LANES), lambda ei, j: (ei * per_e + j, 0))
    wspec = lambda a, b: pl.BlockSpec((1, a, b), lambda ei, j: (ei, 0, 0))
    return pl.pallas_call(
        _ffn_kernel,
        grid=(e, per_e),
        in_specs=[slots, wspec(D_MODEL, EXPERT_FF), wspec(D_MODEL, EXPERT_FF),
                  wspec(EXPERT_FF, D_MODEL)],
        out_specs=slots,
        out_shape=jax.ShapeDtypeStruct(xe.shape, F32),
        compiler_params=_params("parallel", "arbitrary"),
        name="expert_ffn",
    )(xe, wg, wu, wd)


def _threshold_kernel(aff_ref, thr_ref, need_ref, *, cap):
    def count(mask):
        return jnp.sum(jnp.where(mask, 1.0, 0.0), axis=1, keepdims=True)

    def body(b, thr):
        cand = thr | jnp.left_shift(jnp.int32(1), 30 - b)
        bits = pltpu.bitcast(aff_ref[...], jnp.int32)
        return jnp.where(count(bits >= cand) >= cap, cand, thr)

    thr = lax.fori_loop(0, 31, body, jnp.zeros((N_EXPERTS, 1), jnp.int32))
    bits = pltpu.bitcast(aff_ref[...], jnp.int32)
    thr_ref[...] = thr
    need_ref[...] = cap - count(bits > thr)


def _positions_kernel(aff_ref, thr_ref, need_ref, pos_ref, tie_run_ref, sel_run_ref, *, block):
    @pl.when(pl.program_id(0) == 0)
    def _():
        tie_run_ref[...] = jnp.zeros_like(tie_run_ref)
        sel_run_ref[...] = jnp.zeros_like(sel_run_ref)

    thr = thr_ref[...]
    need = need_ref[...]
    r = lax.broadcasted_iota(jnp.int32, (LANES, LANES), 0)
    c = lax.broadcasted_iota(jnp.int32, (LANES, LANES), 1)
    before = jnp.where(r < c, 1.0, 0.0).astype(BF16)
    tie_run = tie_run_ref[...]
    sel_run = sel_run_ref[...]
    for s in range(block // LANES):
        lanes = slice(s * LANES, (s + 1) * LANES)
        bits = pltpu.bitcast(aff_ref[:, lanes], jnp.int32)
        tie = jnp.where(bits == thr, 1.0, 0.0)
        tie_before = jnp.dot(tie.astype(BF16), before, preferred_element_type=F32) + tie_run
        sel = (bits > thr) | ((bits == thr) & (tie_before < need))
        picked = jnp.where(sel, 1.0, 0.0)
        sel_before = jnp.dot(picked.astype(BF16), before, preferred_element_type=F32) + sel_run
        pos_ref[:, lanes] = jnp.where(sel, sel_before.astype(jnp.int32), -1)
        tie_run = tie_run + jnp.sum(tie, axis=1, keepdims=True)
        sel_run = sel_run + jnp.sum(picked, axis=1, keepdims=True)
    tie_run_ref[...] = tie_run
    sel_run_ref[...] = sel_run


def _route(aff_t, cap):
    e, t = aff_t.shape
    col = jax.ShapeDtypeStruct((e, 1), jnp.int32)
    thr, need = pl.pallas_call(
        functools.partial(_threshold_kernel, cap=cap),
        out_shape=[col, jax.ShapeDtypeStruct((e, 1), F32)],
        compiler_params=_params(),
        name="route_threshold",
    )(aff_t)
    block = min(POS_BLOCK, t)
    cspec = pl.BlockSpec((e, 1), lambda i: (0, 0))
    return pl.pallas_call(
        functools.partial(_positions_kernel, block=block),
        grid=(t // block,),
        in_specs=[pl.BlockSpec((e, block), lambda i: (0, i)), cspec, cspec],
        out_specs=pl.BlockSpec((e, block), lambda i: (0, i)),
        out_shape=jax.ShapeDtypeStruct((e, t), jnp.int32),
        scratch_shapes=[pltpu.VMEM((e, 1), F32), pltpu.VMEM((e, 1), F32)],
        compiler_params=_params("arbitrary"),
        name="route_positions",
    )(aff_t, thr, need)


def _tile_tables(pos, tile):
    e, t = pos.shape
    n = jnp.sum((pos >= 0).reshape(e, t // tile, tile), axis=-1, dtype=jnp.int32)
    s0 = jnp.cumsum(n, axis=1) - n
    off = jnp.cumsum(n, axis=0) - n
    shift = (off - s0).T.reshape(t // tile, e, 1)
    return n.reshape(-1), s0.reshape(-1), shift


def _tile_counts(n_ref, tile, ntile):
    counts = [n_ref[e * ntile + tile] for e in range(N_EXPERTS)]
    offs = [jnp.int32(0)]
    for c in counts:
        offs.append(offs[-1] + c)
    return counts, offs


def _one_hot_t(stagepos, kbase, values=None):
    rel = stagepos - kbase
    rel = jnp.where((rel >= 0) & (rel < STAGE_CHUNK), rel, -1).astype(F32).astype(BF16)
    k = lax.broadcasted_iota(jnp.int32, (STAGE_CHUNK, stagepos.shape[1]), 0).astype(F32).astype(BF16)
    one = jnp.ones((1, stagepos.shape[1]), BF16)
    pt = jnp.zeros(k.shape, BF16)
    for e in range(N_EXPERTS):
        hit = rel[e:e + 1, :] == k
        pt = jnp.where(hit, one if values is None else values[e:e + 1, :], pt)
    return pt


def _rows_copy(src, dst, sem, src_row, dst_row, n):
    return pltpu.make_async_copy(
        src.at[pl.ds(pl.multiple_of(src_row * SUBLANES, SUBLANES), n * SUBLANES)],
        dst.at[pl.ds(pl.multiple_of(dst_row * SUBLANES, SUBLANES), n * SUBLANES)], sem)


def _dispatch_kernel(n_ref, s0_ref, pos_ref, shift_ref, h_ref, xe_hbm, stage_ref, sem, *, ntile, cap):
    i = pl.program_id(0)
    slot = i % 2

    def wait_writes(tile, slot):
        _, offs = _tile_counts(n_ref, tile, ntile)

        @pl.when(offs[-1] > 0)
        def _():
            _rows_copy(stage_ref.at[slot], xe_hbm, sem.at[slot], 0, 0, offs[-1]).wait()

    @pl.when(i >= 2)
    def _():
        wait_writes(i - 2, slot)

    counts, offs = _tile_counts(n_ref, i, ntile)
    pos = pos_ref[...]
    stagepos = jnp.where(pos >= 0, pos + shift_ref[0], -1)

    def chunk(c, carry):
        kbase = pl.multiple_of(c * STAGE_CHUNK, STAGE_CHUNK)
        rows = jnp.dot(_one_hot_t(stagepos, kbase), h_ref[...], preferred_element_type=F32)
        _store_rows(stage_ref, kbase, rows, lead=(slot,))
        return carry

    lax.fori_loop(0, (offs[-1] + STAGE_CHUNK - 1) // STAGE_CHUNK, chunk, 0)

    for e in range(N_EXPERTS):
        @pl.when(counts[e] > 0)
        def _(e=e):
            _rows_copy(stage_ref.at[slot], xe_hbm, sem.at[slot], offs[e],
                       e * cap + s0_ref[e * ntile + i], counts[e]).start()

    @pl.when(i == ntile - 1)
    def _():
        wait_writes(i, slot)
        if ntile > 1:
            wait_writes(i - 1, 1 - slot)


def _dispatch(n, s0, pos, shift, h1b, cap):
    e, t = pos.shape
    tile = ROUTE_TILE
    ntile = t // tile
    return pl.pallas_call(
        functools.partial(_dispatch_kernel, ntile=ntile, cap=cap),
        grid_spec=pltpu.PrefetchScalarGridSpec(
            num_scalar_prefetch=2,
            grid=(ntile,),
            in_specs=[
                pl.BlockSpec((e, tile), lambda i, *_: (0, i)),
                pl.BlockSpec((1, e, 1), lambda i, *_: (i, 0, 0)),
                pl.BlockSpec((tile, D_MODEL), lambda i, *_: (i, 0)),
            ],
            out_specs=pl.BlockSpec(memory_space=pl.ANY),
            scratch_shapes=[pltpu.VMEM((2, e * tile * SUBLANES, LANES), F32),
                            pltpu.SemaphoreType.DMA((2,))],
        ),
        out_shape=jax.ShapeDtypeStruct((e * cap * SUBLANES, LANES), F32),
        compiler_params=_params("arbitrary"),
        name="route_dispatch",
    )(n, s0, pos, shift, h1b)


def _combine_kernel(n_ref, s0_ref, pos_ref, aff_ref, shift_ref, h_ref, g_ref, b_ref, ye_hbm, o_ref,
                    stage_ref, sem, *, ntile, cap):
    i = pl.program_id(0)
    slot = i % 2

    def fetch(tile, slot):
        counts, offs = _tile_counts(n_ref, tile, ntile)
        for e in range(N_EXPERTS):
            @pl.when(counts[e] > 0)
            def _(e=e):
                _rows_copy(ye_hbm, stage_ref.at[slot], sem.at[slot],
                           e * cap + s0_ref[e * ntile + tile], offs[e], counts[e]).start()

    @pl.when(i == 0)
    def _():
        stage_ref[...] = jnp.zeros_like(stage_ref)
        fetch(0, 0)

    @pl.when(i + 1 < ntile)
    def _():
        fetch(i + 1, 1 - slot)

    _, offs = _tile_counts(n_ref, i, ntile)

    @pl.when(offs[-1] > 0)
    def _():
        _rows_copy(ye_hbm, stage_ref.at[slot], sem.at[slot], 0, 0, offs[-1]).wait()

    pos = pos_ref[...]
    stagepos = jnp.where(pos >= 0, pos + shift_ref[0], -1)
    gates = aff_ref[...].astype(BF16)

    def chunk(c, acc):
        kbase = pl.multiple_of(c * STAGE_CHUNK, STAGE_CHUNK)
        ye = _load_rows(stage_ref, kbase, STAGE_CHUNK, lead=(slot,)).astype(BF16)
        return acc + lax.dot_general(_one_hot_t(stagepos, kbase, gates), ye,
                                     (((0,), (0,)), ((), ())), preferred_element_type=F32)

    ffn = lax.fori_loop(0, (offs[-1] + STAGE_CHUNK - 1) // STAGE_CHUNK, chunk,
                        jnp.zeros(h_ref.shape, F32))
    z = ALPHA * h_ref[...] + ffn
    mu = jnp.mean(z, axis=-1, keepdims=True)
    zc = z - mu
    var = jnp.mean(zc * zc, axis=-1, keepdims=True)
    o_ref[...] = zc * lax.rsqrt(var + LN_EPS) * g_ref[...] + b_ref[...]


def _combine(n, s0, pos, aff_t, shift, h1, g2, b2, ye, cap):
    e, t = pos.shape
    tile = ROUTE_TILE
    ntile = t // tile
    lanes = pl.BlockSpec((e, tile), lambda i, *_: (0, i))
    vec = pl.BlockSpec((1, D_MODEL), lambda i, *_: (0, 0))
    row = pl.BlockSpec((tile, D_MODEL), lambda i, *_: (i, 0))
    return pl.pallas_call(
        functools.partial(_combine_kernel, ntile=ntile, cap=cap),
        grid_spec=pltpu.PrefetchScalarGridSpec(
            num_scalar_prefetch=2,
            grid=(ntile,),
            in_specs=[lanes, lanes, pl.BlockSpec((1, e, 1), lambda i, *_: (i, 0, 0)), row, vec, vec,
                      pl.BlockSpec(memory_space=pl.ANY)],
            out_specs=row,
            scratch_shapes=[pltpu.VMEM((2, e * tile * SUBLANES, LANES), F32),
                            pltpu.SemaphoreType.DMA((2,))],
        ),
        out_shape=jax.ShapeDtypeStruct((t, D_MODEL), F32),
        compiler_params=_params("arbitrary"),
        name="route_combine_ln2",
    )(n, s0, pos, aff_t, shift, h1, g2, b2, ye)


def _layer(x, p):
    b, l, _ = x.shape
    t = b * l
    cap = CAPACITY_FACTOR * t // N_EXPERTS
    x2 = x.reshape(t, D_MODEL)
    q, k, v, xr, gate = _in_proj(x2, p["w_in"])
    attn = _attention(q.reshape(b, l, -1), k.reshape(b, l, -1), v.reshape(b, l, -1),
                      p["bias"], p["sink_row"], p["attn_g_col"])
    xr3 = xr.reshape(b, l, RNN_WIDTH)
    hs = [_lru(xr3, p["conv_w"], p["conv_b"], p["wa"][d], p["wx"][d], p["ba"][d], p["bx"][d],
               p["lam"][d], reverse=bool(d)) for d in range(2)]
    h1, h1b, aff_t = _mix_out(attn.reshape(t, -1), hs[0].reshape(t, -1), hs[1].reshape(t, -1), gate,
                              x2, p["w_out"], p["rnn_g"], p["ln1_g"], p["ln1_b"], p["wr_t"])
    pos = _route(aff_t, cap)
    n, s0, shift = _tile_tables(pos, ROUTE_TILE)
    xe = _dispatch(n, s0, pos, shift, h1b, cap)
    ye = _ffn(xe, p["wg"], p["wu"], p["wd"], cap)
    out = _combine(n, s0, pos, aff_t, shift, h1, p["ln2_g"], p["ln2_b"], ye, cap)
    return out.reshape(b, l, D_MODEL)


def _layer_params(li, w_in, attn_sink, attn_norm_g, rnn_norm_g, conv_w, conv_b, lru_w_a, lru_b_a,
                  lru_w_x, lru_b_x, lru_lambda, w_out, ln1_g, ln1_b, w_router, w_gate, w_up, w_down,
                  ln2_g, ln2_b):
    vec = lambda a: a[li].reshape(1, -1)
    per_dir = lambda f: [f(d) for d in range(2)]
    return dict(
        w_in=w_in[li].astype(BF16),
        bias=_alibi_bias(),
        sink_row=jnp.repeat(attn_sink[li].astype(F32), BLOCK).reshape(N_KV_HEADS, 1, Q_PER_KV * BLOCK),
        attn_g_col=jnp.broadcast_to(attn_norm_g[li].astype(F32)[:, None], (ATTN_WIDTH, BLOCK)),
        rnn_g=vec(rnn_norm_g),
        conv_w=conv_w[li], conv_b=vec(conv_b),
        wa=per_dir(lambda d: (0.5 * _block_diag(lru_w_a[li, d])).astype(BF16)),
        wx=per_dir(lambda d: (0.5 * _block_diag(lru_w_x[li, d])).astype(BF16)),
        ba=per_dir(lambda d: 0.5 * lru_b_a[li, d].reshape(1, -1)),
        bx=per_dir(lambda d: 0.5 * lru_b_x[li, d].reshape(1, -1)),
        lam=per_dir(lambda d: lru_lambda[li, d].reshape(1, -1)),
        w_out=w_out[li].astype(BF16),
        ln1_g=vec(ln1_g), ln1_b=vec(ln1_b),
        wr_t=w_router[li].T.astype(BF16),
        wg=w_gate[li].astype(BF16), wu=w_up[li].astype(BF16), wd=w_down[li].astype(BF16),
        ln2_g=vec(ln2_g), ln2_b=vec(ln2_b),
    )


def kernel(x_prompt, x_sample, w_in, attn_sink, attn_norm_g, rnn_norm_g, conv_w, conv_b, lru_w_a,
           lru_b_a, lru_w_x, lru_b_x, lru_lambda, w_out, ln1_g, ln1_b, w_router, w_gate, w_up,
           w_down, ln2_g, ln2_b):
    layers = [
        _layer_params(li, w_in, attn_sink, attn_norm_g, rnn_norm_g, conv_w, conv_b, lru_w_a, lru_b_a,
                      lru_w_x, lru_b_x, lru_lambda, w_out, ln1_g, ln1_b, w_router, w_gate, w_up,
                      w_down, ln2_g, ln2_b)
        for li in range(w_in.shape[0])]
    ys = []
    for x in (x_prompt, x_sample):
        for p in layers:
            x = _layer(x, p)
        ys.append(x)
    return tuple(ys)
```

```python
import functools
import math

import jax
import jax.numpy as jnp
from jax import lax
from jax.experimental import pallas as pl
from jax.experimental.pallas import tpu as pltpu

D_MODEL = 1024
HEAD_DIM = 64
N_Q_HEADS = 8
N_KV_HEADS = 2
Q_PER_KV = N_Q_HEADS // N_KV_HEADS
ATTN_WIDTH = N_Q_HEADS * HEAD_DIM
KV_WIDTH = N_KV_HEADS * HEAD_DIM
BLOCK = 128
RNN_WIDTH = 512
RNN_BLOCKS = 8
RNN_BLOCK_W = RNN_WIDTH // RNN_BLOCKS
CONV_WIDTH = 4
LRU_C = 8.0
N_EXPERTS = 16
EXPERT_FF = 2048
CAPACITY_FACTOR = 2
ALPHA = 2.0 ** 0.25
LN_EPS = 1e-5
RMS_EPS = 1e-6
MASKED = -1e30
TINY = 1e-37

SUBLANES = 8
LANES = 128
VMEM_LIMIT = 56 * 1024 * 1024

ROW_TILE = 512
ATTN_QBLOCKS = 8
LRU_CHUNK = 256
LRU_BATCH = SUBLANES
LRU_UNROLL = 8
FFN_ROWS = 512
FFN_FCHUNK = 512
POS_BLOCK = 2048
ROUTE_TILE = 256
STAGE_CHUNK = 256

BF16 = jnp.bfloat16
F32 = jnp.float32


def _params(*sem):
    return pltpu.CompilerParams(dimension_semantics=sem, vmem_limit_bytes=VMEM_LIMIT)


def _in_proj_kernel(x_ref, w_ref, q_ref, k_ref, v_ref, xr_ref, gate_ref):
    xb = x_ref[...].astype(BF16)

    def proj(lo, hi):
        return jnp.dot(xb, w_ref[:, lo:hi], preferred_element_type=F32)

    o = 0
    q_ref[...] = (proj(o, o + ATTN_WIDTH) * (HEAD_DIM ** -0.5)).astype(BF16)
    o += ATTN_WIDTH
    k_ref[...] = proj(o, o + KV_WIDTH).astype(BF16)
    o += KV_WIDTH
    v_ref[...] = proj(o, o + KV_WIDTH).astype(BF16)
    o += KV_WIDTH
    xr_ref[...] = proj(o, o + RNN_WIDTH)
    o += RNN_WIDTH
    gate_ref[...] = proj(o, o + RNN_WIDTH)


def _in_proj(x2, w_bf16):
    t = x2.shape[0]
    in_w = w_bf16.shape[1]
    row = lambda w: pl.BlockSpec((ROW_TILE, w), lambda i: (i, 0))
    return pl.pallas_call(
        _in_proj_kernel,
        grid=(t // ROW_TILE,),
        in_specs=[row(D_MODEL), pl.BlockSpec((D_MODEL, in_w), lambda i: (0, 0))],
        out_specs=[row(ATTN_WIDTH), row(KV_WIDTH), row(KV_WIDTH), row(RNN_WIDTH), row(RNN_WIDTH)],
        out_shape=[
            jax.ShapeDtypeStruct((t, ATTN_WIDTH), BF16),
            jax.ShapeDtypeStruct((t, KV_WIDTH), BF16),
            jax.ShapeDtypeStruct((t, KV_WIDTH), BF16),
            jax.ShapeDtypeStruct((t, RNN_WIDTH), F32),
            jax.ShapeDtypeStruct((t, RNN_WIDTH), F32),
        ],
        compiler_params=_params("parallel"),
        name="in_proj",
    )(x2, w_bf16)


def _attn_kernel(q_ref, kp_ref, kc_ref, kn_ref, vp_ref, vc_ref, vn_ref, bias_first_ref, bias_mid_ref,
                 bias_last_ref, sink_ref, g_ref, o_ref):
    k = jnp.concatenate([kp_ref[0], kc_ref[0], kn_ref[0]], axis=0)
    v = jnp.concatenate([vp_ref[0], vc_ref[0], vn_ref[0]], axis=0)
    v_t = v.astype(F32).T
    pad_rows = 2 * SUBLANES
    ones_row = jnp.where(
        lax.broadcasted_iota(jnp.int32, (pad_rows, 3 * BLOCK), 0) == 0, 1.0, 0.0)
    for qb in range(ATTN_QBLOCKS):
        q = q_ref[0, qb * BLOCK:(qb + 1) * BLOCK, :]
        keys = slice(qb * BLOCK, (qb + 3) * BLOCK)
        bias_ref = (bias_first_ref if qb == 0 else
                    bias_last_ref if qb == ATTN_QBLOCKS - 1 else bias_mid_ref)
        heads = []
        for g in range(N_KV_HEADS):
            qs = jnp.concatenate(
                [q[:, (Q_PER_KV * g + j) * HEAD_DIM:(Q_PER_KV * g + j + 1) * HEAD_DIM]
                 for j in range(Q_PER_KV)], axis=0)
            kg = k[keys, g * HEAD_DIM:(g + 1) * HEAD_DIM]
            s = lax.dot_general(kg, qs, (((1,), (1,)), ((), ())), preferred_element_type=F32)
            s = s + bias_ref[0, g]
            sink = sink_ref[g]
            m = jnp.maximum(jnp.max(s, axis=0, keepdims=True), sink)
            p = jnp.exp(s - m).astype(BF16)
            lhs = jnp.concatenate([v_t[g * HEAD_DIM:(g + 1) * HEAD_DIM, keys], ones_row],
                                  axis=0).astype(BF16)
            o_aug = jnp.dot(lhs, p, preferred_element_type=F32)
            denom = o_aug[HEAD_DIM:HEAD_DIM + 1] + jnp.exp(sink - m)
            o = o_aug[:HEAD_DIM] * (1.0 / denom)
            heads += [o[:, j * BLOCK:(j + 1) * BLOCK] for j in range(Q_PER_KV)]
        y_t = jnp.concatenate(heads, axis=0)
        ms = jnp.mean(y_t * y_t, axis=0, keepdims=True)
        o_ref[0, qb * BLOCK:(qb + 1) * BLOCK, :] = (
            y_t * lax.rsqrt(ms + RMS_EPS) * g_ref[...]).T.astype(BF16)


def _attention(q, k, v, bias, sink_row, attn_g_col):
    b, l, _ = q.shape
    span = ATTN_QBLOCKS * BLOCK
    assert ATTN_QBLOCKS >= 2 and l % span == 0
    ns = l // span
    nb = l // BLOCK
    edge = lambda f: pl.BlockSpec((1, BLOCK, KV_WIDTH), f)
    prev = lambda bi, i: (bi, jnp.maximum(i * ATTN_QBLOCKS - 1, 0), 0)
    cur = lambda bi, i: (bi, i, 0)
    nxt = lambda bi, i: (bi, jnp.minimum((i + 1) * ATTN_QBLOCKS, nb - 1), 0)
    mid = pl.BlockSpec((1, span, KV_WIDTH), cur)
    bias_spec = lambda f: pl.BlockSpec((1,) + bias.shape[1:], f)
    first = lambda bi, i: ((i == 0).astype(jnp.int32), 0, 0, 0)
    last = lambda bi, i: (2 * (i == ns - 1).astype(jnp.int32), 0, 0, 0)
    return pl.pallas_call(
        _attn_kernel,
        grid=(b, ns),
        in_specs=[
            pl.BlockSpec((1, span, ATTN_WIDTH), cur),
            edge(prev), mid, edge(nxt),
            edge(prev), mid, edge(nxt),
            bias_spec(first), bias_spec(lambda bi, i: (0, 0, 0, 0)), bias_spec(last),
            pl.BlockSpec(sink_row.shape, lambda bi, i: (0, 0, 0)),
            pl.BlockSpec(attn_g_col.shape, lambda bi, i: (0, 0)),
        ],
        out_specs=pl.BlockSpec((1, span, ATTN_WIDTH), cur),
        out_shape=jax.ShapeDtypeStruct((b, l, ATTN_WIDTH), BF16),
        compiler_params=_params("parallel", "parallel"),
        name="banded_attention",
    )(q, k, k, k, v, v, v, bias, bias, bias, sink_row, attn_g_col)


def _alibi_bias():
    qi = jnp.arange(BLOCK)[None, :]
    sj = jnp.arange(3 * BLOCK)[:, None]
    rel = sj - BLOCK - qi
    dist = jnp.abs(rel).astype(F32)
    slopes = jnp.asarray([2.0 ** (-8.0 * (h + 1) / N_Q_HEADS) for h in range(N_Q_HEADS)], F32)
    bias = jnp.where((jnp.abs(rel) <= BLOCK)[None], -slopes[:, None, None] * dist[None], MASKED)
    bias = bias.reshape(N_KV_HEADS, Q_PER_KV, 3 * BLOCK, BLOCK).transpose(0, 2, 1, 3)
    bias = bias.reshape(N_KV_HEADS, 3 * BLOCK, Q_PER_KV * BLOCK)
    no_prev = (sj < BLOCK)[None]
    no_next = (sj >= 2 * BLOCK)[None]
    return jnp.stack([
        bias,
        jnp.where(no_prev, MASKED, bias),
        jnp.where(no_next, MASKED, bias),
    ])


def _lru_kernel(x_ref, xp_ref, xn_ref, cw_ref, cb_ref, wa_ref, wx_ref, ba_ref, bx_ref, lam_ref,
                h_ref, a_scr, u_scr, carry_ref, *, nc, lc, reverse):
    c = pl.program_id(1)
    cc = nc - 1 - c if reverse else c
    pitch = lc + SUBLANES
    slabs = RNN_WIDTH // LANES

    @pl.when(c == 0)
    def _():
        carry_ref[...] = jnp.zeros_like(carry_ref)

    row8 = lax.broadcasted_iota(jnp.int32, (SUBLANES, 1), 0)
    w = cw_ref[...]
    lam = lam_ref[...]
    softplus_neg_lam = jnp.maximum(-lam, 0.0) + jnp.log1p(jnp.exp(-jnp.abs(lam)))
    half_rate = (-0.5 * LRU_C) * softplus_neg_lam
    for b in range(LRU_BATCH):
        x = x_ref[b]
        prev8 = jnp.where(cc > 0, xp_ref[b], 0.0)
        next8 = jnp.where(cc < nc - 1, xn_ref[b], 0.0)

        def shifted_down(k):
            r = pltpu.roll(x, k, 0)
            top = jnp.where(row8 < k, pltpu.roll(prev8, k, 0), r[:SUBLANES])
            return jnp.concatenate([top, r[SUBLANES:]], axis=0)

        r1 = pltpu.roll(x, lc - 1, 0)
        bot = jnp.where(row8 == SUBLANES - 1, pltpu.roll(next8, SUBLANES - 1, 0),
                        r1[lc - SUBLANES:])
        x_p1 = jnp.concatenate([r1[:lc - SUBLANES], bot], axis=0)

        xc = shifted_down(2) * w[0:1] + shifted_down(1) * w[1:2]
        xc = xc + x * w[2:3]
        xc = xc + x_p1 * w[3:4]
        xc = xc + cb_ref[...]

        xcb = xc.astype(BF16)
        th_r = jnp.tanh(jnp.dot(xcb, wa_ref[...], preferred_element_type=F32) + ba_ref[...])
        th_i = jnp.tanh(jnp.dot(xcb, wx_ref[...], preferred_element_type=F32) + bx_ref[...])
        log_a = th_r * half_rate + half_rate
        a = jnp.exp(log_a)
        z = jnp.tanh(log_a) * (-1.0 - a * a)
        root = z * lax.rsqrt(jnp.maximum(z, TINY))
        u = root * ((0.5 * th_i + 0.5) * xc)
        for j in range(slabs):
            a_scr[j, b * pitch:b * pitch + lc, :] = a[:, j * LANES:(j + 1) * LANES]
            u_scr[j, b * pitch:b * pitch + lc, :] = u[:, j * LANES:(j + 1) * LANES]

    def step(i, hs):
        t = lc - 1 - i if reverse else i
        out = []
        for j in range(slabs):
            rows = pl.ds(t, LRU_BATCH, stride=pitch)
            h = a_scr[j, rows, :] * hs[j] + u_scr[j, rows, :]
            u_scr[j, rows, :] = h
            out.append(h)
        return tuple(out)

    hs = lax.fori_loop(0, lc, step, tuple(carry_ref[j] for j in range(slabs)), unroll=LRU_UNROLL)
    for j in range(slabs):
        carry_ref[j] = hs[j]
        for b in range(LRU_BATCH):
            h_ref[b, :, j * LANES:(j + 1) * LANES] = u_scr[j, b * pitch:b * pitch + lc, :]


def _lru(xr, conv_w, conv_b, wa, wx, ba, bx, lam, *, reverse):
    b, l, _ = xr.shape
    lc = LRU_CHUNK
    assert b % LRU_BATCH == 0 and l % lc == 0
    nc = l // lc
    per8 = lc // SUBLANES
    n8 = l // SUBLANES
    pos = (lambda c: nc - 1 - c) if reverse else (lambda c: c)
    const = lambda bi, c: (0, 0)
    vec = pl.BlockSpec((1, RNN_WIDTH), const)
    mat = pl.BlockSpec((RNN_WIDTH, RNN_WIDTH), const)
    slabs = RNN_WIDTH // LANES
    scratch = pltpu.VMEM((slabs, LRU_BATCH * (lc + SUBLANES), LANES), F32)
    return pl.pallas_call(
        functools.partial(_lru_kernel, nc=nc, lc=lc, reverse=reverse),
        grid=(b // LRU_BATCH, nc),
        in_specs=[
            pl.BlockSpec((LRU_BATCH, lc, RNN_WIDTH), lambda bi, c: (bi, pos(c), 0)),
            pl.BlockSpec((LRU_BATCH, SUBLANES, RNN_WIDTH),
                         lambda bi, c: (bi, jnp.maximum(pos(c) * per8 - 1, 0), 0)),
            pl.BlockSpec((LRU_BATCH, SUBLANES, RNN_WIDTH),
                         lambda bi, c: (bi, jnp.minimum((pos(c) + 1) * per8, n8 - 1), 0)),
            pl.BlockSpec((CONV_WIDTH, RNN_WIDTH), const), vec, mat, mat, vec, vec, vec,
        ],
        out_specs=pl.BlockSpec((LRU_BATCH, lc, RNN_WIDTH), lambda bi, c: (bi, pos(c), 0)),
        out_shape=jax.ShapeDtypeStruct((b, l, RNN_WIDTH), F32),
        scratch_shapes=[scratch, scratch, pltpu.VMEM((slabs, LRU_BATCH, LANES), F32)],
        compiler_params=_params("parallel", "arbitrary"),
        name="rg_lru_bwd" if reverse else "rg_lru_fwd",
    )(xr, xr, xr, conv_w, conv_b, wa, wx, ba, bx, lam)


def _block_diag(w):
    eye = jnp.eye(RNN_BLOCKS, dtype=w.dtype)
    return jnp.einsum('hij,hk->hikj', w, eye).reshape(RNN_WIDTH, RNN_WIDTH)


def _mix_out_kernel(attn_ref, hf_ref, hb_ref, gate_ref, x_ref, wo_ref, rg_ref, g1_ref, b1_ref,
                    wr_ref, h_ref, hb16_ref, aff_ref):
    gt = gate_ref[...]
    gelu = 0.5 * gt * (1.0 + jnp.tanh(math.sqrt(2.0 / math.pi) * (gt + 0.044715 * (gt * gt * gt))))
    yr = (hf_ref[...] + hb_ref[...]) * gelu
    ms = jnp.mean(yr * yr, axis=-1, keepdims=True)
    yn = (yr * lax.rsqrt(ms + RMS_EPS) * rg_ref[...]).astype(BF16)
    mix = jnp.dot(attn_ref[...], wo_ref[:ATTN_WIDTH, :], preferred_element_type=F32)
    mix = mix + jnp.dot(yn, wo_ref[ATTN_WIDTH:, :], preferred_element_type=F32)
    z = ALPHA * x_ref[...] + mix
    mu = jnp.mean(z, axis=-1, keepdims=True)
    zc = z - mu
    var = jnp.mean(zc * zc, axis=-1, keepdims=True)
    h = zc * lax.rsqrt(var + LN_EPS) * g1_ref[...] + b1_ref[...]
    h_ref[...] = h
    hb = h.astype(BF16)
    hb16_ref[...] = hb
    logits = lax.dot_general(wr_ref[...], hb, (((1,), (1,)), ((), ())),
                             preferred_element_type=F32)
    mx = jnp.max(logits, axis=0, keepdims=True)
    e = jnp.exp(logits - mx)
    aff_ref[...] = e / jnp.sum(e, axis=0, keepdims=True)


def _mix_out(attn, hf, hb, gate, x2, wo, rnn_g, g1, b1, wr_t):
    t = x2.shape[0]
    row = lambda w: pl.BlockSpec((ROW_TILE, w), lambda i: (i, 0))
    const = lambda i: (0, 0)
    return pl.pallas_call(
        _mix_out_kernel,
        grid=(t // ROW_TILE,),
        in_specs=[
            row(ATTN_WIDTH), row(RNN_WIDTH), row(RNN_WIDTH), row(RNN_WIDTH), row(D_MODEL),
            pl.BlockSpec((D_MODEL, D_MODEL), const),
            pl.BlockSpec((1, RNN_WIDTH), const),
            pl.BlockSpec((1, D_MODEL), const),
            pl.BlockSpec((1, D_MODEL), const),
            pl.BlockSpec((N_EXPERTS, D_MODEL), const),
        ],
        out_specs=[row(D_MODEL), row(D_MODEL), pl.BlockSpec((N_EXPERTS, ROW_TILE), lambda i: (0, i))],
        out_shape=[
            jax.ShapeDtypeStruct((t, D_MODEL), F32),
            jax.ShapeDtypeStruct((t, D_MODEL), BF16),
            jax.ShapeDtypeStruct((N_EXPERTS, t), F32),
        ],
        compiler_params=_params("parallel"),
        name="mix_out_ln1_router",
    )(attn, hf, hb, gate, x2, wo, rnn_g, g1, b1, wr_t)


def _load_rows(ref, start, n, lead=()):
    return jnp.concatenate(
        [ref[lead + (pl.ds(start * SUBLANES + j, n, stride=SUBLANES), slice(None))]
         for j in range(D_MODEL // LANES)], axis=1)


def _store_rows(ref, start, rows, lead=()):
    n = rows.shape[0]
    for j in range(D_MODEL // LANES):
        ref[lead + (pl.ds(start * SUBLANES + j, n, stride=SUBLANES), slice(None))] = (
            rows[:, j * LANES:(j + 1) * LANES])


def _ffn_kernel(x_ref, wg_ref, wu_ref, wd_ref, o_ref):
    x = _load_rows(x_ref, 0, FFN_ROWS).astype(BF16)
    acc = jnp.zeros((FFN_ROWS, D_MODEL), F32)
    for f in range(EXPERT_FF // FFN_FCHUNK):
        cols = slice(f * FFN_FCHUNK, (f + 1) * FFN_FCHUNK)
        gt = jnp.dot(x, wg_ref[0, :, cols], preferred_element_type=F32)
        up = jnp.dot(x, wu_ref[0, :, cols], preferred_element_type=F32)
        hid = (gt * jax.nn.sigmoid(gt) * up).astype(BF16)
        acc = acc + jnp.dot(hid, wd_ref[0, cols, :], preferred_element_type=F32)
    _store_rows(o_ref, 0, acc)


def _ffn(xe, wg, wu, wd, cap):
    e = wg.shape[0]
    per_e = cap // FFN_ROWS
    slots = pl.BlockSpec((FFN_ROWS * SUBLANES, LANES), lambda ei, j: (ei * per_e + j, 0))
    wspec = lambda a, b: pl.BlockSpec((1, a, b), lambda ei, j: (ei, 0, 0))
    return pl.pallas_call(
        _ffn_kernel,
        grid=(e, per_e),
        in_specs=[slots, wspec(D_MODEL, EXPERT_FF), wspec(D_MODEL, EXPERT_FF),
                  wspec(EXPERT_FF, D_MODEL)],
        out_specs=slots,
        out_shape=jax.ShapeDtypeStruct(xe.shape, F32),
        compiler_params=_params("parallel", "arbitrary"),
        name="expert_ffn",
    )(xe, wg, wu, wd)


def _threshold_kernel(aff_ref, thr_ref, need_ref, *, cap):
    def count(mask):
        return jnp.sum(jnp.where(mask, 1.0, 0.0), axis=1, keepdims=True)

    def body(b, thr):
        cand = thr | jnp.left_shift(jnp.int32(1), 30 - b)
        bits = pltpu.bitcast(aff_ref[...], jnp.int32)
        return jnp.where(count(bits >= cand) >= cap, cand, thr)

    thr = lax.fori_loop(0, 31, body, jnp.zeros((N_EXPERTS, 1), jnp.int32))
    bits = pltpu.bitcast(aff_ref[...], jnp.int32)
    thr_ref[...] = thr
    need_ref[...] = cap - count(bits > thr)


def _positions_kernel(aff_ref, thr_ref, need_ref, pos_ref, tie_run_ref, sel_run_ref, *, block):
    @pl.when(pl.program_id(0) == 0)
    def _():
        tie_run_ref[...] = jnp.zeros_like(tie_run_ref)
        sel_run_ref[...] = jnp.zeros_like(sel_run_ref)

    thr = thr_ref[...]
    need = need_ref[...]
    r = lax.broadcasted_iota(jnp.int32, (LANES, LANES), 0)
    c = lax.broadcasted_iota(jnp.int32, (LANES, LANES), 1)
    before = jnp.where(r < c, 1.0, 0.0).astype(BF16)
    tie_run = tie_run_ref[...]
    sel_run = sel_run_ref[...]
    for s in range(block // LANES):
        lanes = slice(s * LANES, (s + 1) * LANES)
        bits = pltpu.bitcast(aff_ref[:, lanes], jnp.int32)
        tie = jnp.where(bits == thr, 1.0, 0.0)
        tie_before = jnp.dot(tie.astype(BF16), before, preferred_element_type=F32) + tie_run
        sel = (bits > thr) | ((bits == thr) & (tie_before < need))
        picked = jnp.where(sel, 1.0, 0.0)
        sel_before = jnp.dot(picked.astype(BF16), before, preferred_element_type=F32) + sel_run
        pos_ref[:, lanes] = jnp.where(sel, sel_before.astype(jnp.int32), -1)
        tie_run = tie_run + jnp.sum(tie, axis=1, keepdims=True)
        sel_run = sel_run + jnp.sum(picked, axis=1, keepdims=True)
    tie_run_ref[...] = tie_run
    sel_run_ref[...] = sel_run


def _route(aff_t, cap):
    e, t = aff_t.shape
    col = jax.ShapeDtypeStruct((e, 1), jnp.int32)
    thr, need = pl.pallas_call(
        functools.partial(_threshold_kernel, cap=cap),
        out_shape=[col, jax.ShapeDtypeStruct((e, 1), F32)],
        compiler_params=_params(),
        name="route_threshold",
    )(aff_t)
    block = min(POS_BLOCK, t)
    cspec = pl.BlockSpec((e, 1), lambda i: (0, 0))
    return pl.pallas_call(
        functools.partial(_positions_kernel, block=block),
        grid=(t // block,),
        in_specs=[pl.BlockSpec((e, block), lambda i: (0, i)), cspec, cspec],
        out_specs=pl.BlockSpec((e, block), lambda i: (0, i)),
        out_shape=jax.ShapeDtypeStruct((e, t), jnp.int32),
        scratch_shapes=[pltpu.VMEM((e, 1), F32), pltpu.VMEM((e, 1), F32)],
        compiler_params=_params("arbitrary"),
        name="route_positions",
    )(aff_t, thr, need)


def _tile_tables(pos, tile):
    e, t = pos.shape
    n = jnp.sum((pos >= 0).reshape(e, t // tile, tile), axis=-1, dtype=jnp.int32)
    s0 = jnp.cumsum(n, axis=1) - n
    off = jnp.cumsum(n, axis=0) - n
    shift = (off - s0).T.reshape(t // tile, e, 1)
    return n.reshape(-1), s0.reshape(-1), shift


def _tile_counts(n_ref, tile, ntile):
    counts = [n_ref[e * ntile + tile] for e in range(N_EXPERTS)]
    offs = [jnp.int32(0)]
    for c in counts:
        offs.append(offs[-1] + c)
    return counts, offs


def _one_hot_t(stagepos, kbase, values=None):
    rel = stagepos - kbase
    rel = jnp.where((rel >= 0) & (rel < STAGE_CHUNK), rel, -1).astype(F32).astype(BF16)
    k = lax.broadcasted_iota(jnp.int32, (STAGE_CHUNK, stagepos.shape[1]), 0).astype(F32).astype(BF16)
    one = jnp.ones((1, stagepos.shape[1]), BF16)
    pt = jnp.zeros(k.shape, BF16)
    for e in range(N_EXPERTS):
        hit = rel[e:e + 1, :] == k
        pt = jnp.where(hit, one if values is None else values[e:e + 1, :], pt)
    return pt


def _rows_copy(src, dst, sem, src_row, dst_row, n):
    return pltpu.make_async_copy(
        src.at[pl.ds(pl.multiple_of(src_row * SUBLANES, SUBLANES), n * SUBLANES)],
        dst.at[pl.ds(pl.multiple_of(dst_row * SUBLANES, SUBLANES), n * SUBLANES)], sem)


def _dispatch_kernel(n_ref, s0_ref, pos_ref, shift_ref, h_ref, xe_hbm, stage_ref, sem, *, ntile, cap):
    i = pl.program_id(0)
    slot = i % 2

    def wait_writes(tile, slot):
        _, offs = _tile_counts(n_ref, tile, ntile)

        @pl.when(offs[-1] > 0)
        def _():
            _rows_copy(stage_ref.at[slot], xe_hbm, sem.at[slot], 0, 0, offs[-1]).wait()

    @pl.when(i >= 2)
    def _():
        wait_writes(i - 2, slot)

    counts, offs = _tile_counts(n_ref, i, ntile)
    pos = pos_ref[...]
    stagepos = jnp.where(pos >= 0, pos + shift_ref[0], -1)

    def chunk(c, carry):
        kbase = pl.multiple_of(c * STAGE_CHUNK, STAGE_CHUNK)
        rows = jnp.dot(_one_hot_t(stagepos, kbase), h_ref[...], preferred_element_type=F32)
        _store_rows(stage_ref, kbase, rows, lead=(slot,))
        return carry

    lax.fori_loop(0, (offs[-1] + STAGE_CHUNK - 1) // STAGE_CHUNK, chunk, 0)

    for e in range(N_EXPERTS):
        @pl.when(counts[e] > 0)
        def _(e=e):
            _rows_copy(stage_ref.at[slot], xe_hbm, sem.at[slot], offs[e],
                       e * cap + s0_ref[e * ntile + i], counts[e]).start()

    @pl.when(i == ntile - 1)
    def _():
        wait_writes(i, slot)
        if ntile > 1:
            wait_writes(i - 1, 1 - slot)


def _dispatch(n, s0, pos, shift, h1b, cap):
    e, t = pos.shape
    tile = ROUTE_TILE
    ntile = t // tile
    return pl.pallas_call(
        functools.partial(_dispatch_kernel, ntile=ntile, cap=cap),
        grid_spec=pltpu.PrefetchScalarGridSpec(
            num_scalar_prefetch=2,
            grid=(ntile,),
            in_specs=[
                pl.BlockSpec((e, tile), lambda i, *_: (0, i)),
                pl.BlockSpec((1, e, 1), lambda i, *_: (i, 0, 0)),
                pl.BlockSpec((tile, D_MODEL), lambda i, *_: (i, 0)),
            ],
            out_specs=pl.BlockSpec(memory_space=pl.ANY),
            scratch_shapes=[pltpu.VMEM((2, e * tile * SUBLANES, LANES), F32),
                            pltpu.SemaphoreType.DMA((2,))],
        ),
        out_shape=jax.ShapeDtypeStruct((e * cap * SUBLANES, LANES), F32),
        compiler_params=_params("arbitrary"),
        name="route_dispatch",
    )(n, s0, pos, shift, h1b)


def _combine_kernel(n_ref, s0_ref, pos_ref, aff_ref, shift_ref, h_ref, g_ref, b_ref, ye_hbm, o_ref,
                    stage_ref, sem, *, ntile, cap):
    i = pl.program_id(0)
    slot = i % 2

    def fetch(tile, slot):
        counts, offs = _tile_counts(n_ref, tile, ntile)
        for e in range(N_EXPERTS):
            @pl.when(counts[e] > 0)
            def _(e=e):
                _rows_copy(ye_hbm, stage_ref.at[slot], sem.at[slot],
                           e * cap + s0_ref[e * ntile + tile], offs[e], counts[e]).start()

    @pl.when(i == 0)
    def _():
        stage_ref[...] = jnp.zeros_like(stage_ref)
        fetch(0, 0)

    @pl.when(i + 1 < ntile)
    def _():
        fetch(i + 1, 1 - slot)

    _, offs = _tile_counts(n_ref, i, ntile)

    @pl.when(offs[-1] > 0)
    def _():
        _rows_copy(ye_hbm, stage_ref.at[slot], sem.at[slot], 0, 0, offs[-1]).wait()

    pos = pos_ref[...]
    stagepos = jnp.where(pos >= 0, pos + shift_ref[0], -1)
    gates = aff_ref[...].astype(BF16)

    def chunk(c, acc):
        kbase = pl.multiple_of(c * STAGE_CHUNK, STAGE_CHUNK)
        ye = _load_rows(stage_ref, kbase, STAGE_CHUNK, lead=(slot,)).astype(BF16)
        return acc + lax.dot_general(_one_hot_t(stagepos, kbase, gates), ye,
                                     (((0,), (0,)), ((), ())), preferred_element_type=F32)

    ffn = lax.fori_loop(0, (offs[-1] + STAGE_CHUNK - 1) // STAGE_CHUNK, chunk,
                        jnp.zeros(h_ref.shape, F32))
    z = ALPHA * h_ref[...] + ffn
    mu = jnp.mean(z, axis=-1, keepdims=True)
    zc = z - mu
    var = jnp.mean(zc * zc, axis=-1, keepdims=True)
    o_ref[...] = zc * lax.rsqrt(var + LN_EPS) * g_ref[...] + b_ref[...]


def _combine(n, s0, pos, aff_t, shift, h1, g2, b2, ye, cap):
    e, t = pos.shape
    tile = ROUTE_TILE
    ntile = t // tile
    lanes = pl.BlockSpec((e, tile), lambda i, *_: (0, i))
    vec = pl.BlockSpec((1, D_MODEL), lambda i, *_: (0, 0))
    row = pl.BlockSpec((tile, D_MODEL), lambda i, *_: (i, 0))
    return pl.pallas_call(
        functools.partial(_combine_kernel, ntile=ntile, cap=cap),
        grid_spec=pltpu.PrefetchScalarGridSpec(
            num_scalar_prefetch=2,
            grid=(ntile,),
            in_specs=[lanes, lanes, pl.BlockSpec((1, e, 1), lambda i, *_: (i, 0, 0)), row, vec, vec,
                      pl.BlockSpec(memory_space=pl.ANY)],
            out_specs=row,
            scratch_shapes=[pltpu.VMEM((2, e * tile * SUBLANES, LANES), F32),
                            pltpu.SemaphoreType.DMA((2,))],
        ),
        out_shape=jax.ShapeDtypeStruct((t, D_MODEL), F32),
        compiler_params=_params("arbitrary"),
        name="route_combine_ln2",
    )(n, s0, pos, aff_t, shift, h1, g2, b2, ye)


def _layer(x, p):
    b, l, _ = x.shape
    t = b * l
    cap = CAPACITY_FACTOR * t // N_EXPERTS
    x2 = x.reshape(t, D_MODEL)
    q, k, v, xr, gate = _in_proj(x2, p["w_in"])
    attn = _attention(q.reshape(b, l, -1), k.reshape(b, l, -1), v.reshape(b, l, -1),
                      p["bias"], p["sink_row"], p["attn_g_col"])
    xr3 = xr.reshape(b, l, RNN_WIDTH)
    hs = [_lru(xr3, p["conv_w"], p["conv_b"], p["wa"][d], p["wx"][d], p["ba"][d], p["bx"][d],
               p["lam"][d], reverse=bool(d)) for d in range(2)]
    h1, h1b, aff_t = _mix_out(attn.reshape(t, -1), hs[0].reshape(t, -1), hs[1].reshape(t, -1), gate,
                              x2, p["w_out"], p["rnn_g"], p["ln1_g"], p["ln1_b"], p["wr_t"])
    pos = _route(aff_t, cap)
    n, s0, shift = _tile_tables(pos, ROUTE_TILE)
    xe = _dispatch(n, s0, pos, shift, h1b, cap)
    ye = _ffn(xe, p["wg"], p["wu"], p["wd"], cap)
    out = _combine(n, s0, pos, aff_t, shift, h1, p["ln2_g"], p["ln2_b"], ye, cap)
    return out.reshape(b, l, D_MODEL)


def _layer_params(li, w_in, attn_sink, attn_norm_g, rnn_norm_g, conv_w, conv_b, lru_w_a, lru_b_a,
                  lru_w_x, lru_b_x, lru_lambda, w_out, ln1_g, ln1_b, w_router, w_gate, w_up, w_down,
                  ln2_g, ln2_b):
    vec = lambda a: a[li].reshape(1, -1)
    per_dir = lambda f: [f(d) for d in range(2)]
    return dict(
        w_in=w_in[li].astype(BF16),
        bias=_alibi_bias(),
        sink_row=jnp.repeat(attn_sink[li].astype(F32), BLOCK).reshape(N_KV_HEADS, 1, Q_PER_KV * BLOCK),
        attn_g_col=jnp.broadcast_to(attn_norm_g[li].astype(F32)[:, None], (ATTN_WIDTH, BLOCK)),
        rnn_g=vec(rnn_norm_g),
        conv_w=conv_w[li], conv_b=vec(conv_b),
        wa=per_dir(lambda d: (0.5 * _block_diag(lru_w_a[li, d])).astype(BF16)),
        wx=per_dir(lambda d: (0.5 * _block_diag(lru_w_x[li, d])).astype(BF16)),
        ba=per_dir(lambda d: 0.5 * lru_b_a[li, d].reshape(1, -1)),
        bx=per_dir(lambda d: 0.5 * lru_b_x[li, d].reshape(1, -1)),
        lam=per_dir(lambda d: lru_lambda[li, d].reshape(1, -1)),
        w_out=w_out[li].astype(BF16),
        ln1_g=vec(ln1_g), ln1_b=vec(ln1_b),
        wr_t=w_router[li].T.astype(BF16),
        wg=w_gate[li].astype(BF16), wu=w_up[li].astype(BF16), wd=w_down[li].astype(BF16),
        ln2_g=vec(ln2_g), ln2_b=vec(ln2_b),
    )


def kernel(x_prompt, x_sample, w_in, attn_sink, attn_norm_g, rnn_norm_g, conv_w, conv_b, lru_w_a,
           lru_b_a, lru_w_x, lru_b_x, lru_lambda, w_out, ln1_g, ln1_b, w_router, w_gate, w_up,
           w_down, ln2_g, ln2_b):
    layers = [
        _layer_params(li, w_in, attn_sink, attn_norm_g, rnn_norm_g, conv_w, conv_b, lru_w_a, lru_b_a,
                      lru_w_x, lru_b_x, lru_lambda, w_out, ln1_g, ln1_b, w_router, w_gate, w_up,
                      w_down, ln2_g, ln2_b)
        for li in range(w_in.shape[0])]
    ys = []
    for x in (x_prompt, x_sample):
        for p in layers:
            x = _layer(x, p)
        ys.append(x)
    return tuple(ys)
```

```python
import functools
import math

import jax
import jax.numpy as jnp
from jax import lax
from jax.experimental import pallas as pl
from jax.experimental.pallas import tpu as pltpu

D_MODEL = 1024
HEAD_DIM = 64
N_Q_HEADS = 8
N_KV_HEADS = 2
Q_PER_KV = N_Q_HEADS // N_KV_HEADS
ATTN_WIDTH = N_Q_HEADS * HEAD_DIM
KV_WIDTH = N_KV_HEADS * HEAD_DIM
BLOCK = 128
RNN_WIDTH = 512
RNN_BLOCKS = 8
RNN_BLOCK_W = RNN_WIDTH // RNN_BLOCKS
CONV_WIDTH = 4
LRU_C = 8.0
N_EXPERTS = 16
EXPERT_FF = 2048
CAPACITY_FACTOR = 2
ALPHA = 2.0 ** 0.25
LN_EPS = 1e-5
RMS_EPS = 1e-6
MASKED = -1e30
TINY = 1e-37

SUBLANES = 8
LANES = 128
VMEM_LIMIT = 56 * 1024 * 1024

ROW_TILE = 512
ATTN_QBLOCKS = 8
LRU_CHUNK = 256
LRU_BATCH = SUBLANES
LRU_UNROLL = 8
FFN_ROWS = 512
FFN_FCHUNK = 512
POS_BLOCK = 2048
ROUTE_TILE = 256
STAGE_CHUNK = 192
STATIC_CHUNKS = 3

BF16 = jnp.bfloat16
F32 = jnp.float32


def _params(*sem):
    return pltpu.CompilerParams(dimension_semantics=sem, vmem_limit_bytes=VMEM_LIMIT)


def _in_proj_kernel(x_ref, w_ref, q_ref, k_ref, v_ref, xr_ref, gate_ref):
    xb = x_ref[...].astype(BF16)

    def proj(lo, hi):
        return jnp.dot(xb, w_ref[:, lo:hi], preferred_element_type=F32)

    o = 0
    q_ref[...] = (proj(o, o + ATTN_WIDTH) * (HEAD_DIM ** -0.5)).astype(BF16)
    o += ATTN_WIDTH
    k_ref[...] = proj(o, o + KV_WIDTH).astype(BF16)
    o += KV_WIDTH
    v_ref[...] = proj(o, o + KV_WIDTH).astype(BF16)
    o += KV_WIDTH
    xr_ref[...] = proj(o, o + RNN_WIDTH)
    o += RNN_WIDTH
    gate_ref[...] = proj(o, o + RNN_WIDTH)


def _in_proj(x2, w_bf16):
    t = x2.shape[0]
    in_w = w_bf16.shape[1]
    row = lambda w: pl.BlockSpec((ROW_TILE, w), lambda i: (i, 0))
    return pl.pallas_call(
        _in_proj_kernel,
        grid=(t // ROW_TILE,),
        in_specs=[row(D_MODEL), pl.BlockSpec((D_MODEL, in_w), lambda i: (0, 0))],
        out_specs=[row(ATTN_WIDTH), row(KV_WIDTH), row(KV_WIDTH), row(RNN_WIDTH), row(RNN_WIDTH)],
        out_shape=[
            jax.ShapeDtypeStruct((t, ATTN_WIDTH), BF16),
            jax.ShapeDtypeStruct((t, KV_WIDTH), BF16),
            jax.ShapeDtypeStruct((t, KV_WIDTH), BF16),
            jax.ShapeDtypeStruct((t, RNN_WIDTH), F32),
            jax.ShapeDtypeStruct((t, RNN_WIDTH), F32),
        ],
        compiler_params=_params("parallel"),
        name="in_proj",
    )(x2, w_bf16)


def _attn_kernel(q_ref, kp_ref, kc_ref, kn_ref, vp_ref, vc_ref, vn_ref, bias_first_ref, bias_mid_ref,
                 bias_last_ref, sink_ref, g_ref, o_ref):
    k = jnp.concatenate([kp_ref[0], kc_ref[0], kn_ref[0]], axis=0)
    v = jnp.concatenate([vp_ref[0], vc_ref[0], vn_ref[0]], axis=0)
    v_t = v.astype(F32).T
    pad_rows = 2 * SUBLANES
    ones_row = jnp.where(
        lax.broadcasted_iota(jnp.int32, (pad_rows, 3 * BLOCK), 0) == 0, 1.0, 0.0)
    for qb in range(ATTN_QBLOCKS):
        q = q_ref[0, qb * BLOCK:(qb + 1) * BLOCK, :]
        keys = slice(qb * BLOCK, (qb + 3) * BLOCK)
        bias_ref = (bias_first_ref if qb == 0 else
                    bias_last_ref if qb == ATTN_QBLOCKS - 1 else bias_mid_ref)
        heads = []
        for g in range(N_KV_HEADS):
            qs = jnp.concatenate(
                [q[:, (Q_PER_KV * g + j) * HEAD_DIM:(Q_PER_KV * g + j + 1) * HEAD_DIM]
                 for j in range(Q_PER_KV)], axis=0)
            kg = k[keys, g * HEAD_DIM:(g + 1) * HEAD_DIM]
            s = lax.dot_general(kg, qs, (((1,), (1,)), ((), ())), preferred_element_type=F32)
            s = s + bias_ref[0, g]
            sink = sink_ref[g]
            m = jnp.maximum(jnp.max(s, axis=0, keepdims=True), sink)
            p = jnp.exp(s - m).astype(BF16)
            lhs = jnp.concatenate([v_t[g * HEAD_DIM:(g + 1) * HEAD_DIM, keys], ones_row],
                                  axis=0).astype(BF16)
            o_aug = jnp.dot(lhs, p, preferred_element_type=F32)
            denom = o_aug[HEAD_DIM:HEAD_DIM + 1] + jnp.exp(sink - m)
            o = o_aug[:HEAD_DIM] * (1.0 / denom)
            heads += [o[:, j * BLOCK:(j + 1) * BLOCK] for j in range(Q_PER_KV)]
        y_t = jnp.concatenate(heads, axis=0)
        ms = jnp.mean(y_t * y_t, axis=0, keepdims=True)
        o_ref[0, qb * BLOCK:(qb + 1) * BLOCK, :] = (
            y_t * lax.rsqrt(ms + RMS_EPS) * g_ref[...]).T.astype(BF16)


def _attention(q, k, v, bias, sink_row, attn_g_col):
    b, l, _ = q.shape
    span = ATTN_QBLOCKS * BLOCK
    assert ATTN_QBLOCKS >= 2 and l % span == 0
    ns = l // span
    nb = l // BLOCK
    edge = lambda f: pl.BlockSpec((1, BLOCK, KV_WIDTH), f)
    prev = lambda bi, i: (bi, jnp.maximum(i * ATTN_QBLOCKS - 1, 0), 0)
    cur = lambda bi, i: (bi, i, 0)
    nxt = lambda bi, i: (bi, jnp.minimum((i + 1) * ATTN_QBLOCKS, nb - 1), 0)
    mid = pl.BlockSpec((1, span, KV_WIDTH), cur)
    bias_spec = lambda f: pl.BlockSpec((1,) + bias.shape[1:], f)
    first = lambda bi, i: ((i == 0).astype(jnp.int32), 0, 0, 0)
    last = lambda bi, i: (2 * (i == ns - 1).astype(jnp.int32), 0, 0, 0)
    return pl.pallas_call(
        _attn_kernel,
        grid=(b, ns),
        in_specs=[
            pl.BlockSpec((1, span, ATTN_WIDTH), cur),
            edge(prev), mid, edge(nxt),
            edge(prev), mid, edge(nxt),
            bias_spec(first), bias_spec(lambda bi, i: (0, 0, 0, 0)), bias_spec(last),
            pl.BlockSpec(sink_row.shape, lambda bi, i: (0, 0, 0)),
            pl.BlockSpec(attn_g_col.shape, lambda bi, i: (0, 0)),
        ],
        out_specs=pl.BlockSpec((1, span, ATTN_WIDTH), cur),
        out_shape=jax.ShapeDtypeStruct((b, l, ATTN_WIDTH), BF16),
        compiler_params=_params("parallel", "parallel"),
        name="banded_attention",
    )(q, k, k, k, v, v, v, bias, bias, bias, sink_row, attn_g_col)


def _alibi_bias():
    qi = jnp.arange(BLOCK)[None, :]
    sj = jnp.arange(3 * BLOCK)[:, None]
    rel = sj - BLOCK - qi
    dist = jnp.abs(rel).astype(F32)
    slopes = jnp.asarray([2.0 ** (-8.0 * (h + 1) / N_Q_HEADS) for h in range(N_Q_HEADS)], F32)
    bias = jnp.where((jnp.abs(rel) <= BLOCK)[None], -slopes[:, None, None] * dist[None], MASKED)
    bias = bias.reshape(N_KV_HEADS, Q_PER_KV, 3 * BLOCK, BLOCK).transpose(0, 2, 1, 3)
    bias = bias.reshape(N_KV_HEADS, 3 * BLOCK, Q_PER_KV * BLOCK)
    no_prev = (sj < BLOCK)[None]
    no_next = (sj >= 2 * BLOCK)[None]
    return jnp.stack([
        bias,
        jnp.where(no_prev, MASKED, bias),
        jnp.where(no_next, MASKED, bias),
    ])


def _lru_kernel(x_ref, xp_ref, xn_ref, cw_ref, cb_ref, wa_ref, wx_ref, ba_ref, bx_ref, lam_ref,
                h_ref, a_scr, u_scr, carry_ref, *, nc, lc, reverse):
    c = pl.program_id(1)
    cc = nc - 1 - c if reverse else c
    pitch = lc + SUBLANES
    slabs = RNN_WIDTH // LANES

    @pl.when(c == 0)
    def _():
        carry_ref[...] = jnp.zeros_like(carry_ref)

    row8 = lax.broadcasted_iota(jnp.int32, (SUBLANES, 1), 0)
    w = cw_ref[...]
    lam = lam_ref[...]
    softplus_neg_lam = jnp.maximum(-lam, 0.0) + jnp.log1p(jnp.exp(-jnp.abs(lam)))
    half_rate = (-0.5 * LRU_C) * softplus_neg_lam
    for b in range(LRU_BATCH):
        x = x_ref[b]
        prev8 = jnp.where(cc > 0, xp_ref[b], 0.0)
        next8 = jnp.where(cc < nc - 1, xn_ref[b], 0.0)

        def shifted_down(k):
            r = pltpu.roll(x, k, 0)
            top = jnp.where(row8 < k, pltpu.roll(prev8, k, 0), r[:SUBLANES])
            return jnp.concatenate([top, r[SUBLANES:]], axis=0)

        r1 = pltpu.roll(x, lc - 1, 0)
        bot = jnp.where(row8 == SUBLANES - 1, pltpu.roll(next8, SUBLANES - 1, 0),
                        r1[lc - SUBLANES:])
        x_p1 = jnp.concatenate([r1[:lc - SUBLANES], bot], axis=0)

        xc = shifted_down(2) * w[0:1] + shifted_down(1) * w[1:2]
        xc = xc + x * w[2:3]
        xc = xc + x_p1 * w[3:4]
        xc = xc + cb_ref[...]

        xcb = xc.astype(BF16)
        th_r = jnp.tanh(jnp.dot(xcb, wa_ref[...], preferred_element_type=F32) + ba_ref[...])
        th_i = jnp.tanh(jnp.dot(xcb, wx_ref[...], preferred_element_type=F32) + bx_ref[...])
        log_a = th_r * half_rate + half_rate
        a = jnp.exp(log_a)
        z = jnp.tanh(log_a) * (-1.0 - a * a)
        root = z * lax.rsqrt(jnp.maximum(z, TINY))
        u = root * ((0.5 * th_i + 0.5) * xc)
        for j in range(slabs):
            a_scr[j, b * pitch:b * pitch + lc, :] = a[:, j * LANES:(j + 1) * LANES]
            u_scr[j, b * pitch:b * pitch + lc, :] = u[:, j * LANES:(j + 1) * LANES]

    def step(i, hs):
        t = lc - 1 - i if reverse else i
        out = []
        for j in range(slabs):
            rows = pl.ds(t, LRU_BATCH, stride=pitch)
            h = a_scr[j, rows, :] * hs[j] + u_scr[j, rows, :]
            u_scr[j, rows, :] = h
            out.append(h)
        return tuple(out)

    hs = lax.fori_loop(0, lc, step, tuple(carry_ref[j] for j in range(slabs)), unroll=LRU_UNROLL)
    for j in range(slabs):
        carry_ref[j] = hs[j]
        for b in range(LRU_BATCH):
            h_ref[b, :, j * LANES:(j + 1) * LANES] = u_scr[j, b * pitch:b * pitch + lc, :]


def _lru(xr, conv_w, conv_b, wa, wx, ba, bx, lam, *, reverse):
    b, l, _ = xr.shape
    lc = LRU_CHUNK
    assert b % LRU_BATCH == 0 and l % lc == 0
    nc = l // lc
    per8 = lc // SUBLANES
    n8 = l // SUBLANES
    pos = (lambda c: nc - 1 - c) if reverse else (lambda c: c)
    const = lambda bi, c: (0, 0)
    vec = pl.BlockSpec((1, RNN_WIDTH), const)
    mat = pl.BlockSpec((RNN_WIDTH, RNN_WIDTH), const)
    slabs = RNN_WIDTH // LANES
    scratch = pltpu.VMEM((slabs, LRU_BATCH * (lc + SUBLANES), LANES), F32)
    return pl.pallas_call(
        functools.partial(_lru_kernel, nc=nc, lc=lc, reverse=reverse),
        grid=(b // LRU_BATCH, nc),
        in_specs=[
            pl.BlockSpec((LRU_BATCH, lc, RNN_WIDTH), lambda bi, c: (bi, pos(c), 0)),
            pl.BlockSpec((LRU_BATCH, SUBLANES, RNN_WIDTH),
                         lambda bi, c: (bi, jnp.maximum(pos(c) * per8 - 1, 0), 0)),
            pl.BlockSpec((LRU_BATCH, SUBLANES, RNN_WIDTH),
                         lambda bi, c: (bi, jnp.minimum((pos(c) + 1) * per8, n8 - 1), 0)),
            pl.BlockSpec((CONV_WIDTH, RNN_WIDTH), const), vec, mat, mat, vec, vec, vec,
        ],
        out_specs=pl.BlockSpec((LRU_BATCH, lc, RNN_WIDTH), lambda bi, c: (bi, pos(c), 0)),
        out_shape=jax.ShapeDtypeStruct((b, l, RNN_WIDTH), F32),
        scratch_shapes=[scratch, scratch, pltpu.VMEM((slabs, LRU_BATCH, LANES), F32)],
        compiler_params=_params("parallel", "arbitrary"),
        name="rg_lru_bwd" if reverse else "rg_lru_fwd",
    )(xr, xr, xr, conv_w, conv_b, wa, wx, ba, bx, lam)


def _block_diag(w):
    eye = jnp.eye(RNN_BLOCKS, dtype=w.dtype)
    return jnp.einsum('hij,hk->hikj', w, eye).reshape(RNN_WIDTH, RNN_WIDTH)


def _mix_out_kernel(attn_ref, hf_ref, hb_ref, gate_ref, x_ref, wo_ref, rg_ref, g1_ref, b1_ref,
                    wr_ref, h_ref, hb16_ref, aff_ref):
    gt = gate_ref[...]
    gelu = 0.5 * gt * (1.0 + jnp.tanh(math.sqrt(2.0 / math.pi) * (gt + 0.044715 * (gt * gt * gt))))
    yr = (hf_ref[...] + hb_ref[...]) * gelu
    ms = jnp.mean(yr * yr, axis=-1, keepdims=True)
    yn = (yr * lax.rsqrt(ms + RMS_EPS) * rg_ref[...]).astype(BF16)
    mix = jnp.dot(attn_ref[...], wo_ref[:ATTN_WIDTH, :], preferred_element_type=F32)
    mix = mix + jnp.dot(yn, wo_ref[ATTN_WIDTH:, :], preferred_element_type=F32)
    z = ALPHA * x_ref[...] + mix
    mu = jnp.mean(z, axis=-1, keepdims=True)
    zc = z - mu
    var = jnp.mean(zc * zc, axis=-1, keepdims=True)
    h = zc * lax.rsqrt(var + LN_EPS) * g1_ref[...] + b1_ref[...]
    h_ref[...] = h
    hb = h.astype(BF16)
    hb16_ref[...] = hb
    logits = lax.dot_general(wr_ref[...], hb, (((1,), (1,)), ((), ())),
                             preferred_element_type=F32)
    mx = jnp.max(logits, axis=0, keepdims=True)
    e = jnp.exp(logits - mx)
    aff_ref[...] = e / jnp.sum(e, axis=0, keepdims=True)


def _mix_out(attn, hf, hb, gate, x2, wo, rnn_g, g1, b1, wr_t):
    t = x2.shape[0]
    row = lambda w: pl.BlockSpec((ROW_TILE, w), lambda i: (i, 0))
    const = lambda i: (0, 0)
    return pl.pallas_call(
        _mix_out_kernel,
        grid=(t // ROW_TILE,),
        in_specs=[
            row(ATTN_WIDTH), row(RNN_WIDTH), row(RNN_WIDTH), row(RNN_WIDTH), row(D_MODEL),
            pl.BlockSpec((D_MODEL, D_MODEL), const),
            pl.BlockSpec((1, RNN_WIDTH), const),
            pl.BlockSpec((1, D_MODEL), const),
            pl.BlockSpec((1, D_MODEL), const),
            pl.BlockSpec((N_EXPERTS, D_MODEL), const),
        ],
        out_specs=[row(D_MODEL), row(D_MODEL), pl.BlockSpec((N_EXPERTS, ROW_TILE), lambda i: (0, i))],
        out_shape=[
            jax.ShapeDtypeStruct((t, D_MODEL), F32),
            jax.ShapeDtypeStruct((t, D_MODEL), BF16),
            jax.ShapeDtypeStruct((N_EXPERTS, t), F32),
        ],
        compiler_params=_params("parallel"),
        name="mix_out_ln1_router",
    )(attn, hf, hb, gate, x2, wo, rnn_g, g1, b1, wr_t)


def _load_rows(ref, start, n, lead=()):
    return jnp.concatenate(
        [ref[lead + (pl.ds(start * SUBLANES + j, n, stride=SUBLANES), slice(None))]
         for j in range(D_MODEL // LANES)], axis=1)


def _store_rows(ref, start, rows, lead=()):
    n = rows.shape[0]
    for j in range(D_MODEL // LANES):
        ref[lead + (pl.ds(start * SUBLANES + j, n, stride=SUBLANES), slice(None))] = (
            rows[:, j * LANES:(j + 1) * LANES])


def _ffn_kernel(x_ref, wg_ref, wu_ref, wd_ref, o_ref):
    x = _load_rows(x_ref, 0, FFN_ROWS).astype(BF16)
    acc = jnp.zeros((FFN_ROWS, D_MODEL), F32)
    for f in range(EXPERT_FF // FFN_FCHUNK):
        cols = slice(f * FFN_FCHUNK, (f + 1) * FFN_FCHUNK)
        gt = jnp.dot(x, wg_ref[0, :, cols], preferred_element_type=F32)
        up = jnp.dot(x, wu_ref[0, :, cols], preferred_element_type=F32)
        hid = (gt * jax.nn.sigmoid(gt) * up).astype(BF16)
        acc = acc + jnp.dot(hid, wd_ref[0, cols, :], preferred_element_type=F32)
    _store_rows(o_ref, 0, acc)


def _ffn(xe, wg, wu, wd, cap):
    e = wg.shape[0]
    per_e = cap // FFN_ROWS
    slots = pl.BlockSpec((FFN_ROWS * SUBLANES, LANES), lambda ei, j: (ei * per_e + j, 0))
    wspec = lambda a, b: pl.BlockSpec((1, a, b), lambda ei, j: (ei, 0, 0))
    return pl.pallas_call(
        _ffn_kernel,
        grid=(e, per_e),
        in_specs=[slots, wspec(D_MODEL, EXPERT_FF), wspec(D_MODEL, EXPERT_FF),
                  wspec(EXPERT_FF, D_MODEL)],
        out_specs=slots,
        out_shape=jax.ShapeDtypeStruct(xe.shape, F32),
        compiler_params=_params("parallel", "arbitrary"),
        name="expert_ffn",
    )(xe, wg, wu, wd)


def _threshold_kernel(aff_ref, thr_ref, need_ref, *, cap):
    def count(mask):
        return jnp.sum(jnp.where(mask, 1.0, 0.0), axis=1, keepdims=True)

    def body(b, thr):
        cand = thr | jnp.left_shift(jnp.int32(1), 30 - b)
        bits = pltpu.bitcast(aff_ref[...], jnp.int32)
        return jnp.where(count(bits >= cand) >= cap, cand, thr)

    thr = lax.fori_loop(0, 31, body, jnp.zeros((N_EXPERTS, 1), jnp.int32))
    bits = pltpu.bitcast(aff_ref[...], jnp.int32)
    thr_ref[...] = thr
    need_ref[...] = cap - count(bits > thr)


def _positions_kernel(aff_ref, thr_ref, need_ref, pos_ref, tie_run_ref, sel_run_ref, *, block):
    @pl.when(pl.program_id(0) == 0)
    def _():
        tie_run_ref[...] = jnp.zeros_like(tie_run_ref)
        sel_run_ref[...] = jnp.zeros_like(sel_run_ref)

    thr = thr_ref[...]
    need = need_ref[...]
    r = lax.broadcasted_iota(jnp.int32, (LANES, LANES), 0)
    c = lax.broadcasted_iota(jnp.int32, (LANES, LANES), 1)
    before = jnp.where(r < c, 1.0, 0.0).astype(BF16)
    tie_run = tie_run_ref[...]
    sel_run = sel_run_ref[...]
    for s in range(block // LANES):
        lanes = slice(s * LANES, (s + 1) * LANES)
        bits = pltpu.bitcast(aff_ref[:, lanes], jnp.int32)
        tie = jnp.where(bits == thr, 1.0, 0.0)
        tie_before = jnp.dot(tie.astype(BF16), before, preferred_element_type=F32) + tie_run
        sel = (bits > thr) | ((bits == thr) & (tie_before < need))
        picked = jnp.where(sel, 1.0, 0.0)
        sel_before = jnp.dot(picked.astype(BF16), before, preferred_element_type=F32) + sel_run
        pos_ref[:, lanes] = jnp.where(sel, sel_before.astype(jnp.int32), -1)
        tie_run = tie_run + jnp.sum(tie, axis=1, keepdims=True)
        sel_run = sel_run + jnp.sum(picked, axis=1, keepdims=True)
    tie_run_ref[...] = tie_run
    sel_run_ref[...] = sel_run


def _route(aff_t, cap):
    e, t = aff_t.shape
    col = jax.ShapeDtypeStruct((e, 1), jnp.int32)
    thr, need = pl.pallas_call(
        functools.partial(_threshold_kernel, cap=cap),
        out_shape=[col, jax.ShapeDtypeStruct((e, 1), F32)],
        compiler_params=_params(),
        name="route_threshold",
    )(aff_t)
    block = min(POS_BLOCK, t)
    cspec = pl.BlockSpec((e, 1), lambda i: (0, 0))
    return pl.pallas_call(
        functools.partial(_positions_kernel, block=block),
        grid=(t // block,),
        in_specs=[pl.BlockSpec((e, block), lambda i: (0, i)), cspec, cspec],
        out_specs=pl.BlockSpec((e, block), lambda i: (0, i)),
        out_shape=jax.ShapeDtypeStruct((e, t), jnp.int32),
        scratch_shapes=[pltpu.VMEM((e, 1), F32), pltpu.VMEM((e, 1), F32)],
        compiler_params=_params("arbitrary"),
        name="route_positions",
    )(aff_t, thr, need)


def _tile_tables(pos, tile):
    e, t = pos.shape
    n = jnp.sum((pos >= 0).reshape(e, t // tile, tile), axis=-1, dtype=jnp.int32)
    s0 = jnp.cumsum(n, axis=1) - n
    off = jnp.cumsum(n, axis=0) - n
    shift = (off - s0).T.reshape(t // tile, e, 1)
    return n.reshape(-1), s0.reshape(-1), shift


def _tile_counts(n_ref, tile, ntile):
    counts = [n_ref[e * ntile + tile] for e in range(N_EXPERTS)]
    offs = [jnp.int32(0)]
    for c in counts:
        offs.append(offs[-1] + c)
    return counts, offs


def _one_hot_t(stagepos, kbase, values=None):
    rel = stagepos - kbase
    rel = jnp.where((rel >= 0) & (rel < STAGE_CHUNK), rel, -1).astype(F32).astype(BF16)
    k = lax.broadcasted_iota(jnp.int32, (STAGE_CHUNK, stagepos.shape[1]), 0).astype(F32).astype(BF16)
    one = jnp.ones((1, stagepos.shape[1]), BF16)
    pt = jnp.zeros(k.shape, BF16)
    for e in range(N_EXPERTS):
        hit = rel[e:e + 1, :] == k
        pt = jnp.where(hit, one if values is None else values[e:e + 1, :], pt)
    return pt


def _stage_rows(e, tile):
    chunks = max(STATIC_CHUNKS, -(-e * tile // STAGE_CHUNK))
    return chunks * STAGE_CHUNK


def _for_each_chunk(total, body, init):
    carry = init
    for c in range(STATIC_CHUNKS):
        carry = body(c * STAGE_CHUNK, carry)
    return lax.fori_loop(
        STATIC_CHUNKS, (total + STAGE_CHUNK - 1) // STAGE_CHUNK,
        lambda c, carry: body(pl.multiple_of(c * STAGE_CHUNK, STAGE_CHUNK), carry), carry)


def _rows_copy(src, dst, sem, src_row, dst_row, n):
    return pltpu.make_async_copy(
        src.at[pl.ds(pl.multiple_of(src_row * SUBLANES, SUBLANES), n * SUBLANES)],
        dst.at[pl.ds(pl.multiple_of(dst_row * SUBLANES, SUBLANES), n * SUBLANES)], sem)


def _dispatch_kernel(n_ref, s0_ref, pos_ref, shift_ref, h_ref, xe_hbm, stage_ref, sem, *, ntile, cap):
    i = pl.program_id(0)
    slot = i % 2

    def wait_writes(tile, slot):
        _, offs = _tile_counts(n_ref, tile, ntile)

        @pl.when(offs[-1] > 0)
        def _():
            _rows_copy(stage_ref.at[slot], xe_hbm, sem.at[slot], 0, 0, offs[-1]).wait()

    @pl.when(i >= 2)
    def _():
        wait_writes(i - 2, slot)

    counts, offs = _tile_counts(n_ref, i, ntile)
    pos = pos_ref[...]
    stagepos = jnp.where(pos >= 0, pos + shift_ref[0], -1)

    def chunk(kbase, carry):
        rows = jnp.dot(_one_hot_t(stagepos, kbase), h_ref[...], preferred_element_type=F32)
        _store_rows(stage_ref, kbase, rows, lead=(slot,))
        return carry

    _for_each_chunk(offs[-1], chunk, 0)

    for e in range(N_EXPERTS):
        @pl.when(counts[e] > 0)
        def _(e=e):
            _rows_copy(stage_ref.at[slot], xe_hbm, sem.at[slot], offs[e],
                       e * cap + s0_ref[e * ntile + i], counts[e]).start()

    @pl.when(i == ntile - 1)
    def _():
        wait_writes(i, slot)
        if ntile > 1:
            wait_writes(i - 1, 1 - slot)


def _dispatch(n, s0, pos, shift, h1b, cap):
    e, t = pos.shape
    tile = ROUTE_TILE
    ntile = t // tile
    return pl.pallas_call(
        functools.partial(_dispatch_kernel, ntile=ntile, cap=cap),
        grid_spec=pltpu.PrefetchScalarGridSpec(
            num_scalar_prefetch=2,
            grid=(ntile,),
            in_specs=[
                pl.BlockSpec((e, tile), lambda i, *_: (0, i)),
                pl.BlockSpec((1, e, 1), lambda i, *_: (i, 0, 0)),
                pl.BlockSpec((tile, D_MODEL), lambda i, *_: (i, 0)),
            ],
            out_specs=pl.BlockSpec(memory_space=pl.ANY),
            scratch_shapes=[pltpu.VMEM((2, _stage_rows(e, tile) * SUBLANES, LANES), F32),
                            pltpu.SemaphoreType.DMA((2,))],
        ),
        out_shape=jax.ShapeDtypeStruct((e * cap * SUBLANES, LANES), F32),
        compiler_params=_params("arbitrary"),
        name="route_dispatch",
    )(n, s0, pos, shift, h1b)


def _combine_kernel(n_ref, s0_ref, pos_ref, aff_ref, shift_ref, h_ref, g_ref, b_ref, ye_hbm, o_ref,
                    stage_ref, sem, *, ntile, cap):
    i = pl.program_id(0)
    slot = i % 2

    def fetch(tile, slot):
        counts, offs = _tile_counts(n_ref, tile, ntile)
        for e in range(N_EXPERTS):
            @pl.when(counts[e] > 0)
            def _(e=e):
                _rows_copy(ye_hbm, stage_ref.at[slot], sem.at[slot],
                           e * cap + s0_ref[e * ntile + tile], offs[e], counts[e]).start()

    @pl.when(i == 0)
    def _():
        stage_ref[...] = jnp.zeros_like(stage_ref)
        fetch(0, 0)

    @pl.when(i + 1 < ntile)
    def _():
        fetch(i + 1, 1 - slot)

    _, offs = _tile_counts(n_ref, i, ntile)

    @pl.when(offs[-1] > 0)
    def _():
        _rows_copy(ye_hbm, stage_ref.at[slot], sem.at[slot], 0, 0, offs[-1]).wait()

    pos = pos_ref[...]
    stagepos = jnp.where(pos >= 0, pos + shift_ref[0], -1)
    gates = aff_ref[...].astype(BF16)

    def chunk(kbase, acc):
        ye = _load_rows(stage_ref, kbase, STAGE_CHUNK, lead=(slot,)).astype(BF16)
        return acc + lax.dot_general(_one_hot_t(stagepos, kbase, gates), ye,
                                     (((0,), (0,)), ((), ())), preferred_element_type=F32)

    z = _for_each_chunk(offs[-1], chunk, ALPHA * h_ref[...])
    mu = jnp.mean(z, axis=-1, keepdims=True)
    zc = z - mu
    var = jnp.mean(zc * zc, axis=-1, keepdims=True)
    o_ref[...] = zc * lax.rsqrt(var + LN_EPS) * g_ref[...] + b_ref[...]


def _combine(n, s0, pos, aff_t, shift, h1, g2, b2, ye, cap):
    e, t = pos.shape
    tile = ROUTE_TILE
    ntile = t // tile
    lanes = pl.BlockSpec((e, tile), lambda i, *_: (0, i))
    vec = pl.BlockSpec((1, D_MODEL), lambda i, *_: (0, 0))
    row = pl.BlockSpec((tile, D_MODEL), lambda i, *_: (i, 0))
    return pl.pallas_call(
        functools.partial(_combine_kernel, ntile=ntile, cap=cap),
        grid_spec=pltpu.PrefetchScalarGridSpec(
            num_scalar_prefetch=2,
            grid=(ntile,),
            in_specs=[lanes, lanes, pl.BlockSpec((1, e, 1), lambda i, *_: (i, 0, 0)), row, vec, vec,
                      pl.BlockSpec(memory_space=pl.ANY)],
            out_specs=row,
            scratch_shapes=[pltpu.VMEM((2, _stage_rows(e, tile) * SUBLANES, LANES), F32),
                            pltpu.SemaphoreType.DMA((2,))],
        ),
        out_shape=jax.ShapeDtypeStruct((t, D_MODEL), F32),
        compiler_params=_params("arbitrary"),
        name="route_combine_ln2",
    )(n, s0, pos, aff_t, shift, h1, g2, b2, ye)


def _layer(x, p):
    b, l, _ = x.shape
    t = b * l
    cap = CAPACITY_FACTOR * t // N_EXPERTS
    x2 = x.reshape(t, D_MODEL)
    q, k, v, xr, gate = _in_proj(x2, p["w_in"])
    attn = _attention(q.reshape(b, l, -1), k.reshape(b, l, -1), v.reshape(b, l, -1),
                      p["bias"], p["sink_row"], p["attn_g_col"])
    xr3 = xr.reshape(b, l, RNN_WIDTH)
    hs = [_lru(xr3, p["conv_w"], p["conv_b"], p["wa"][d], p["wx"][d], p["ba"][d], p["bx"][d],
               p["lam"][d], reverse=bool(d)) for d in range(2)]
    h1, h1b, aff_t = _mix_out(attn.reshape(t, -1), hs[0].reshape(t, -1), hs[1].reshape(t, -1), gate,
                              x2, p["w_out"], p["rnn_g"], p["ln1_g"], p["ln1_b"], p["wr_t"])
    pos = _route(aff_t, cap)
    n, s0, shift = _tile_tables(pos, ROUTE_TILE)
    xe = _dispatch(n, s0, pos, shift, h1b, cap)
    ye = _ffn(xe, p["wg"], p["wu"], p["wd"], cap)
    out = _combine(n, s0, pos, aff_t, shift, h1, p["ln2_g"], p["ln2_b"], ye, cap)
    return out.reshape(b, l, D_MODEL)


def _layer_params(li, w_in, attn_sink, attn_norm_g, rnn_norm_g, conv_w, conv_b, lru_w_a, lru_b_a,
                  lru_w_x, lru_b_x, lru_lambda, w_out, ln1_g, ln1_b, w_router, w_gate, w_up, w_down,
                  ln2_g, ln2_b):
    vec = lambda a: a[li].reshape(1, -1)
    per_dir = lambda f: [f(d) for d in range(2)]
    return dict(
        w_in=w_in[li].astype(BF16),
        bias=_alibi_bias(),
        sink_row=jnp.repeat(attn_sink[li].astype(F32), BLOCK).reshape(N_KV_HEADS, 1, Q_PER_KV * BLOCK),
        attn_g_col=jnp.broadcast_to(attn_norm_g[li].astype(F32)[:, None], (ATTN_WIDTH, BLOCK)),
        rnn_g=vec(rnn_norm_g),
        conv_w=conv_w[li], conv_b=vec(conv_b),
        wa=per_dir(lambda d: (0.5 * _block_diag(lru_w_a[li, d])).astype(BF16)),
        wx=per_dir(lambda d: (0.5 * _block_diag(lru_w_x[li, d])).astype(BF16)),
        ba=per_dir(lambda d: 0.5 * lru_b_a[li, d].reshape(1, -1)),
        bx=per_dir(lambda d: 0.5 * lru_b_x[li, d].reshape(1, -1)),
        lam=per_dir(lambda d: lru_lambda[li, d].reshape(1, -1)),
        w_out=w_out[li].astype(BF16),
        ln1_g=vec(ln1_g), ln1_b=vec(ln1_b),
        wr_t=w_router[li].T.astype(BF16),
        wg=w_gate[li].astype(BF16), wu=w_up[li].astype(BF16), wd=w_down[li].astype(BF16),
        ln2_g=vec(ln2_g), ln2_b=vec(ln2_b),
    )


def kernel(x_prompt, x_sample, w_in, attn_sink, attn_norm_g, rnn_norm_g, conv_w, conv_b, lru_w_a,
           lru_b_a, lru_w_x, lru_b_x, lru_lambda, w_out, ln1_g, ln1_b, w_router, w_gate, w_up,
           w_down, ln2_g, ln2_b):
    layers = [
        _layer_params(li, w_in, attn_sink, attn_norm_g, rnn_norm_g, conv_w, conv_b, lru_w_a, lru_b_a,
                      lru_w_x, lru_b_x, lru_lambda, w_out, ln1_g, ln1_b, w_router, w_gate, w_up,
                      w_down, ln2_g, ln2_b)
        for li in range(w_in.shape[0])]
    ys = []
    for x in (x_prompt, x_sample):
        for p in layers:
            x = _layer(x, p)
        ys.append(x)
    return tuple(ys)
```

```python
import functools
import math

import jax
import jax.numpy as jnp
from jax import lax
from jax.experimental import pallas as pl
from jax.experimental.pallas import tpu as pltpu

D_MODEL = 1024
HEAD_DIM = 64
N_Q_HEADS = 8
N_KV_HEADS = 2
Q_PER_KV = N_Q_HEADS // N_KV_HEADS
ATTN_WIDTH = N_Q_HEADS * HEAD_DIM
KV_WIDTH = N_KV_HEADS * HEAD_DIM
BLOCK = 128
RNN_WIDTH = 512
RNN_BLOCKS = 8
RNN_BLOCK_W = RNN_WIDTH // RNN_BLOCKS
CONV_WIDTH = 4
LRU_C = 8.0
N_EXPERTS = 16
EXPERT_FF = 2048
CAPACITY_FACTOR = 2
ALPHA = 2.0 ** 0.25
LN_EPS = 1e-5
RMS_EPS = 1e-6
MASKED = -1e30
TINY = 1e-37

SUBLANES = 8
LANES = 128
VMEM_LIMIT = 56 * 1024 * 1024

ROW_TILE = 512
HALO = 2 * SUBLANES
ATTN_QBLOCKS = 8
LRU_CHUNK = 256
LRU_BATCH = SUBLANES
LRU_UNROLL = 8
FFN_ROWS = 512
FFN_FCHUNK = 512
POS_BLOCK = 2048
ROUTE_TILE = 256
STAGE_CHUNK = 192
STATIC_CHUNKS = 3

BF16 = jnp.bfloat16
F32 = jnp.float32


def _params(*sem):
    return pltpu.CompilerParams(dimension_semantics=sem, vmem_limit_bytes=VMEM_LIMIT)


def _in_proj_kernel(x_ref, xp_ref, xn_ref, w_ref, cw_ref, cb_ref, q_ref, k_ref, v_ref, xc_ref, gate_ref,
                    *, tiles_per_seq):
    i = pl.program_id(0)
    xb = x_ref[...].astype(BF16)

    def proj(lo, hi):
        return jnp.dot(xb, w_ref[:, lo:hi], preferred_element_type=F32)

    q_lo, k_lo, v_lo = 0, ATTN_WIDTH, ATTN_WIDTH + KV_WIDTH
    r_lo = v_lo + KV_WIDTH
    g_lo = r_lo + RNN_WIDTH

    ext_b = jnp.concatenate([xp_ref[...].astype(BF16), xb, xn_ref[...].astype(BF16)], axis=0)
    ext = jnp.dot(ext_b, w_ref[:, r_lo:g_lo], preferred_element_type=F32)
    first = i % tiles_per_seq == 0
    last = i % tiles_per_seq == tiles_per_seq - 1
    n = ROW_TILE + 2 * HALO
    ext = jnp.concatenate([jnp.where(first, 0.0, ext[:HALO]), ext[HALO:HALO + ROW_TILE],
                           jnp.where(last, 0.0, ext[HALO + ROW_TILE:])], axis=0)
    mid = slice(HALO, HALO + ROW_TILE)
    w = cw_ref[...]
    xc = pltpu.roll(ext, 2, 0)[mid] * w[0:1] + pltpu.roll(ext, 1, 0)[mid] * w[1:2]
    xc = xc + ext[mid] * w[2:3]
    xc = xc + pltpu.roll(ext, n - 1, 0)[mid] * w[3:4]
    xc_ref[...] = xc + cb_ref[...]

    q_ref[...] = (proj(q_lo, k_lo) * (HEAD_DIM ** -0.5)).astype(BF16)
    k_ref[...] = proj(k_lo, v_lo).astype(BF16)
    v_ref[...] = proj(v_lo, r_lo).astype(BF16)
    gate_ref[...] = proj(g_lo, g_lo + RNN_WIDTH)


def _in_proj(x2, w_bf16, conv_w, conv_b, seq_len):
    t = x2.shape[0]
    in_w = w_bf16.shape[1]
    assert seq_len % ROW_TILE == 0 and ROW_TILE % HALO == 0
    per = ROW_TILE // HALO
    row = lambda w: pl.BlockSpec((ROW_TILE, w), lambda i: (i, 0))
    const = lambda i: (0, 0)
    return pl.pallas_call(
        functools.partial(_in_proj_kernel, tiles_per_seq=seq_len // ROW_TILE),
        grid=(t // ROW_TILE,),
        in_specs=[
            row(D_MODEL),
            pl.BlockSpec((HALO, D_MODEL), lambda i: (jnp.maximum(i * per - 1, 0), 0)),
            pl.BlockSpec((HALO, D_MODEL), lambda i: (jnp.minimum((i + 1) * per, t // HALO - 1), 0)),
            pl.BlockSpec((D_MODEL, in_w), const),
            pl.BlockSpec((CONV_WIDTH, RNN_WIDTH), const),
            pl.BlockSpec((1, RNN_WIDTH), const),
        ],
        out_specs=[row(ATTN_WIDTH), row(KV_WIDTH), row(KV_WIDTH), row(RNN_WIDTH), row(RNN_WIDTH)],
        out_shape=[
            jax.ShapeDtypeStruct((t, ATTN_WIDTH), BF16),
            jax.ShapeDtypeStruct((t, KV_WIDTH), BF16),
            jax.ShapeDtypeStruct((t, KV_WIDTH), BF16),
            jax.ShapeDtypeStruct((t, RNN_WIDTH), F32),
            jax.ShapeDtypeStruct((t, RNN_WIDTH), F32),
        ],
        compiler_params=_params("parallel"),
        name="in_proj",
    )(x2, x2, x2, w_bf16, conv_w, conv_b)


def _attn_kernel(q_ref, kp_ref, kc_ref, kn_ref, vp_ref, vc_ref, vn_ref, bias_first_ref, bias_mid_ref,
                 bias_last_ref, sink_ref, g_ref, o_ref):
    k = jnp.concatenate([kp_ref[0], kc_ref[0], kn_ref[0]], axis=0)
    v = jnp.concatenate([vp_ref[0], vc_ref[0], vn_ref[0]], axis=0)
    v_t = v.astype(F32).T
    pad_rows = 2 * SUBLANES
    ones_row = jnp.where(
        lax.broadcasted_iota(jnp.int32, (pad_rows, 3 * BLOCK), 0) == 0, 1.0, 0.0)
    for qb in range(ATTN_QBLOCKS):
        q = q_ref[0, qb * BLOCK:(qb + 1) * BLOCK, :]
        keys = slice(qb * BLOCK, (qb + 3) * BLOCK)
        bias_ref = (bias_first_ref if qb == 0 else
                    bias_last_ref if qb == ATTN_QBLOCKS - 1 else bias_mid_ref)
        heads = []
        for g in range(N_KV_HEADS):
            qs = jnp.concatenate(
                [q[:, (Q_PER_KV * g + j) * HEAD_DIM:(Q_PER_KV * g + j + 1) * HEAD_DIM]
                 for j in range(Q_PER_KV)], axis=0)
            kg = k[keys, g * HEAD_DIM:(g + 1) * HEAD_DIM]
            s = lax.dot_general(kg, qs, (((1,), (1,)), ((), ())), preferred_element_type=F32)
            s = s + bias_ref[0, g]
            sink = sink_ref[g]
            m = jnp.maximum(jnp.max(s, axis=0, keepdims=True), sink)
            p = jnp.exp(s - m).astype(BF16)
            lhs = jnp.concatenate([v_t[g * HEAD_DIM:(g + 1) * HEAD_DIM, keys], ones_row],
                                  axis=0).astype(BF16)
            o_aug = jnp.dot(lhs, p, preferred_element_type=F32)
            denom = o_aug[HEAD_DIM:HEAD_DIM + 1] + jnp.exp(sink - m)
            o = o_aug[:HEAD_DIM] * (1.0 / denom)
            heads += [o[:, j * BLOCK:(j + 1) * BLOCK] for j in range(Q_PER_KV)]
        y_t = jnp.concatenate(heads, axis=0)
        ms = jnp.mean(y_t * y_t, axis=0, keepdims=True)
        o_ref[0, qb * BLOCK:(qb + 1) * BLOCK, :] = (
            y_t * lax.rsqrt(ms + RMS_EPS) * g_ref[...]).T.astype(BF16)


def _attention(q, k, v, bias, sink_row, attn_g_col):
    b, l, _ = q.shape
    span = ATTN_QBLOCKS * BLOCK
    assert ATTN_QBLOCKS >= 2 and l % span == 0
    ns = l // span
    nb = l // BLOCK
    edge = lambda f: pl.BlockSpec((1, BLOCK, KV_WIDTH), f)
    prev = lambda bi, i: (bi, jnp.maximum(i * ATTN_QBLOCKS - 1, 0), 0)
    cur = lambda bi, i: (bi, i, 0)
    nxt = lambda bi, i: (bi, jnp.minimum((i + 1) * ATTN_QBLOCKS, nb - 1), 0)
    mid = pl.BlockSpec((1, span, KV_WIDTH), cur)
    bias_spec = lambda f: pl.BlockSpec((1,) + bias.shape[1:], f)
    first = lambda bi, i: ((i == 0).astype(jnp.int32), 0, 0, 0)
    last = lambda bi, i: (2 * (i == ns - 1).astype(jnp.int32), 0, 0, 0)
    return pl.pallas_call(
        _attn_kernel,
        grid=(b, ns),
        in_specs=[
            pl.BlockSpec((1, span, ATTN_WIDTH), cur),
            edge(prev), mid, edge(nxt),
            edge(prev), mid, edge(nxt),
            bias_spec(first), bias_spec(lambda bi, i: (0, 0, 0, 0)), bias_spec(last),
            pl.BlockSpec(sink_row.shape, lambda bi, i: (0, 0, 0)),
            pl.BlockSpec(attn_g_col.shape, lambda bi, i: (0, 0)),
        ],
        out_specs=pl.BlockSpec((1, span, ATTN_WIDTH), cur),
        out_shape=jax.ShapeDtypeStruct((b, l, ATTN_WIDTH), BF16),
        compiler_params=_params("parallel", "parallel"),
        name="banded_attention",
    )(q, k, k, k, v, v, v, bias, bias, bias, sink_row, attn_g_col)


def _alibi_bias():
    qi = jnp.arange(BLOCK)[None, :]
    sj = jnp.arange(3 * BLOCK)[:, None]
    rel = sj - BLOCK - qi
    dist = jnp.abs(rel).astype(F32)
    slopes = jnp.asarray([2.0 ** (-8.0 * (h + 1) / N_Q_HEADS) for h in range(N_Q_HEADS)], F32)
    bias = jnp.where((jnp.abs(rel) <= BLOCK)[None], -slopes[:, None, None] * dist[None], MASKED)
    bias = bias.reshape(N_KV_HEADS, Q_PER_KV, 3 * BLOCK, BLOCK).transpose(0, 2, 1, 3)
    bias = bias.reshape(N_KV_HEADS, 3 * BLOCK, Q_PER_KV * BLOCK)
    no_prev = (sj < BLOCK)[None]
    no_next = (sj >= 2 * BLOCK)[None]
    return jnp.stack([
        bias,
        jnp.where(no_prev, MASKED, bias),
        jnp.where(no_next, MASKED, bias),
    ])


def _lru_kernel(x_ref, wa_ref, wx_ref, ba_ref, bx_ref, lam_ref, h_ref, a_scr, u_scr, h_scr,
                carry_ref, *, lc, reverse):
    pitch = lc + SUBLANES
    slabs = RNN_WIDTH // LANES

    @pl.when(pl.program_id(1) == 0)
    def _():
        carry_ref[...] = jnp.zeros_like(carry_ref)

    lam = lam_ref[...]
    softplus_neg_lam = jnp.maximum(-lam, 0.0) + jnp.log1p(jnp.exp(-jnp.abs(lam)))
    half_rate = (-0.5 * LRU_C) * softplus_neg_lam
    for b in range(LRU_BATCH):
        xc = x_ref[b]
        xcb = xc.astype(BF16)
        th_r = jnp.tanh(jnp.dot(xcb, wa_ref[...], preferred_element_type=F32) + ba_ref[...])
        th_i = jnp.tanh(jnp.dot(xcb, wx_ref[...], preferred_element_type=F32) + bx_ref[...])
        log_a = th_r * half_rate + half_rate
        a = jnp.exp(log_a)
        z = jnp.tanh(log_a) * (-1.0 - a * a)
        root = z * lax.rsqrt(jnp.maximum(z, TINY))
        u = root * ((0.5 * th_i + 0.5) * xc)
        for j in range(slabs):
            a_scr[j, b * pitch:b * pitch + lc, :] = a[:, j * LANES:(j + 1) * LANES]
            u_scr[j, b * pitch:b * pitch + lc, :] = u[:, j * LANES:(j + 1) * LANES]

    def step(i, hs):
        t = lc - 1 - i if reverse else i
        out = []
        for j in range(slabs):
            rows = pl.ds(t, LRU_BATCH, stride=pitch)
            h = a_scr[j, rows, :] * hs[j] + u_scr[j, rows, :]
            h_scr[j, rows, :] = h
            out.append(h)
        return tuple(out)

    hs = lax.fori_loop(0, lc, step, tuple(carry_ref[j] for j in range(slabs)), unroll=LRU_UNROLL)
    for j in range(slabs):
        carry_ref[j] = hs[j]
        for b in range(LRU_BATCH):
            h_ref[b, :, j * LANES:(j + 1) * LANES] = h_scr[j, b * pitch:b * pitch + lc, :]


def _lru(xc, wa, wx, ba, bx, lam, *, reverse):
    b, l, _ = xc.shape
    lc = LRU_CHUNK
    assert b % LRU_BATCH == 0 and l % lc == 0
    nc = l // lc
    pos = (lambda c: nc - 1 - c) if reverse else (lambda c: c)
    const = lambda bi, c: (0, 0)
    vec = pl.BlockSpec((1, RNN_WIDTH), const)
    mat = pl.BlockSpec((RNN_WIDTH, RNN_WIDTH), const)
    chunk = pl.BlockSpec((LRU_BATCH, lc, RNN_WIDTH), lambda bi, c: (bi, pos(c), 0))
    slabs = RNN_WIDTH // LANES
    scratch = pltpu.VMEM((slabs, LRU_BATCH * (lc + SUBLANES), LANES), F32)
    return pl.pallas_call(
        functools.partial(_lru_kernel, lc=lc, reverse=reverse),
        grid=(b // LRU_BATCH, nc),
        in_specs=[chunk, mat, mat, vec, vec, vec],
        out_specs=chunk,
        out_shape=jax.ShapeDtypeStruct((b, l, RNN_WIDTH), F32),
        scratch_shapes=[scratch, scratch, scratch, pltpu.VMEM((slabs, LRU_BATCH, LANES), F32)],
        compiler_params=_params("parallel", "arbitrary"),
        name="rg_lru_bwd" if reverse else "rg_lru_fwd",
    )(xc, wa, wx, ba, bx, lam)


def _block_diag(w):
    eye = jnp.eye(RNN_BLOCKS, dtype=w.dtype)
    return jnp.einsum('hij,hk->hikj', w, eye).reshape(RNN_WIDTH, RNN_WIDTH)


def _mix_out_kernel(attn_ref, hf_ref, hb_ref, gate_ref, x_ref, wo_ref, rg_ref, g1_ref, b1_ref,
                    wr_ref, h_ref, hb16_ref, aff_ref):
    gt = gate_ref[...]
    gelu = 0.5 * gt * (1.0 + jnp.tanh(math.sqrt(2.0 / math.pi) * (gt + 0.044715 * (gt * gt * gt))))
    yr = (hf_ref[...] + hb_ref[...]) * gelu
    ms = jnp.mean(yr * yr, axis=-1, keepdims=True)
    yn = (yr * lax.rsqrt(ms + RMS_EPS) * rg_ref[...]).astype(BF16)
    mix = jnp.dot(attn_ref[...], wo_ref[:ATTN_WIDTH, :], preferred_element_type=F32)
    mix = mix + jnp.dot(yn, wo_ref[ATTN_WIDTH:, :], preferred_element_type=F32)
    z = ALPHA * x_ref[...] + mix
    mu = jnp.mean(z, axis=-1, keepdims=True)
    zc = z - mu
    var = jnp.mean(zc * zc, axis=-1, keepdims=True)
    h = zc * lax.rsqrt(var + LN_EPS) * g1_ref[...] + b1_ref[...]
    h_ref[...] = h
    hb = h.astype(BF16)
    hb16_ref[...] = hb
    logits = lax.dot_general(wr_ref[...], hb, (((1,), (1,)), ((), ())),
                             preferred_element_type=F32)
    mx = jnp.max(logits, axis=0, keepdims=True)
    e = jnp.exp(logits - mx)
    aff_ref[...] = e / jnp.sum(e, axis=0, keepdims=True)


def _mix_out(attn, hf, hb, gate, x2, wo, rnn_g, g1, b1, wr_t):
    t = x2.shape[0]
    row = lambda w: pl.BlockSpec((ROW_TILE, w), lambda i: (i, 0))
    const = lambda i: (0, 0)
    return pl.pallas_call(
        _mix_out_kernel,
        grid=(t // ROW_TILE,),
        in_specs=[
            row(ATTN_WIDTH), row(RNN_WIDTH), row(RNN_WIDTH), row(RNN_WIDTH), row(D_MODEL),
            pl.BlockSpec((D_MODEL, D_MODEL), const),
            pl.BlockSpec((1, RNN_WIDTH), const),
            pl.BlockSpec((1, D_MODEL), const),
            pl.BlockSpec((1, D_MODEL), const),
            pl.BlockSpec((N_EXPERTS, D_MODEL), const),
        ],
        out_specs=[row(D_MODEL), row(D_MODEL), pl.BlockSpec((N_EXPERTS, ROW_TILE), lambda i: (0, i))],
        out_shape=[
            jax.ShapeDtypeStruct((t, D_MODEL), F32),
            jax.ShapeDtypeStruct((t, D_MODEL), BF16),
            jax.ShapeDtypeStruct((N_EXPERTS, t), F32),
        ],
        compiler_params=_params("parallel"),
        name="mix_out_ln1_router",
    )(attn, hf, hb, gate, x2, wo, rnn_g, g1, b1, wr_t)


def _load_rows(ref, start, n, lead=()):
    return jnp.concatenate(
        [ref[lead + (pl.ds(start * SUBLANES + j, n, stride=SUBLANES), slice(None))]
         for j in range(D_MODEL // LANES)], axis=1)


def _store_rows(ref, start, rows, lead=()):
    n = rows.shape[0]
    for j in range(D_MODEL // LANES):
        ref[lead + (pl.ds(start * SUBLANES + j, n, stride=SUBLANES), slice(None))] = (
            rows[:, j * LANES:(j + 1) * LANES])


def _ffn_kernel(x_ref, wg_ref, wu_ref, wd_ref, o_ref):
    x = _load_rows(x_ref, 0, FFN_ROWS).astype(BF16)
    acc = jnp.zeros((FFN_ROWS, D_MODEL), F32)
    for f in range(EXPERT_FF // FFN_FCHUNK):
        cols = slice(f * FFN_FCHUNK, (f + 1) * FFN_FCHUNK)
        gt = jnp.dot(x, wg_ref[0, :, cols], preferred_element_type=F32)
        up = jnp.dot(x, wu_ref[0, :, cols], preferred_element_type=F32)
        hid = (gt * jax.nn.sigmoid(gt) * up).astype(BF16)
        acc = acc + jnp.dot(hid, wd_ref[0, cols, :], preferred_element_type=F32)
    _store_rows(o_ref, 0, acc)


def _ffn(xe, wg, wu, wd, cap):
    e = wg.shape[0]
    per_e = cap // FFN_ROWS
    slots = pl.BlockSpec((FFN_ROWS * SUBLANES, LANES), lambda ei, j: (ei * per_e + j, 0))
    wspec = lambda a, b: pl.BlockSpec((1, a, b), lambda ei, j: (ei, 0, 0))
    return pl.pallas_call(
        _ffn_kernel,
        grid=(e, per_e),
        in_specs=[slots, wspec(D_MODEL, EXPERT_FF), wspec(D_MODEL, EXPERT_FF),
                  wspec(EXPERT_FF, D_MODEL)],
        out_specs=slots,
        out_shape=jax.ShapeDtypeStruct(xe.shape, F32),
        compiler_params=_params("parallel", "arbitrary"),
        name="expert_ffn",
    )(xe, wg, wu, wd)


def _threshold_kernel(aff_ref, thr_ref, need_ref, *, cap):
    def count(mask):
        return jnp.sum(jnp.where(mask, 1.0, 0.0), axis=1, keepdims=True)

    def body(b, thr):
        cand = thr | jnp.left_shift(jnp.int32(1), 30 - b)
        bits = pltpu.bitcast(aff_ref[...], jnp.int32)
        return jnp.where(count(bits >= cand) >= cap, cand, thr)

    thr = lax.fori_loop(0, 31, body, jnp.zeros((N_EXPERTS, 1), jnp.int32))
    bits = pltpu.bitcast(aff_ref[...], jnp.int32)
    thr_ref[...] = thr
    need_ref[...] = cap - count(bits > thr)


def _positions_kernel(aff_ref, thr_ref, need_ref, pos_ref, tie_run_ref, sel_run_ref, *, block):
    @pl.when(pl.program_id(0) == 0)
    def _():
        tie_run_ref[...] = jnp.zeros_like(tie_run_ref)
        sel_run_ref[...] = jnp.zeros_like(sel_run_ref)

    thr = thr_ref[...]
    need = need_ref[...]
    r = lax.broadcasted_iota(jnp.int32, (LANES, LANES), 0)
    c = lax.broadcasted_iota(jnp.int32, (LANES, LANES), 1)
    before = jnp.where(r < c, 1.0, 0.0).astype(BF16)
    tie_run = tie_run_ref[...]
    sel_run = sel_run_ref[...]
    for s in range(block // LANES):
        lanes = slice(s * LANES, (s + 1) * LANES)
        bits = pltpu.bitcast(aff_ref[:, lanes], jnp.int32)
        tie = jnp.where(bits == thr, 1.0, 0.0)
        tie_before = jnp.dot(tie.astype(BF16), before, preferred_element_type=F32) + tie_run
        sel = (bits > thr) | ((bits == thr) & (tie_before < need))
        picked = jnp.where(sel, 1.0, 0.0)
        sel_before = jnp.dot(picked.astype(BF16), before, preferred_element_type=F32) + sel_run
        pos_ref[:, lanes] = jnp.where(sel, sel_before.astype(jnp.int32), -1)
        tie_run = tie_run + jnp.sum(tie, axis=1, keepdims=True)
        sel_run = sel_run + jnp.sum(picked, axis=1, keepdims=True)
    tie_run_ref[...] = tie_run
    sel_run_ref[...] = sel_run


def _route(aff_t, cap):
    e, t = aff_t.shape
    col = jax.ShapeDtypeStruct((e, 1), jnp.int32)
    thr, need = pl.pallas_call(
        functools.partial(_threshold_kernel, cap=cap),
        out_shape=[col, jax.ShapeDtypeStruct((e, 1), F32)],
        compiler_params=_params(),
        name="route_threshold",
    )(aff_t)
    block = min(POS_BLOCK, t)
    cspec = pl.BlockSpec((e, 1), lambda i: (0, 0))
    return pl.pallas_call(
        functools.partial(_positions_kernel, block=block),
        grid=(t // block,),
        in_specs=[pl.BlockSpec((e, block), lambda i: (0, i)), cspec, cspec],
        out_specs=pl.BlockSpec((e, block), lambda i: (0, i)),
        out_shape=jax.ShapeDtypeStruct((e, t), jnp.int32),
        scratch_shapes=[pltpu.VMEM((e, 1), F32), pltpu.VMEM((e, 1), F32)],
        compiler_params=_params("arbitrary"),
        name="route_positions",
    )(aff_t, thr, need)


def _tile_tables(pos, tile):
    e, t = pos.shape
    n = jnp.sum((pos >= 0).reshape(e, t // tile, tile), axis=-1, dtype=jnp.int32)
    s0 = jnp.cumsum(n, axis=1) - n
    off = jnp.cumsum(n, axis=0) - n
    shift = (off - s0).T.reshape(t // tile, e, 1)
    return n.reshape(-1), s0.reshape(-1), shift


def _tile_counts(n_ref, tile, ntile):
    counts = [n_ref[e * ntile + tile] for e in range(N_EXPERTS)]
    offs = [jnp.int32(0)]
    for c in counts:
        offs.append(offs[-1] + c)
    return counts, offs


def _one_hot_t(stagepos, kbase, values=None):
    rel = stagepos - kbase
    rel = jnp.where((rel >= 0) & (rel < STAGE_CHUNK), rel, -1).astype(F32).astype(BF16)
    k = lax.broadcasted_iota(jnp.int32, (STAGE_CHUNK, stagepos.shape[1]), 0).astype(F32).astype(BF16)
    one = jnp.ones((1, stagepos.shape[1]), BF16)
    pt = jnp.zeros(k.shape, BF16)
    for e in range(N_EXPERTS):
        hit = rel[e:e + 1, :] == k
        pt = jnp.where(hit, one if values is None else values[e:e + 1, :], pt)
    return pt


def _stage_rows(e, tile):
    chunks = max(STATIC_CHUNKS, -(-e * tile // STAGE_CHUNK))
    return chunks * STAGE_CHUNK


def _for_each_chunk(total, body, init):
    carry = init
    for c in range(STATIC_CHUNKS):
        carry = body(c * STAGE_CHUNK, carry)
    return lax.fori_loop(
        STATIC_CHUNKS, (total + STAGE_CHUNK - 1) // STAGE_CHUNK,
        lambda c, carry: body(pl.multiple_of(c * STAGE_CHUNK, STAGE_CHUNK), carry), carry)


def _rows_copy(src, dst, sem, src_row, dst_row, n):
    return pltpu.make_async_copy(
        src.at[pl.ds(pl.multiple_of(src_row * SUBLANES, SUBLANES), n * SUBLANES)],
        dst.at[pl.ds(pl.multiple_of(dst_row * SUBLANES, SUBLANES), n * SUBLANES)], sem)


def _dispatch_kernel(n_ref, s0_ref, pos_ref, shift_ref, h_ref, xe_hbm, stage_ref, sem, *, ntile, cap):
    i = pl.program_id(0)
    slot = i % 2

    def wait_writes(tile, slot):
        _, offs = _tile_counts(n_ref, tile, ntile)

        @pl.when(offs[-1] > 0)
        def _():
            _rows_copy(stage_ref.at[slot], xe_hbm, sem.at[slot], 0, 0, offs[-1]).wait()

    @pl.when(i >= 2)
    def _():
        wait_writes(i - 2, slot)

    counts, offs = _tile_counts(n_ref, i, ntile)
    pos = pos_ref[...]
    stagepos = jnp.where(pos >= 0, pos + shift_ref[0], -1)

    def chunk(kbase, carry):
        rows = jnp.dot(_one_hot_t(stagepos, kbase), h_ref[...], preferred_element_type=F32)
        _store_rows(stage_ref, kbase, rows, lead=(slot,))
        return carry

    _for_each_chunk(offs[-1], chunk, 0)

    for e in range(N_EXPERTS):
        @pl.when(counts[e] > 0)
        def _(e=e):
            _rows_copy(stage_ref.at[slot], xe_hbm, sem.at[slot], offs[e],
                       e * cap + s0_ref[e * ntile + i], counts[e]).start()

    @pl.when(i == ntile - 1)
    def _():
        wait_writes(i, slot)
        if ntile > 1:
            wait_writes(i - 1, 1 - slot)


def _dispatch(n, s0, pos, shift, h1b, cap):
    e, t = pos.shape
    tile = ROUTE_TILE
    ntile = t // tile
    return pl.pallas_call(
        functools.partial(_dispatch_kernel, ntile=ntile, cap=cap),
        grid_spec=pltpu.PrefetchScalarGridSpec(
            num_scalar_prefetch=2,
            grid=(ntile,),
            in_specs=[
                pl.BlockSpec((e, tile), lambda i, *_: (0, i)),
                pl.BlockSpec((1, e, 1), lambda i, *_: (i, 0, 0)),
                pl.BlockSpec((tile, D_MODEL), lambda i, *_: (i, 0)),
            ],
            out_specs=pl.BlockSpec(memory_space=pl.ANY),
            scratch_shapes=[pltpu.VMEM((2, _stage_rows(e, tile) * SUBLANES, LANES), F32),
                            pltpu.SemaphoreType.DMA((2,))],
        ),
        out_shape=jax.ShapeDtypeStruct((e * cap * SUBLANES, LANES), F32),
        compiler_params=_params("arbitrary"),
        name="route_dispatch",
    )(n, s0, pos, shift, h1b)


def _combine_kernel(n_ref, s0_ref, pos_ref, aff_ref, shift_ref, h_ref, g_ref, b_ref, ye_hbm, o_ref,
                    stage_ref, sem, *, ntile, cap):
    i = pl.program_id(0)
    slot = i % 2

    def fetch(tile, slot):
        counts, offs = _tile_counts(n_ref, tile, ntile)
        for e in range(N_EXPERTS):
            @pl.when(counts[e] > 0)
            def _(e=e):
                _rows_copy(ye_hbm, stage_ref.at[slot], sem.at[slot],
                           e * cap + s0_ref[e * ntile + tile], offs[e], counts[e]).start()

    @pl.when(i == 0)
    def _():
        stage_ref[...] = jnp.zeros_like(stage_ref)
        fetch(0, 0)

    @pl.when(i + 1 < ntile)
    def _():
        fetch(i + 1, 1 - slot)

    _, offs = _tile_counts(n_ref, i, ntile)

    @pl.when(offs[-1] > 0)
    def _():
        _rows_copy(ye_hbm, stage_ref.at[slot], sem.at[slot], 0, 0, offs[-1]).wait()

    pos = pos_ref[...]
    stagepos = jnp.where(pos >= 0, pos + shift_ref[0], -1)
    gates = aff_ref[...].astype(BF16)

    def chunk(kbase, acc):
        ye = _load_rows(stage_ref, kbase, STAGE_CHUNK, lead=(slot,)).astype(BF16)
        return acc + lax.dot_general(_one_hot_t(stagepos, kbase, gates), ye,
                                     (((0,), (0,)), ((), ())), preferred_element_type=F32)

    z = _for_each_chunk(offs[-1], chunk, ALPHA * h_ref[...])
    mu = jnp.mean(z, axis=-1, keepdims=True)
    zc = z - mu
    var = jnp.mean(zc * zc, axis=-1, keepdims=True)
    o_ref[...] = zc * lax.rsqrt(var + LN_EPS) * g_ref[...] + b_ref[...]


def _combine(n, s0, pos, aff_t, shift, h1, g2, b2, ye, cap):
    e, t = pos.shape
    tile = ROUTE_TILE
    ntile = t // tile
    lanes = pl.BlockSpec((e, tile), lambda i, *_: (0, i))
    vec = pl.BlockSpec((1, D_MODEL), lambda i, *_: (0, 0))
    row = pl.BlockSpec((tile, D_MODEL), lambda i, *_: (i, 0))
    return pl.pallas_call(
        functools.partial(_combine_kernel, ntile=ntile, cap=cap),
        grid_spec=pltpu.PrefetchScalarGridSpec(
            num_scalar_prefetch=2,
            grid=(ntile,),
            in_specs=[lanes, lanes, pl.BlockSpec((1, e, 1), lambda i, *_: (i, 0, 0)), row, vec, vec,
                      pl.BlockSpec(memory_space=pl.ANY)],
            out_specs=row,
            scratch_shapes=[pltpu.VMEM((2, _stage_rows(e, tile) * SUBLANES, LANES), F32),
                            pltpu.SemaphoreType.DMA((2,))],
        ),
        out_shape=jax.ShapeDtypeStruct((t, D_MODEL), F32),
        compiler_params=_params("arbitrary"),
        name="route_combine_ln2",
    )(n, s0, pos, aff_t, shift, h1, g2, b2, ye)


def _layer(x, p):
    b, l, _ = x.shape
    t = b * l
    cap = CAPACITY_FACTOR * t // N_EXPERTS
    x2 = x.reshape(t, D_MODEL)
    q, k, v, xc, gate = _in_proj(x2, p["w_in"], p["conv_w"], p["conv_b"], l)
    attn = _attention(q.reshape(b, l, -1), k.reshape(b, l, -1), v.reshape(b, l, -1),
                      p["bias"], p["sink_row"], p["attn_g_col"])
    xc3 = xc.reshape(b, l, RNN_WIDTH)
    hs = [_lru(xc3, p["wa"][d], p["wx"][d], p["ba"][d], p["bx"][d], p["lam"][d], reverse=bool(d))
          for d in range(2)]
    h1, h1b, aff_t = _mix_out(attn.reshape(t, -1), hs[0].reshape(t, -1), hs[1].reshape(t, -1), gate,
                              x2, p["w_out"], p["rnn_g"], p["ln1_g"], p["ln1_b"], p["wr_t"])
    pos = _route(aff_t, cap)
    n, s0, shift = _tile_tables(pos, ROUTE_TILE)
    xe = _dispatch(n, s0, pos, shift, h1b, cap)
    ye = _ffn(xe, p["wg"], p["wu"], p["wd"], cap)
    out = _combine(n, s0, pos, aff_t, shift, h1, p["ln2_g"], p["ln2_b"], ye, cap)
    return out.reshape(b, l, D_MODEL)


def _layer_params(li, w_in, attn_sink, attn_norm_g, rnn_norm_g, conv_w, conv_b, lru_w_a, lru_b_a,
                  lru_w_x, lru_b_x, lru_lambda, w_out, ln1_g, ln1_b, w_router, w_gate, w_up, w_down,
                  ln2_g, ln2_b):
    vec = lambda a: a[li].reshape(1, -1)
    per_dir = lambda f: [f(d) for d in range(2)]
    return dict(
        w_in=w_in[li].astype(BF16),
        bias=_alibi_bias(),
        sink_row=jnp.repeat(attn_sink[li].astype(F32), BLOCK).reshape(N_KV_HEADS, 1, Q_PER_KV * BLOCK),
        attn_g_col=jnp.broadcast_to(attn_norm_g[li].astype(F32)[:, None], (ATTN_WIDTH, BLOCK)),
        rnn_g=vec(rnn_norm_g),
        conv_w=conv_w[li], conv_b=vec(conv_b),
        wa=per_dir(lambda d: (0.5 * _block_diag(lru_w_a[li, d])).astype(BF16)),
        wx=per_dir(lambda d: (0.5 * _block_diag(lru_w_x[li, d])).astype(BF16)),
        ba=per_dir(lambda d: 0.5 * lru_b_a[li, d].reshape(1, -1)),
        bx=per_dir(lambda d: 0.5 * lru_b_x[li, d].reshape(1, -1)),
        lam=per_dir(lambda d: lru_lambda[li, d].reshape(1, -1)),
        w_out=w_out[li].astype(BF16),
        ln1_g=vec(ln1_g), ln1_b=vec(ln1_b),
        wr_t=w_router[li].T.astype(BF16),
        wg=w_gate[li].astype(BF16), wu=w_up[li].astype(BF16), wd=w_down[li].astype(BF16),
        ln2_g=vec(ln2_g), ln2_b=vec(ln2_b),
    )


def kernel(x_prompt, x_sample, w_in, attn_sink, attn_norm_g, rnn_norm_g, conv_w, conv_b, lru_w_a,
           lru_b_a, lru_w_x, lru_b_x, lru_lambda, w_out, ln1_g, ln1_b, w_router, w_gate, w_up,
           w_down, ln2_g, ln2_b):
    layers = [
        _layer_params(li, w_in, attn_sink, attn_norm_g, rnn_norm_g, conv_w, conv_b, lru_w_a, lru_b_a,
                      lru_w_x, lru_b_x, lru_lambda, w_out, ln1_g, ln1_b, w_router, w_gate, w_up,
                      w_down, ln2_g, ln2_b)
        for li in range(w_in.shape[0])]
    ys = []
    for x in (x_prompt, x_sample):
        for p in layers:
            x = _layer(x, p)
        ys.append(x)
    return tuple(ys)
```

```python
import functools
import math

import jax
import jax.numpy as jnp
from jax import lax
from jax.experimental import pallas as pl
from jax.experimental.pallas import tpu as pltpu

D_MODEL = 1024
HEAD_DIM = 64
N_Q_HEADS = 8
N_KV_HEADS = 2
Q_PER_KV = N_Q_HEADS // N_KV_HEADS
ATTN_WIDTH = N_Q_HEADS * HEAD_DIM
KV_WIDTH = N_KV_HEADS * HEAD_DIM
BLOCK = 128
RNN_WIDTH = 512
RNN_BLOCKS = 8
RNN_BLOCK_W = RNN_WIDTH // RNN_BLOCKS
CONV_WIDTH = 4
LRU_C = 8.0
N_EXPERTS = 16
EXPERT_FF = 2048
CAPACITY_FACTOR = 2
ALPHA = 2.0 ** 0.25
LN_EPS = 1e-5
RMS_EPS = 1e-6
MASKED = -1e30
TINY = 1e-37

SUBLANES = 8
LANES = 128
PACKED_SUB = 4
VMEM_LIMIT = 56 * 1024 * 1024

ROW_TILE = 512
HALO = 2 * SUBLANES
ATTN_QBLOCKS = 8
LRU_CHUNK = 256
LRU_BATCH = SUBLANES
LRU_UNROLL = 8
FFN_ROWS = 512
FFN_FCHUNK = 512
POS_BLOCK = 2048
ROUTE_TILE = 256
STAGE_CHUNK = 192
STATIC_CHUNKS = 3

BF16 = jnp.bfloat16
F32 = jnp.float32


def _params(*sem):
    return pltpu.CompilerParams(dimension_semantics=sem, vmem_limit_bytes=VMEM_LIMIT)


def _in_proj_kernel(x_ref, xp_ref, xn_ref, w_ref, cw_ref, cb_ref, q_ref, k_ref, v_ref, xc_ref, gate_ref,
                    *, tiles_per_seq):
    i = pl.program_id(0)
    xb = x_ref[...].astype(BF16)

    def proj(lo, hi):
        return jnp.dot(xb, w_ref[:, lo:hi], preferred_element_type=F32)

    q_lo, k_lo, v_lo = 0, ATTN_WIDTH, ATTN_WIDTH + KV_WIDTH
    r_lo = v_lo + KV_WIDTH
    g_lo = r_lo + RNN_WIDTH

    ext_b = jnp.concatenate([xp_ref[...].astype(BF16), xb, xn_ref[...].astype(BF16)], axis=0)
    ext = jnp.dot(ext_b, w_ref[:, r_lo:g_lo], preferred_element_type=F32)
    first = i % tiles_per_seq == 0
    last = i % tiles_per_seq == tiles_per_seq - 1
    n = ROW_TILE + 2 * HALO
    ext = jnp.concatenate([jnp.where(first, 0.0, ext[:HALO]), ext[HALO:HALO + ROW_TILE],
                           jnp.where(last, 0.0, ext[HALO + ROW_TILE:])], axis=0)
    mid = slice(HALO, HALO + ROW_TILE)
    w = cw_ref[...]
    xc = pltpu.roll(ext, 2, 0)[mid] * w[0:1] + pltpu.roll(ext, 1, 0)[mid] * w[1:2]
    xc = xc + ext[mid] * w[2:3]
    xc = xc + pltpu.roll(ext, n - 1, 0)[mid] * w[3:4]
    xc_ref[...] = xc + cb_ref[...]

    q_ref[...] = (proj(q_lo, k_lo) * (HEAD_DIM ** -0.5)).astype(BF16)
    k_ref[...] = proj(k_lo, v_lo).astype(BF16)
    v_ref[...] = proj(v_lo, r_lo).astype(BF16)
    gate_ref[...] = proj(g_lo, g_lo + RNN_WIDTH)


def _in_proj(x2, w_bf16, conv_w, conv_b, seq_len):
    t = x2.shape[0]
    in_w = w_bf16.shape[1]
    assert seq_len % ROW_TILE == 0 and ROW_TILE % HALO == 0
    per = ROW_TILE // HALO
    row = lambda w: pl.BlockSpec((ROW_TILE, w), lambda i: (i, 0))
    const = lambda i: (0, 0)
    return pl.pallas_call(
        functools.partial(_in_proj_kernel, tiles_per_seq=seq_len // ROW_TILE),
        grid=(t // ROW_TILE,),
        in_specs=[
            row(D_MODEL),
            pl.BlockSpec((HALO, D_MODEL), lambda i: (jnp.maximum(i * per - 1, 0), 0)),
            pl.BlockSpec((HALO, D_MODEL), lambda i: (jnp.minimum((i + 1) * per, t // HALO - 1), 0)),
            pl.BlockSpec((D_MODEL, in_w), const),
            pl.BlockSpec((CONV_WIDTH, RNN_WIDTH), const),
            pl.BlockSpec((1, RNN_WIDTH), const),
        ],
        out_specs=[row(ATTN_WIDTH), row(KV_WIDTH), row(KV_WIDTH), row(RNN_WIDTH), row(RNN_WIDTH)],
        out_shape=[
            jax.ShapeDtypeStruct((t, ATTN_WIDTH), BF16),
            jax.ShapeDtypeStruct((t, KV_WIDTH), BF16),
            jax.ShapeDtypeStruct((t, KV_WIDTH), BF16),
            jax.ShapeDtypeStruct((t, RNN_WIDTH), F32),
            jax.ShapeDtypeStruct((t, RNN_WIDTH), F32),
        ],
        compiler_params=_params("parallel"),
        name="in_proj",
    )(x2, x2, x2, w_bf16, conv_w, conv_b)


def _attn_kernel(q_ref, kp_ref, kc_ref, kn_ref, vp_ref, vc_ref, vn_ref, bias_first_ref, bias_mid_ref,
                 bias_last_ref, sink_ref, g_ref, o_ref):
    k = jnp.concatenate([kp_ref[0], kc_ref[0], kn_ref[0]], axis=0)
    v = jnp.concatenate([vp_ref[0], vc_ref[0], vn_ref[0]], axis=0)
    v_t = v.astype(F32).T
    pad_rows = 2 * SUBLANES
    ones_row = jnp.where(
        lax.broadcasted_iota(jnp.int32, (pad_rows, 3 * BLOCK), 0) == 0, 1.0, 0.0)
    for qb in range(ATTN_QBLOCKS):
        q = q_ref[0, qb * BLOCK:(qb + 1) * BLOCK, :]
        keys = slice(qb * BLOCK, (qb + 3) * BLOCK)
        bias_ref = (bias_first_ref if qb == 0 else
                    bias_last_ref if qb == ATTN_QBLOCKS - 1 else bias_mid_ref)
        heads = []
        for g in range(N_KV_HEADS):
            qs = jnp.concatenate(
                [q[:, (Q_PER_KV * g + j) * HEAD_DIM:(Q_PER_KV * g + j + 1) * HEAD_DIM]
                 for j in range(Q_PER_KV)], axis=0)
            kg = k[keys, g * HEAD_DIM:(g + 1) * HEAD_DIM]
            s = lax.dot_general(kg, qs, (((1,), (1,)), ((), ())), preferred_element_type=F32)
            s = s + bias_ref[0, g]
            sink = sink_ref[g]
            m = jnp.maximum(jnp.max(s, axis=0, keepdims=True), sink)
            p = jnp.exp(s - m).astype(BF16)
            lhs = jnp.concatenate([v_t[g * HEAD_DIM:(g + 1) * HEAD_DIM, keys], ones_row],
                                  axis=0).astype(BF16)
            o_aug = jnp.dot(lhs, p, preferred_element_type=F32)
            denom = o_aug[HEAD_DIM:HEAD_DIM + 1] + jnp.exp(sink - m)
            o = o_aug[:HEAD_DIM] * (1.0 / denom)
            heads += [o[:, j * BLOCK:(j + 1) * BLOCK] for j in range(Q_PER_KV)]
        y_t = jnp.concatenate(heads, axis=0)
        ms = jnp.mean(y_t * y_t, axis=0, keepdims=True)
        o_ref[0, qb * BLOCK:(qb + 1) * BLOCK, :] = (
            y_t * lax.rsqrt(ms + RMS_EPS) * g_ref[...]).T.astype(BF16)


def _attention(q, k, v, bias, sink_row, attn_g_col):
    b, l, _ = q.shape
    span = ATTN_QBLOCKS * BLOCK
    assert ATTN_QBLOCKS >= 2 and l % span == 0
    ns = l // span
    nb = l // BLOCK
    edge = lambda f: pl.BlockSpec((1, BLOCK, KV_WIDTH), f)
    prev = lambda bi, i: (bi, jnp.maximum(i * ATTN_QBLOCKS - 1, 0), 0)
    cur = lambda bi, i: (bi, i, 0)
    nxt = lambda bi, i: (bi, jnp.minimum((i + 1) * ATTN_QBLOCKS, nb - 1), 0)
    mid = pl.BlockSpec((1, span, KV_WIDTH), cur)
    bias_spec = lambda f: pl.BlockSpec((1,) + bias.shape[1:], f)
    first = lambda bi, i: ((i == 0).astype(jnp.int32), 0, 0, 0)
    last = lambda bi, i: (2 * (i == ns - 1).astype(jnp.int32), 0, 0, 0)
    return pl.pallas_call(
        _attn_kernel,
        grid=(b, ns),
        in_specs=[
            pl.BlockSpec((1, span, ATTN_WIDTH), cur),
            edge(prev), mid, edge(nxt),
            edge(prev), mid, edge(nxt),
            bias_spec(first), bias_spec(lambda bi, i: (0, 0, 0, 0)), bias_spec(last),
            pl.BlockSpec(sink_row.shape, lambda bi, i: (0, 0, 0)),
            pl.BlockSpec(attn_g_col.shape, lambda bi, i: (0, 0)),
        ],
        out_specs=pl.BlockSpec((1, span, ATTN_WIDTH), cur),
        out_shape=jax.ShapeDtypeStruct((b, l, ATTN_WIDTH), BF16),
        compiler_params=_params("parallel", "parallel"),
        name="banded_attention",
    )(q, k, k, k, v, v, v, bias, bias, bias, sink_row, attn_g_col)


def _alibi_bias():
    qi = jnp.arange(BLOCK)[None, :]
    sj = jnp.arange(3 * BLOCK)[:, None]
    rel = sj - BLOCK - qi
    dist = jnp.abs(rel).astype(F32)
    slopes = jnp.asarray([2.0 ** (-8.0 * (h + 1) / N_Q_HEADS) for h in range(N_Q_HEADS)], F32)
    bias = jnp.where((jnp.abs(rel) <= BLOCK)[None], -slopes[:, None, None] * dist[None], MASKED)
    bias = bias.reshape(N_KV_HEADS, Q_PER_KV, 3 * BLOCK, BLOCK).transpose(0, 2, 1, 3)
    bias = bias.reshape(N_KV_HEADS, 3 * BLOCK, Q_PER_KV * BLOCK)
    no_prev = (sj < BLOCK)[None]
    no_next = (sj >= 2 * BLOCK)[None]
    return jnp.stack([
        bias,
        jnp.where(no_prev, MASKED, bias),
        jnp.where(no_next, MASKED, bias),
    ])


def _lru_kernel(x_ref, wa_ref, wx_ref, ba_ref, bx_ref, lam_ref, h_ref, a_scr, u_scr, h_scr,
                carry_ref, *, lc, reverse):
    pitch = lc + SUBLANES
    slabs = RNN_WIDTH // LANES

    @pl.when(pl.program_id(1) == 0)
    def _():
        carry_ref[...] = jnp.zeros_like(carry_ref)

    lam = lam_ref[...]
    softplus_neg_lam = jnp.maximum(-lam, 0.0) + jnp.log1p(jnp.exp(-jnp.abs(lam)))
    half_rate = (-0.5 * LRU_C) * softplus_neg_lam
    for b in range(LRU_BATCH):
        xc = x_ref[b]
        xcb = xc.astype(BF16)
        th_r = jnp.tanh(jnp.dot(xcb, wa_ref[...], preferred_element_type=F32) + ba_ref[...])
        th_i = jnp.tanh(jnp.dot(xcb, wx_ref[...], preferred_element_type=F32) + bx_ref[...])
        log_a = th_r * half_rate + half_rate
        a = jnp.exp(log_a)
        z = jnp.tanh(log_a) * (-1.0 - a * a)
        root = z * lax.rsqrt(jnp.maximum(z, TINY))
        u = root * ((0.5 * th_i + 0.5) * xc)
        for j in range(slabs):
            a_scr[j, b * pitch:b * pitch + lc, :] = a[:, j * LANES:(j + 1) * LANES]
            u_scr[j, b * pitch:b * pitch + lc, :] = u[:, j * LANES:(j + 1) * LANES]

    def step(i, hs):
        t = lc - 1 - i if reverse else i
        out = []
        for j in range(slabs):
            rows = pl.ds(t, LRU_BATCH, stride=pitch)
            h = a_scr[j, rows, :] * hs[j] + u_scr[j, rows, :]
            h_scr[j, rows, :] = h
            out.append(h)
        return tuple(out)

    hs = lax.fori_loop(0, lc, step, tuple(carry_ref[j] for j in range(slabs)), unroll=LRU_UNROLL)
    for j in range(slabs):
        carry_ref[j] = hs[j]
        for b in range(LRU_BATCH):
            h_ref[b, :, j * LANES:(j + 1) * LANES] = h_scr[j, b * pitch:b * pitch + lc, :]


def _lru(xc, wa, wx, ba, bx, lam, *, reverse):
    b, l, _ = xc.shape
    lc = LRU_CHUNK
    assert b % LRU_BATCH == 0 and l % lc == 0
    nc = l // lc
    pos = (lambda c: nc - 1 - c) if reverse else (lambda c: c)
    const = lambda bi, c: (0, 0)
    vec = pl.BlockSpec((1, RNN_WIDTH), const)
    mat = pl.BlockSpec((RNN_WIDTH, RNN_WIDTH), const)
    chunk = pl.BlockSpec((LRU_BATCH, lc, RNN_WIDTH), lambda bi, c: (bi, pos(c), 0))
    slabs = RNN_WIDTH // LANES
    scratch = pltpu.VMEM((slabs, LRU_BATCH * (lc + SUBLANES), LANES), F32)
    return pl.pallas_call(
        functools.partial(_lru_kernel, lc=lc, reverse=reverse),
        grid=(b // LRU_BATCH, nc),
        in_specs=[chunk, mat, mat, vec, vec, vec],
        out_specs=chunk,
        out_shape=jax.ShapeDtypeStruct((b, l, RNN_WIDTH), F32),
        scratch_shapes=[scratch, scratch, scratch, pltpu.VMEM((slabs, LRU_BATCH, LANES), F32)],
        compiler_params=_params("parallel", "arbitrary"),
        name="rg_lru_bwd" if reverse else "rg_lru_fwd",
    )(xc, wa, wx, ba, bx, lam)


def _block_diag(w):
    eye = jnp.eye(RNN_BLOCKS, dtype=w.dtype)
    return jnp.einsum('hij,hk->hikj', w, eye).reshape(RNN_WIDTH, RNN_WIDTH)


def _mix_out_kernel(attn_ref, hf_ref, hb_ref, gate_ref, x_ref, wo_ref, rg_ref, g1_ref, b1_ref,
                    wr_ref, h_ref, hb16_ref, aff_ref):
    gt = gate_ref[...]
    gelu = 0.5 * gt * (1.0 + jnp.tanh(math.sqrt(2.0 / math.pi) * (gt + 0.044715 * (gt * gt * gt))))
    yr = (hf_ref[...] + hb_ref[...]) * gelu
    ms = jnp.mean(yr * yr, axis=-1, keepdims=True)
    yn = (yr * lax.rsqrt(ms + RMS_EPS) * rg_ref[...]).astype(BF16)
    mix = jnp.dot(attn_ref[...], wo_ref[:ATTN_WIDTH, :], preferred_element_type=F32)
    mix = mix + jnp.dot(yn, wo_ref[ATTN_WIDTH:, :], preferred_element_type=F32)
    z = ALPHA * x_ref[...] + mix
    mu = jnp.mean(z, axis=-1, keepdims=True)
    zc = z - mu
    var = jnp.mean(zc * zc, axis=-1, keepdims=True)
    h = zc * lax.rsqrt(var + LN_EPS) * g1_ref[...] + b1_ref[...]
    h_ref[...] = h
    hb = h.astype(BF16)
    hb16_ref[...] = hb
    logits = lax.dot_general(wr_ref[...], hb, (((1,), (1,)), ((), ())),
                             preferred_element_type=F32)
    mx = jnp.max(logits, axis=0, keepdims=True)
    e = jnp.exp(logits - mx)
    aff_ref[...] = e / jnp.sum(e, axis=0, keepdims=True)


def _mix_out(attn, hf, hb, gate, x2, wo, rnn_g, g1, b1, wr_t):
    t = x2.shape[0]
    row = lambda w: pl.BlockSpec((ROW_TILE, w), lambda i: (i, 0))
    const = lambda i: (0, 0)
    return pl.pallas_call(
        _mix_out_kernel,
        grid=(t // ROW_TILE,),
        in_specs=[
            row(ATTN_WIDTH), row(RNN_WIDTH), row(RNN_WIDTH), row(RNN_WIDTH), row(D_MODEL),
            pl.BlockSpec((D_MODEL, D_MODEL), const),
            pl.BlockSpec((1, RNN_WIDTH), const),
            pl.BlockSpec((1, D_MODEL), const),
            pl.BlockSpec((1, D_MODEL), const),
            pl.BlockSpec((N_EXPERTS, D_MODEL), const),
        ],
        out_specs=[row(D_MODEL), row(D_MODEL), pl.BlockSpec((N_EXPERTS, ROW_TILE), lambda i: (0, i))],
        out_shape=[
            jax.ShapeDtypeStruct((t, D_MODEL), F32),
            jax.ShapeDtypeStruct((t, D_MODEL), BF16),
            jax.ShapeDtypeStruct((N_EXPERTS, t), F32),
        ],
        compiler_params=_params("parallel"),
        name="mix_out_ln1_router",
    )(attn, hf, hb, gate, x2, wo, rnn_g, g1, b1, wr_t)


def _load_rows(ref, start, n, lead=()):
    return jnp.concatenate(
        [ref[lead + (pl.ds(start * SUBLANES + j, n, stride=SUBLANES), slice(None))]
         for j in range(D_MODEL // LANES)], axis=1)


def _store_rows(ref, start, rows, lead=()):
    n = rows.shape[0]
    for j in range(D_MODEL // LANES):
        ref[lead + (pl.ds(start * SUBLANES + j, n, stride=SUBLANES), slice(None))] = (
            rows[:, j * LANES:(j + 1) * LANES])


def _pack_rows(rows):
    half = D_MODEL // 2
    lo = pltpu.bitcast(rows[:, :half].astype(BF16).astype(F32), jnp.uint32)
    hi = pltpu.bitcast(rows[:, half:].astype(BF16).astype(F32), jnp.uint32)
    return hi | (lo >> 16)


def _unpack_rows(words):
    lo = pltpu.bitcast(words << 16, F32)
    hi = pltpu.bitcast(words & jnp.uint32(0xFFFF0000), F32)
    return jnp.concatenate([lo, hi], axis=1).astype(BF16)


def _ffn_kernel(x_ref, wg_ref, wu_ref, wd_ref, o_ref):
    x = _load_rows(x_ref, 0, FFN_ROWS).astype(BF16)
    acc = jnp.zeros((FFN_ROWS, D_MODEL), F32)
    for f in range(EXPERT_FF // FFN_FCHUNK):
        cols = slice(f * FFN_FCHUNK, (f + 1) * FFN_FCHUNK)
        gt = jnp.dot(x, wg_ref[0, :, cols], preferred_element_type=F32)
        up = jnp.dot(x, wu_ref[0, :, cols], preferred_element_type=F32)
        hid = (gt * jax.nn.sigmoid(gt) * up).astype(BF16)
        acc = acc + jnp.dot(hid, wd_ref[0, cols, :], preferred_element_type=F32)
    words = _pack_rows(acc)
    for j in range(PACKED_SUB):
        o_ref[pl.ds(j, FFN_ROWS, stride=PACKED_SUB), :] = words[:, j * LANES:(j + 1) * LANES]


def _ffn(xe, wg, wu, wd, cap):
    e = wg.shape[0]
    per_e = cap // FFN_ROWS
    slots = pl.BlockSpec((FFN_ROWS * SUBLANES, LANES), lambda ei, j: (ei * per_e + j, 0))
    wspec = lambda a, b: pl.BlockSpec((1, a, b), lambda ei, j: (ei, 0, 0))
    return pl.pallas_call(
        _ffn_kernel,
        grid=(e, per_e),
        in_specs=[slots, wspec(D_MODEL, EXPERT_FF), wspec(D_MODEL, EXPERT_FF),
                  wspec(EXPERT_FF, D_MODEL)],
        out_specs=pl.BlockSpec((FFN_ROWS * PACKED_SUB, LANES), lambda ei, j: (ei * per_e + j, 0)),
        out_shape=jax.ShapeDtypeStruct((e * cap * PACKED_SUB, LANES), jnp.uint32),
        compiler_params=_params("parallel", "arbitrary"),
        name="expert_ffn",
    )(xe, wg, wu, wd)


def _threshold_kernel(aff_ref, thr_ref, need_ref, *, cap):
    def count(mask):
        return jnp.sum(jnp.where(mask, 1.0, 0.0), axis=1, keepdims=True)

    def body(b, thr):
        cand = thr | jnp.left_shift(jnp.int32(1), 30 - b)
        bits = pltpu.bitcast(aff_ref[...], jnp.int32)
        return jnp.where(count(bits >= cand) >= cap, cand, thr)

    thr = lax.fori_loop(0, 31, body, jnp.zeros((N_EXPERTS, 1), jnp.int32))
    bits = pltpu.bitcast(aff_ref[...], jnp.int32)
    thr_ref[...] = thr
    need_ref[...] = cap - count(bits > thr)


def _positions_kernel(aff_ref, thr_ref, need_ref, pos_ref, tie_run_ref, sel_run_ref, *, block):
    @pl.when(pl.program_id(0) == 0)
    def _():
        tie_run_ref[...] = jnp.zeros_like(tie_run_ref)
        sel_run_ref[...] = jnp.zeros_like(sel_run_ref)

    thr = thr_ref[...]
    need = need_ref[...]
    r = lax.broadcasted_iota(jnp.int32, (LANES, LANES), 0)
    c = lax.broadcasted_iota(jnp.int32, (LANES, LANES), 1)
    before = jnp.where(r < c, 1.0, 0.0).astype(BF16)
    tie_run = tie_run_ref[...]
    sel_run = sel_run_ref[...]
    for s in range(block // LANES):
        lanes = slice(s * LANES, (s + 1) * LANES)
        bits = pltpu.bitcast(aff_ref[:, lanes], jnp.int32)
        tie = jnp.where(bits == thr, 1.0, 0.0)
        tie_before = jnp.dot(tie.astype(BF16), before, preferred_element_type=F32) + tie_run
        sel = (bits > thr) | ((bits == thr) & (tie_before < need))
        picked = jnp.where(sel, 1.0, 0.0)
        sel_before = jnp.dot(picked.astype(BF16), before, preferred_element_type=F32) + sel_run
        pos_ref[:, lanes] = jnp.where(sel, sel_before.astype(jnp.int32), -1)
        tie_run = tie_run + jnp.sum(tie, axis=1, keepdims=True)
        sel_run = sel_run + jnp.sum(picked, axis=1, keepdims=True)
    tie_run_ref[...] = tie_run
    sel_run_ref[...] = sel_run


def _route(aff_t, cap):
    e, t = aff_t.shape
    col = jax.ShapeDtypeStruct((e, 1), jnp.int32)
    thr, need = pl.pallas_call(
        functools.partial(_threshold_kernel, cap=cap),
        out_shape=[col, jax.ShapeDtypeStruct((e, 1), F32)],
        compiler_params=_params(),
        name="route_threshold",
    )(aff_t)
    block = min(POS_BLOCK, t)
    cspec = pl.BlockSpec((e, 1), lambda i: (0, 0))
    return pl.pallas_call(
        functools.partial(_positions_kernel, block=block),
        grid=(t // block,),
        in_specs=[pl.BlockSpec((e, block), lambda i: (0, i)), cspec, cspec],
        out_specs=pl.BlockSpec((e, block), lambda i: (0, i)),
        out_shape=jax.ShapeDtypeStruct((e, t), jnp.int32),
        scratch_shapes=[pltpu.VMEM((e, 1), F32), pltpu.VMEM((e, 1), F32)],
        compiler_params=_params("arbitrary"),
        name="route_positions",
    )(aff_t, thr, need)


def _tile_tables(pos, tile):
    e, t = pos.shape
    n = jnp.sum((pos >= 0).reshape(e, t // tile, tile), axis=-1, dtype=jnp.int32)
    s0 = jnp.cumsum(n, axis=1) - n
    off = jnp.cumsum(n, axis=0) - n
    shift = (off - s0).T.reshape(t // tile, e, 1)
    s0a = s0 - s0 % 2
    m = jnp.where(n > 0, (s0 + n + 1) // 2 * 2 - s0a, 0)
    offa = jnp.cumsum(m, axis=0) - m
    shifta = (offa - s0a).T.reshape(t // tile, e, 1)
    return (n.reshape(-1), s0.reshape(-1), shift), (m.reshape(-1), s0a.reshape(-1), shifta)


def _tile_counts(n_ref, tile, ntile):
    counts = [n_ref[e * ntile + tile] for e in range(N_EXPERTS)]
    offs = [jnp.int32(0)]
    for c in counts:
        offs.append(offs[-1] + c)
    return counts, offs


def _one_hot_t(stagepos, kbase, values=None):
    rel = stagepos - kbase
    rel = jnp.where((rel >= 0) & (rel < STAGE_CHUNK), rel, -1).astype(F32).astype(BF16)
    k = lax.broadcasted_iota(jnp.int32, (STAGE_CHUNK, stagepos.shape[1]), 0).astype(F32).astype(BF16)
    one = jnp.ones((1, stagepos.shape[1]), BF16)
    pt = jnp.zeros(k.shape, BF16)
    for e in range(N_EXPERTS):
        hit = rel[e:e + 1, :] == k
        pt = jnp.where(hit, one if values is None else values[e:e + 1, :], pt)
    return pt


def _stage_rows(e, tile, extra=0):
    chunks = max(STATIC_CHUNKS, -(-(e * tile + extra) // STAGE_CHUNK))
    return chunks * STAGE_CHUNK


def _for_each_chunk(total, body, init):
    carry = init
    for c in range(STATIC_CHUNKS):
        carry = body(c * STAGE_CHUNK, carry)
    return lax.fori_loop(
        STATIC_CHUNKS, (total + STAGE_CHUNK - 1) // STAGE_CHUNK,
        lambda c, carry: body(pl.multiple_of(c * STAGE_CHUNK, STAGE_CHUNK), carry), carry)


def _rows_copy(src, dst, sem, src_row, dst_row, n):
    return pltpu.make_async_copy(
        src.at[pl.ds(pl.multiple_of(src_row * SUBLANES, SUBLANES), n * SUBLANES)],
        dst.at[pl.ds(pl.multiple_of(dst_row * SUBLANES, SUBLANES), n * SUBLANES)], sem)


def _pairs_copy(src, dst, sem, src_row, dst_row, n):
    return pltpu.make_async_copy(
        src.at[pl.ds(pl.multiple_of(src_row * PACKED_SUB, SUBLANES), n // 2 * SUBLANES)],
        dst.at[pl.ds(pl.multiple_of(dst_row * PACKED_SUB, SUBLANES), n // 2 * SUBLANES)], sem)


def _dispatch_kernel(n_ref, s0_ref, pos_ref, shift_ref, h_ref, xe_hbm, stage_ref, sem, *, ntile, cap):
    i = pl.program_id(0)
    slot = i % 2

    def wait_writes(tile, slot):
        _, offs = _tile_counts(n_ref, tile, ntile)

        @pl.when(offs[-1] > 0)
        def _():
            _rows_copy(stage_ref.at[slot], xe_hbm, sem.at[slot], 0, 0, offs[-1]).wait()

    @pl.when(i >= 2)
    def _():
        wait_writes(i - 2, slot)

    counts, offs = _tile_counts(n_ref, i, ntile)
    pos = pos_ref[...]
    stagepos = jnp.where(pos >= 0, pos + shift_ref[0], -1)

    def chunk(kbase, carry):
        rows = jnp.dot(_one_hot_t(stagepos, kbase), h_ref[...], preferred_element_type=F32)
        _store_rows(stage_ref, kbase, rows, lead=(slot,))
        return carry

    _for_each_chunk(offs[-1], chunk, 0)

    for e in range(N_EXPERTS):
        @pl.when(counts[e] > 0)
        def _(e=e):
            _rows_copy(stage_ref.at[slot], xe_hbm, sem.at[slot], offs[e],
                       e * cap + s0_ref[e * ntile + i], counts[e]).start()

    @pl.when(i == ntile - 1)
    def _():
        wait_writes(i, slot)
        if ntile > 1:
            wait_writes(i - 1, 1 - slot)


def _dispatch(n, s0, pos, shift, h1b, cap):
    e, t = pos.shape
    tile = ROUTE_TILE
    ntile = t // tile
    return pl.pallas_call(
        functools.partial(_dispatch_kernel, ntile=ntile, cap=cap),
        grid_spec=pltpu.PrefetchScalarGridSpec(
            num_scalar_prefetch=2,
            grid=(ntile,),
            in_specs=[
                pl.BlockSpec((e, tile), lambda i, *_: (0, i)),
                pl.BlockSpec((1, e, 1), lambda i, *_: (i, 0, 0)),
                pl.BlockSpec((tile, D_MODEL), lambda i, *_: (i, 0)),
            ],
            out_specs=pl.BlockSpec(memory_space=pl.ANY),
            scratch_shapes=[pltpu.VMEM((2, _stage_rows(e, tile) * SUBLANES, LANES), F32),
                            pltpu.SemaphoreType.DMA((2,))],
        ),
        out_shape=jax.ShapeDtypeStruct((e * cap * SUBLANES, LANES), F32),
        compiler_params=_params("arbitrary"),
        name="route_dispatch",
    )(n, s0, pos, shift, h1b)


def _combine_kernel(n_ref, s0_ref, pos_ref, aff_ref, shift_ref, h_ref, g_ref, b_ref, ye_hbm, o_ref,
                    stage_ref, sem, *, ntile, cap):
    i = pl.program_id(0)
    slot = i % 2

    def fetch(tile, slot):
        counts, offs = _tile_counts(n_ref, tile, ntile)
        for e in range(N_EXPERTS):
            @pl.when(counts[e] > 0)
            def _(e=e):
                _pairs_copy(ye_hbm, stage_ref.at[slot], sem.at[slot],
                            e * cap + s0_ref[e * ntile + tile], offs[e], counts[e]).start()

    @pl.when(i == 0)
    def _():
        stage_ref[...] = jnp.zeros_like(stage_ref)
        fetch(0, 0)

    @pl.when(i + 1 < ntile)
    def _():
        fetch(i + 1, 1 - slot)

    _, offs = _tile_counts(n_ref, i, ntile)

    @pl.when(offs[-1] > 0)
    def _():
        _pairs_copy(ye_hbm, stage_ref.at[slot], sem.at[slot], 0, 0, offs[-1]).wait()

    pos = pos_ref[...]
    stagepos = jnp.where(pos >= 0, pos + shift_ref[0], -1)
    gates = aff_ref[...].astype(BF16)

    def chunk(kbase, acc):
        ye = _unpack_rows(jnp.concatenate(
            [stage_ref[slot, pl.ds(kbase * PACKED_SUB + j, STAGE_CHUNK, stride=PACKED_SUB), :]
             for j in range(PACKED_SUB)], axis=1))
        return acc + lax.dot_general(_one_hot_t(stagepos, kbase, gates), ye,
                                     (((0,), (0,)), ((), ())), preferred_element_type=F32)

    z = _for_each_chunk(offs[-1], chunk, ALPHA * h_ref[...])
    mu = jnp.mean(z, axis=-1, keepdims=True)
    zc = z - mu
    var = jnp.mean(zc * zc, axis=-1, keepdims=True)
    o_ref[...] = zc * lax.rsqrt(var + LN_EPS) * g_ref[...] + b_ref[...]


def _combine(n, s0, pos, aff_t, shift, h1, g2, b2, ye, cap):
    e, t = pos.shape
    tile = ROUTE_TILE
    ntile = t // tile
    lanes = pl.BlockSpec((e, tile), lambda i, *_: (0, i))
    vec = pl.BlockSpec((1, D_MODEL), lambda i, *_: (0, 0))
    row = pl.BlockSpec((tile, D_MODEL), lambda i, *_: (i, 0))
    return pl.pallas_call(
        functools.partial(_combine_kernel, ntile=ntile, cap=cap),
        grid_spec=pltpu.PrefetchScalarGridSpec(
            num_scalar_prefetch=2,
            grid=(ntile,),
            in_specs=[lanes, lanes, pl.BlockSpec((1, e, 1), lambda i, *_: (i, 0, 0)), row, vec, vec,
                      pl.BlockSpec(memory_space=pl.ANY)],
            out_specs=row,
            scratch_shapes=[pltpu.VMEM((2, _stage_rows(e, tile, 2 * e) * PACKED_SUB, LANES),
                                       jnp.uint32),
                            pltpu.SemaphoreType.DMA((2,))],
        ),
        out_shape=jax.ShapeDtypeStruct((t, D_MODEL), F32),
        compiler_params=_params("arbitrary"),
        name="route_combine_ln2",
    )(n, s0, pos, aff_t, shift, h1, g2, b2, ye)


def _layer(x, p):
    b, l, _ = x.shape
    t = b * l
    cap = CAPACITY_FACTOR * t // N_EXPERTS
    x2 = x.reshape(t, D_MODEL)
    q, k, v, xc, gate = _in_proj(x2, p["w_in"], p["conv_w"], p["conv_b"], l)
    attn = _attention(q.reshape(b, l, -1), k.reshape(b, l, -1), v.reshape(b, l, -1),
                      p["bias"], p["sink_row"], p["attn_g_col"])
    xc3 = xc.reshape(b, l, RNN_WIDTH)
    hs = [_lru(xc3, p["wa"][d], p["wx"][d], p["ba"][d], p["bx"][d], p["lam"][d], reverse=bool(d))
          for d in range(2)]
    h1, h1b, aff_t = _mix_out(attn.reshape(t, -1), hs[0].reshape(t, -1), hs[1].reshape(t, -1), gate,
                              x2, p["w_out"], p["rnn_g"], p["ln1_g"], p["ln1_b"], p["wr_t"])
    pos = _route(aff_t, cap)
    (n, s0, shift), (m, s0a, shifta) = _tile_tables(pos, ROUTE_TILE)
    xe = _dispatch(n, s0, pos, shift, h1b, cap)
    ye = _ffn(xe, p["wg"], p["wu"], p["wd"], cap)
    out = _combine(m, s0a, pos, aff_t, shifta, h1, p["ln2_g"], p["ln2_b"], ye, cap)
    return out.reshape(b, l, D_MODEL)


def _layer_params(li, w_in, attn_sink, attn_norm_g, rnn_norm_g, conv_w, conv_b, lru_w_a, lru_b_a,
                  lru_w_x, lru_b_x, lru_lambda, w_out, ln1_g, ln1_b, w_router, w_gate, w_up, w_down,
                  ln2_g, ln2_b):
    vec = lambda a: a[li].reshape(1, -1)
    per_dir = lambda f: [f(d) for d in range(2)]
    return dict(
        w_in=w_in[li].astype(BF16),
        bias=_alibi_bias(),
        sink_row=jnp.repeat(attn_sink[li].astype(F32), BLOCK).reshape(N_KV_HEADS, 1, Q_PER_KV * BLOCK),
        attn_g_col=jnp.broadcast_to(attn_norm_g[li].astype(F32)[:, None], (ATTN_WIDTH, BLOCK)),
        rnn_g=vec(rnn_norm_g),
        conv_w=conv_w[li], conv_b=vec(conv_b),
        wa=per_dir(lambda d: (0.5 * _block_diag(lru_w_a[li, d])).astype(BF16)),
        wx=per_dir(lambda d: (0.5 * _block_diag(lru_w_x[li, d])).astype(BF16)),
        ba=per_dir(lambda d: 0.5 * lru_b_a[li, d].reshape(1, -1)),
        bx=per_dir(lambda d: 0.5 * lru_b_x[li, d].reshape(1, -1)),
        lam=per_dir(lambda d: lru_lambda[li, d].reshape(1, -1)),
        w_out=w_out[li].astype(BF16),
        ln1_g=vec(ln1_g), ln1_b=vec(ln1_b),
        wr_t=w_router[li].T.astype(BF16),
        wg=w_gate[li].astype(BF16), wu=w_up[li].astype(BF16), wd=w_down[li].astype(BF16),
        ln2_g=vec(ln2_g), ln2_b=vec(ln2_b),
    )


def kernel(x_prompt, x_sample, w_in, attn_sink, attn_norm_g, rnn_norm_g, conv_w, conv_b, lru_w_a,
           lru_b_a, lru_w_x, lru_b_x, lru_lambda, w_out, ln1_g, ln1_b, w_router, w_gate, w_up,
           w_down, ln2_g, ln2_b):
    layers = [
        _layer_params(li, w_in, attn_sink, attn_norm_g, rnn_norm_g, conv_w, conv_b, lru_w_a, lru_b_a,
                      lru_w_x, lru_b_x, lru_lambda, w_out, ln1_g, ln1_b, w_router, w_gate, w_up,
                      w_down, ln2_g, ln2_b)
        for li in range(w_in.shape[0])]
    ys = []
    for x in (x_prompt, x_sample):
        for p in layers:
            x = _layer(x, p)
        ys.append(x)
    return tuple(ys)
```

```python
import functools
import math

import jax
import jax.numpy as jnp
from jax import lax
from jax.experimental import pallas as pl
from jax.experimental.pallas import tpu as pltpu

D_MODEL = 1024
HEAD_DIM = 64
N_Q_HEADS = 8
N_KV_HEADS = 2
Q_PER_KV = N_Q_HEADS // N_KV_HEADS
ATTN_WIDTH = N_Q_HEADS * HEAD_DIM
KV_WIDTH = N_KV_HEADS * HEAD_DIM
BLOCK = 128
RNN_WIDTH = 512
RNN_BLOCKS = 8
RNN_BLOCK_W = RNN_WIDTH // RNN_BLOCKS
CONV_WIDTH = 4
LRU_C = 8.0
N_EXPERTS = 16
EXPERT_FF = 2048
CAPACITY_FACTOR = 2
ALPHA = 2.0 ** 0.25
LN_EPS = 1e-5
RMS_EPS = 1e-6
MASKED = -1e30
TINY = 1e-37

SUBLANES = 8
LANES = 128
PACKED_SUB = 4
VMEM_LIMIT = 56 * 1024 * 1024

ROW_TILE = 512
HALO = 2 * SUBLANES
ATTN_QBLOCKS = 8
LRU_CHUNK = 256
LRU_BATCH = SUBLANES
LRU_UNROLL = 8
FFN_ROWS = 512
FFN_FCHUNK = 512
POS_BLOCK = 2048
ROUTE_TILE = 256
STAGE_CHUNK = 192
STATIC_CHUNKS = 3

BF16 = jnp.bfloat16
F32 = jnp.float32


def _params(*sem):
    return pltpu.CompilerParams(dimension_semantics=sem, vmem_limit_bytes=VMEM_LIMIT)


def _in_proj_kernel(x_ref, xp_ref, xn_ref, w_ref, cw_ref, cb_ref, q_ref, k_ref, v_ref, xc_ref, gate_ref,
                    *, tiles_per_seq):
    i = pl.program_id(0)
    xb = x_ref[...].astype(BF16)

    def proj(lo, hi):
        return jnp.dot(xb, w_ref[:, lo:hi], preferred_element_type=F32)

    q_lo, k_lo, v_lo = 0, ATTN_WIDTH, ATTN_WIDTH + KV_WIDTH
    r_lo = v_lo + KV_WIDTH
    g_lo = r_lo + RNN_WIDTH

    ext_b = jnp.concatenate([xp_ref[...].astype(BF16), xb, xn_ref[...].astype(BF16)], axis=0)
    ext = jnp.dot(ext_b, w_ref[:, r_lo:g_lo], preferred_element_type=F32)
    first = i % tiles_per_seq == 0
    last = i % tiles_per_seq == tiles_per_seq - 1
    n = ROW_TILE + 2 * HALO
    ext = jnp.concatenate([jnp.where(first, 0.0, ext[:HALO]), ext[HALO:HALO + ROW_TILE],
                           jnp.where(last, 0.0, ext[HALO + ROW_TILE:])], axis=0)
    mid = slice(HALO, HALO + ROW_TILE)
    w = cw_ref[...]
    xc = pltpu.roll(ext, 2, 0)[mid] * w[0:1] + pltpu.roll(ext, 1, 0)[mid] * w[1:2]
    xc = xc + ext[mid] * w[2:3]
    xc = xc + pltpu.roll(ext, n - 1, 0)[mid] * w[3:4]
    xc_ref[...] = xc + cb_ref[...]

    q_ref[...] = (proj(q_lo, k_lo) * (HEAD_DIM ** -0.5)).astype(BF16)
    k_ref[...] = proj(k_lo, v_lo).astype(BF16)
    v_ref[...] = proj(v_lo, r_lo).astype(BF16)
    gate_ref[...] = proj(g_lo, g_lo + RNN_WIDTH)


def _in_proj(x2, w_bf16, conv_w, conv_b, seq_len):
    t = x2.shape[0]
    in_w = w_bf16.shape[1]
    assert seq_len % ROW_TILE == 0 and ROW_TILE % HALO == 0
    per = ROW_TILE // HALO
    row = lambda w: pl.BlockSpec((ROW_TILE, w), lambda i: (i, 0))
    const = lambda i: (0, 0)
    return pl.pallas_call(
        functools.partial(_in_proj_kernel, tiles_per_seq=seq_len // ROW_TILE),
        grid=(t // ROW_TILE,),
        in_specs=[
            row(D_MODEL),
            pl.BlockSpec((HALO, D_MODEL), lambda i: (jnp.maximum(i * per - 1, 0), 0)),
            pl.BlockSpec((HALO, D_MODEL), lambda i: (jnp.minimum((i + 1) * per, t // HALO - 1), 0)),
            pl.BlockSpec((D_MODEL, in_w), const),
            pl.BlockSpec((CONV_WIDTH, RNN_WIDTH), const),
            pl.BlockSpec((1, RNN_WIDTH), const),
        ],
        out_specs=[row(ATTN_WIDTH), row(KV_WIDTH), row(KV_WIDTH), row(RNN_WIDTH), row(RNN_WIDTH)],
        out_shape=[
            jax.ShapeDtypeStruct((t, ATTN_WIDTH), BF16),
            jax.ShapeDtypeStruct((t, KV_WIDTH), BF16),
            jax.ShapeDtypeStruct((t, KV_WIDTH), BF16),
            jax.ShapeDtypeStruct((t, RNN_WIDTH), F32),
            jax.ShapeDtypeStruct((t, RNN_WIDTH), F32),
        ],
        compiler_params=_params("parallel"),
        name="in_proj",
    )(x2, x2, x2, w_bf16, conv_w, conv_b)


def _attn_kernel(q_ref, kp_ref, kc_ref, kn_ref, vp_ref, vc_ref, vn_ref, bias_first_ref, bias_mid_ref,
                 bias_last_ref, sink_ref, g_ref, o_ref):
    k = jnp.concatenate([kp_ref[0], kc_ref[0], kn_ref[0]], axis=0)
    v = jnp.concatenate([vp_ref[0], vc_ref[0], vn_ref[0]], axis=0)
    v_t = v.astype(F32).T
    pad_rows = 2 * SUBLANES
    ones_row = jnp.where(
        lax.broadcasted_iota(jnp.int32, (pad_rows, 3 * BLOCK), 0) == 0, 1.0, 0.0)
    for qb in range(ATTN_QBLOCKS):
        q = q_ref[0, qb * BLOCK:(qb + 1) * BLOCK, :]
        keys = slice(qb * BLOCK, (qb + 3) * BLOCK)
        bias_ref = (bias_first_ref if qb == 0 else
                    bias_last_ref if qb == ATTN_QBLOCKS - 1 else bias_mid_ref)
        heads = []
        for g in range(N_KV_HEADS):
            qs = jnp.concatenate(
                [q[:, (Q_PER_KV * g + j) * HEAD_DIM:(Q_PER_KV * g + j + 1) * HEAD_DIM]
                 for j in range(Q_PER_KV)], axis=0)
            kg = k[keys, g * HEAD_DIM:(g + 1) * HEAD_DIM]
            s = lax.dot_general(kg, qs, (((1,), (1,)), ((), ())), preferred_element_type=F32)
            s = s + bias_ref[0, g]
            sink = sink_ref[g]
            m = jnp.maximum(jnp.max(s, axis=0, keepdims=True), sink)
            p = jnp.exp(s - m).astype(BF16)
            lhs = jnp.concatenate([v_t[g * HEAD_DIM:(g + 1) * HEAD_DIM, keys], ones_row],
                                  axis=0).astype(BF16)
            o_aug = jnp.dot(lhs, p, preferred_element_type=F32)
            denom = o_aug[HEAD_DIM:HEAD_DIM + 1] + jnp.exp(sink - m)
            o = o_aug[:HEAD_DIM] * (1.0 / denom)
            heads += [o[:, j * BLOCK:(j + 1) * BLOCK] for j in range(Q_PER_KV)]
        y_t = jnp.concatenate(heads, axis=0)
        ms = jnp.mean(y_t * y_t, axis=0, keepdims=True)
        o_ref[0, qb * BLOCK:(qb + 1) * BLOCK, :] = (
            y_t * lax.rsqrt(ms + RMS_EPS) * g_ref[...]).T.astype(BF16)


def _attention(q, k, v, bias, sink_row, attn_g_col):
    b, l, _ = q.shape
    span = ATTN_QBLOCKS * BLOCK
    assert ATTN_QBLOCKS >= 2 and l % span == 0
    ns = l // span
    nb = l // BLOCK
    edge = lambda f: pl.BlockSpec((1, BLOCK, KV_WIDTH), f)
    prev = lambda bi, i: (bi, jnp.maximum(i * ATTN_QBLOCKS - 1, 0), 0)
    cur = lambda bi, i: (bi, i, 0)
    nxt = lambda bi, i: (bi, jnp.minimum((i + 1) * ATTN_QBLOCKS, nb - 1), 0)
    mid = pl.BlockSpec((1, span, KV_WIDTH), cur)
    bias_spec = lambda f: pl.BlockSpec((1,) + bias.shape[1:], f)
    first = lambda bi, i: ((i == 0).astype(jnp.int32), 0, 0, 0)
    last = lambda bi, i: (2 * (i == ns - 1).astype(jnp.int32), 0, 0, 0)
    return pl.pallas_call(
        _attn_kernel,
        grid=(b, ns),
        in_specs=[
            pl.BlockSpec((1, span, ATTN_WIDTH), cur),
            edge(prev), mid, edge(nxt),
            edge(prev), mid, edge(nxt),
            bias_spec(first), bias_spec(lambda bi, i: (0, 0, 0, 0)), bias_spec(last),
            pl.BlockSpec(sink_row.shape, lambda bi, i: (0, 0, 0)),
            pl.BlockSpec(attn_g_col.shape, lambda bi, i: (0, 0)),
        ],
        out_specs=pl.BlockSpec((1, span, ATTN_WIDTH), cur),
        out_shape=jax.ShapeDtypeStruct((b, l, ATTN_WIDTH), BF16),
        compiler_params=_params("parallel", "parallel"),
        name="banded_attention",
    )(q, k, k, k, v, v, v, bias, bias, bias, sink_row, attn_g_col)


def _alibi_bias():
    qi = jnp.arange(BLOCK)[None, :]
    sj = jnp.arange(3 * BLOCK)[:, None]
    rel = sj - BLOCK - qi
    dist = jnp.abs(rel).astype(F32)
    slopes = jnp.asarray([2.0 ** (-8.0 * (h + 1) / N_Q_HEADS) for h in range(N_Q_HEADS)], F32)
    bias = jnp.where((jnp.abs(rel) <= BLOCK)[None], -slopes[:, None, None] * dist[None], MASKED)
    bias = bias.reshape(N_KV_HEADS, Q_PER_KV, 3 * BLOCK, BLOCK).transpose(0, 2, 1, 3)
    bias = bias.reshape(N_KV_HEADS, 3 * BLOCK, Q_PER_KV * BLOCK)
    no_prev = (sj < BLOCK)[None]
    no_next = (sj >= 2 * BLOCK)[None]
    return jnp.stack([
        bias,
        jnp.where(no_prev, MASKED, bias),
        jnp.where(no_next, MASKED, bias),
    ])


def _lru_kernel(x_ref, wa_ref, wx_ref, ba_ref, bx_ref, lam_ref, h_ref, a_scr, u_scr, h_scr,
                carry_ref, *, lc, reverse):
    pitch = lc + SUBLANES
    slabs = RNN_WIDTH // LANES

    @pl.when(pl.program_id(1) == 0)
    def _():
        carry_ref[...] = jnp.zeros_like(carry_ref)

    lam = lam_ref[...]
    softplus_neg_lam = jnp.maximum(-lam, 0.0) + jnp.log1p(jnp.exp(-jnp.abs(lam)))
    half_rate = (-0.5 * LRU_C) * softplus_neg_lam
    for b in range(LRU_BATCH):
        xc = x_ref[b]
        xcb = xc.astype(BF16)
        th_r = jnp.tanh(jnp.dot(xcb, wa_ref[...], preferred_element_type=F32) + ba_ref[...])
        th_i = jnp.tanh(jnp.dot(xcb, wx_ref[...], preferred_element_type=F32) + bx_ref[...])
        log_a = th_r * half_rate + half_rate
        a = jnp.exp(log_a)
        z = jnp.tanh(log_a) * (-1.0 - a * a)
        root = z * lax.rsqrt(jnp.maximum(z, TINY))
        u = root * ((0.5 * th_i + 0.5) * xc)
        for j in range(slabs):
            a_scr[j, b * pitch:b * pitch + lc, :] = a[:, j * LANES:(j + 1) * LANES]
            u_scr[j, b * pitch:b * pitch + lc, :] = u[:, j * LANES:(j + 1) * LANES]

    def step(i, hs):
        t = lc - 1 - i if reverse else i
        out = []
        for j in range(slabs):
            rows = pl.ds(t, LRU_BATCH, stride=pitch)
            h = a_scr[j, rows, :] * hs[j] + u_scr[j, rows, :]
            h_scr[j, rows, :] = h
            out.append(h)
        return tuple(out)

    hs = lax.fori_loop(0, lc, step, tuple(carry_ref[j] for j in range(slabs)), unroll=LRU_UNROLL)
    for j in range(slabs):
        carry_ref[j] = hs[j]
        for b in range(LRU_BATCH):
            h_ref[b, :, j * LANES:(j + 1) * LANES] = h_scr[j, b * pitch:b * pitch + lc, :]


def _lru(xc, wa, wx, ba, bx, lam, *, reverse):
    b, l, _ = xc.shape
    lc = LRU_CHUNK
    assert b % LRU_BATCH == 0 and l % lc == 0
    nc = l // lc
    pos = (lambda c: nc - 1 - c) if reverse else (lambda c: c)
    const = lambda bi, c: (0, 0)
    vec = pl.BlockSpec((1, RNN_WIDTH), const)
    mat = pl.BlockSpec((RNN_WIDTH, RNN_WIDTH), const)
    chunk = pl.BlockSpec((LRU_BATCH, lc, RNN_WIDTH), lambda bi, c: (bi, pos(c), 0))
    slabs = RNN_WIDTH // LANES
    scratch = pltpu.VMEM((slabs, LRU_BATCH * (lc + SUBLANES), LANES), F32)
    return pl.pallas_call(
        functools.partial(_lru_kernel, lc=lc, reverse=reverse),
        grid=(b // LRU_BATCH, nc),
        in_specs=[chunk, mat, mat, vec, vec, vec],
        out_specs=chunk,
        out_shape=jax.ShapeDtypeStruct((b, l, RNN_WIDTH), F32),
        scratch_shapes=[scratch, scratch, scratch, pltpu.VMEM((slabs, LRU_BATCH, LANES), F32)],
        compiler_params=_params("parallel", "arbitrary"),
        name="rg_lru_bwd" if reverse else "rg_lru_fwd",
    )(xc, wa, wx, ba, bx, lam)


def _block_diag(w):
    eye = jnp.eye(RNN_BLOCKS, dtype=w.dtype)
    return jnp.einsum('hij,hk->hikj', w, eye).reshape(RNN_WIDTH, RNN_WIDTH)


def _mix_out_kernel(attn_ref, hf_ref, hb_ref, gate_ref, x_ref, wo_ref, rg_ref, g1_ref, b1_ref,
                    wr_ref, h_ref, hb16_ref, aff_ref):
    gt = gate_ref[...]
    gelu = 0.5 * gt * (1.0 + jnp.tanh(math.sqrt(2.0 / math.pi) * (gt + 0.044715 * (gt * gt * gt))))
    yr = (hf_ref[...] + hb_ref[...]) * gelu
    ms = jnp.mean(yr * yr, axis=-1, keepdims=True)
    yn = (yr * lax.rsqrt(ms + RMS_EPS) * rg_ref[...]).astype(BF16)
    mix = jnp.dot(attn_ref[...], wo_ref[:ATTN_WIDTH, :], preferred_element_type=F32)
    mix = mix + jnp.dot(yn, wo_ref[ATTN_WIDTH:, :], preferred_element_type=F32)
    z = ALPHA * x_ref[...] + mix
    mu = jnp.mean(z, axis=-1, keepdims=True)
    zc = z - mu
    var = jnp.mean(zc * zc, axis=-1, keepdims=True)
    h = zc * lax.rsqrt(var + LN_EPS) * g1_ref[...] + b1_ref[...]
    h_ref[...] = h
    hb = h.astype(BF16)
    hb16_ref[...] = hb
    logits = lax.dot_general(wr_ref[...], hb, (((1,), (1,)), ((), ())),
                             preferred_element_type=F32)
    mx = jnp.max(logits, axis=0, keepdims=True)
    e = jnp.exp(logits - mx)
    aff_ref[...] = e / jnp.sum(e, axis=0, keepdims=True)


def _mix_out(attn, hf, hb, gate, x2, wo, rnn_g, g1, b1, wr_t):
    t = x2.shape[0]
    row = lambda w: pl.BlockSpec((ROW_TILE, w), lambda i: (i, 0))
    const = lambda i: (0, 0)
    return pl.pallas_call(
        _mix_out_kernel,
        grid=(t // ROW_TILE,),
        in_specs=[
            row(ATTN_WIDTH), row(RNN_WIDTH), row(RNN_WIDTH), row(RNN_WIDTH), row(D_MODEL),
            pl.BlockSpec((D_MODEL, D_MODEL), const),
            pl.BlockSpec((1, RNN_WIDTH), const),
            pl.BlockSpec((1, D_MODEL), const),
            pl.BlockSpec((1, D_MODEL), const),
            pl.BlockSpec((N_EXPERTS, D_MODEL), const),
        ],
        out_specs=[row(D_MODEL), row(D_MODEL), pl.BlockSpec((N_EXPERTS, ROW_TILE), lambda i: (0, i))],
        out_shape=[
            jax.ShapeDtypeStruct((t, D_MODEL), F32),
            jax.ShapeDtypeStruct((t, D_MODEL), BF16),
            jax.ShapeDtypeStruct((N_EXPERTS, t), F32),
        ],
        compiler_params=_params("parallel"),
        name="mix_out_ln1_router",
    )(attn, hf, hb, gate, x2, wo, rnn_g, g1, b1, wr_t)


def _pack_rows(rows):
    half = D_MODEL // 2
    lo = pltpu.bitcast(rows[:, :half].astype(BF16).astype(F32), jnp.uint32)
    hi = pltpu.bitcast(rows[:, half:].astype(BF16).astype(F32), jnp.uint32)
    return hi | (lo >> 16)


def _unpack_rows(words):
    lo = pltpu.bitcast(words << 16, F32)
    hi = pltpu.bitcast(words & jnp.uint32(0xFFFF0000), F32)
    return jnp.concatenate([lo, hi], axis=1).astype(BF16)


def _ffn_kernel(x_ref, wg_ref, wu_ref, wd_ref, o_ref):
    x = _unpack_rows(jnp.concatenate(
        [x_ref[pl.ds(j, FFN_ROWS, stride=PACKED_SUB), :] for j in range(PACKED_SUB)], axis=1))
    acc = jnp.zeros((FFN_ROWS, D_MODEL), F32)
    for f in range(EXPERT_FF // FFN_FCHUNK):
        cols = slice(f * FFN_FCHUNK, (f + 1) * FFN_FCHUNK)
        gt = jnp.dot(x, wg_ref[0, :, cols], preferred_element_type=F32)
        up = jnp.dot(x, wu_ref[0, :, cols], preferred_element_type=F32)
        hid = (gt * jax.nn.sigmoid(gt) * up).astype(BF16)
        acc = acc + jnp.dot(hid, wd_ref[0, cols, :], preferred_element_type=F32)
    words = _pack_rows(acc)
    for j in range(PACKED_SUB):
        o_ref[pl.ds(j, FFN_ROWS, stride=PACKED_SUB), :] = words[:, j * LANES:(j + 1) * LANES]


def _ffn(xe, wg, wu, wd, cap):
    e = wg.shape[0]
    per_e = cap // FFN_ROWS
    slots = pl.BlockSpec((FFN_ROWS * PACKED_SUB, LANES), lambda ei, j: (ei * per_e + j, 0))
    wspec = lambda a, b: pl.BlockSpec((1, a, b), lambda ei, j: (ei, 0, 0))
    return pl.pallas_call(
        _ffn_kernel,
        grid=(e, per_e),
        in_specs=[slots, wspec(D_MODEL, EXPERT_FF), wspec(D_MODEL, EXPERT_FF),
                  wspec(EXPERT_FF, D_MODEL)],
        out_specs=slots,
        out_shape=jax.ShapeDtypeStruct(xe.shape, jnp.uint32),
        compiler_params=_params("parallel", "arbitrary"),
        name="expert_ffn",
    )(xe, wg, wu, wd)


def _threshold_kernel(aff_ref, thr_ref, need_ref, *, cap):
    def count(mask):
        return jnp.sum(jnp.where(mask, 1.0, 0.0), axis=1, keepdims=True)

    def body(b, thr):
        cand = thr | jnp.left_shift(jnp.int32(1), 30 - b)
        bits = pltpu.bitcast(aff_ref[...], jnp.int32)
        return jnp.where(count(bits >= cand) >= cap, cand, thr)

    thr = lax.fori_loop(0, 31, body, jnp.zeros((N_EXPERTS, 1), jnp.int32))
    bits = pltpu.bitcast(aff_ref[...], jnp.int32)
    thr_ref[...] = thr
    need_ref[...] = cap - count(bits > thr)


def _positions_kernel(aff_ref, thr_ref, need_ref, pos_ref, tie_run_ref, sel_run_ref, *, block):
    @pl.when(pl.program_id(0) == 0)
    def _():
        tie_run_ref[...] = jnp.zeros_like(tie_run_ref)
        sel_run_ref[...] = jnp.zeros_like(sel_run_ref)

    thr = thr_ref[...]
    need = need_ref[...]
    r = lax.broadcasted_iota(jnp.int32, (LANES, LANES), 0)
    c = lax.broadcasted_iota(jnp.int32, (LANES, LANES), 1)
    before = jnp.where(r < c, 1.0, 0.0).astype(BF16)
    tie_run = tie_run_ref[...]
    sel_run = sel_run_ref[...]
    for s in range(block // LANES):
        lanes = slice(s * LANES, (s + 1) * LANES)
        bits = pltpu.bitcast(aff_ref[:, lanes], jnp.int32)
        tie = jnp.where(bits == thr, 1.0, 0.0)
        tie_before = jnp.dot(tie.astype(BF16), before, preferred_element_type=F32) + tie_run
        sel = (bits > thr) | ((bits == thr) & (tie_before < need))
        picked = jnp.where(sel, 1.0, 0.0)
        sel_before = jnp.dot(picked.astype(BF16), before, preferred_element_type=F32) + sel_run
        pos_ref[:, lanes] = jnp.where(sel, sel_before.astype(jnp.int32), -1)
        tie_run = tie_run + jnp.sum(tie, axis=1, keepdims=True)
        sel_run = sel_run + jnp.sum(picked, axis=1, keepdims=True)
    tie_run_ref[...] = tie_run
    sel_run_ref[...] = sel_run


def _route(aff_t, cap):
    e, t = aff_t.shape
    col = jax.ShapeDtypeStruct((e, 1), jnp.int32)
    thr, need = pl.pallas_call(
        functools.partial(_threshold_kernel, cap=cap),
        out_shape=[col, jax.ShapeDtypeStruct((e, 1), F32)],
        compiler_params=_params(),
        name="route_threshold",
    )(aff_t)
    block = min(POS_BLOCK, t)
    cspec = pl.BlockSpec((e, 1), lambda i: (0, 0))
    return pl.pallas_call(
        functools.partial(_positions_kernel, block=block),
        grid=(t // block,),
        in_specs=[pl.BlockSpec((e, block), lambda i: (0, i)), cspec, cspec],
        out_specs=pl.BlockSpec((e, block), lambda i: (0, i)),
        out_shape=jax.ShapeDtypeStruct((e, t), jnp.int32),
        scratch_shapes=[pltpu.VMEM((e, 1), F32), pltpu.VMEM((e, 1), F32)],
        compiler_params=_params("arbitrary"),
        name="route_positions",
    )(aff_t, thr, need)


def _tile_tables(pos, tile):
    e, t = pos.shape
    ntile = t // tile
    n = jnp.sum((pos >= 0).reshape(e, ntile, tile), axis=-1, dtype=jnp.int32)
    s0 = jnp.cumsum(n, axis=1) - n
    end = s0 + n
    s0a = s0 - s0 % 2
    per_tile = lambda a: a.T.reshape(ntile, e, 1)

    wr = end - end % 2 - s0a
    wr_off = jnp.cumsum(wr, axis=0) - wr
    carried = s0 % 2 == 1
    carry_row = jnp.where(carried & (n > 0), wr_off, -1)
    lanes = jnp.full((ntile, 1, LANES), -1, jnp.int32).at[:, 0, :e].set(carry_row.T)
    dispatch = dict(
        count=wr.reshape(-1), start=s0a.reshape(-1), shift=per_tile(wr_off - s0a),
        end=per_tile(end), dangling=per_tile((end % 2 == 1).astype(jnp.int32)),
        keep=per_tile((carried & (n == 0)).astype(jnp.int32)), carry_row=lanes)

    rd = jnp.where(n > 0, (end + 1) // 2 * 2 - s0a, 0)
    rd_off = jnp.cumsum(rd, axis=0) - rd
    combine = dict(count=rd.reshape(-1), start=s0a.reshape(-1), shift=per_tile(rd_off - s0a))
    return dispatch, combine


def _tile_counts(n_ref, tile, ntile):
    counts = [n_ref[e * ntile + tile] for e in range(N_EXPERTS)]
    offs = [jnp.int32(0)]
    for c in counts:
        offs.append(offs[-1] + c)
    return counts, offs


def _one_hot_t(stagepos, kbase, values=None):
    rel = stagepos - kbase
    rel = jnp.where((rel >= 0) & (rel < STAGE_CHUNK), rel, -1).astype(F32).astype(BF16)
    k = lax.broadcasted_iota(jnp.int32, (STAGE_CHUNK, stagepos.shape[1]), 0).astype(F32).astype(BF16)
    one = jnp.ones((1, stagepos.shape[1]), BF16)
    pt = jnp.zeros(k.shape, BF16)
    for e in range(N_EXPERTS):
        hit = rel[e:e + 1, :] == k
        pt = jnp.where(hit, one if values is None else values[e:e + 1, :], pt)
    return pt


def _stage_rows(e, tile, extra=0):
    chunks = max(STATIC_CHUNKS, -(-(e * tile + extra) // STAGE_CHUNK))
    return chunks * STAGE_CHUNK


def _for_each_chunk(total, body, init):
    carry = init
    for c in range(STATIC_CHUNKS):
        carry = body(c * STAGE_CHUNK, carry)
    return lax.fori_loop(
        STATIC_CHUNKS, (total + STAGE_CHUNK - 1) // STAGE_CHUNK,
        lambda c, carry: body(pl.multiple_of(c * STAGE_CHUNK, STAGE_CHUNK), carry), carry)


def _pairs_copy(src, dst, sem, src_row, dst_row, n):
    return pltpu.make_async_copy(
        src.at[pl.ds(pl.multiple_of(src_row * PACKED_SUB, SUBLANES), n // 2 * SUBLANES)],
        dst.at[pl.ds(pl.multiple_of(dst_row * PACKED_SUB, SUBLANES), n // 2 * SUBLANES)], sem)


def _dispatch_kernel(n_ref, s0_ref, pos_ref, shift_ref, end_ref, dangling_ref, keep_ref,
                     carry_row_ref, h_ref, xe_hbm, stage_ref, carry_ref, sem, *, ntile, cap):
    i = pl.program_id(0)
    slot = i % 2

    def wait_writes(tile, slot):
        _, offs = _tile_counts(n_ref, tile, ntile)

        @pl.when(offs[-1] > 0)
        def _():
            _pairs_copy(stage_ref.at[slot], xe_hbm, sem.at[slot], 0, 0, offs[-1]).wait()

    @pl.when(i >= 2)
    def _():
        wait_writes(i - 2, slot)

    @pl.when(i == 0)
    def _():
        carry_ref[...] = jnp.zeros_like(carry_ref)

    counts, offs = _tile_counts(n_ref, i, ntile)
    pos = pos_ref[...]
    h = h_ref[...]
    carry = carry_ref[...]
    trailing = (pos == end_ref[0] - 1) & (dangling_ref[0] > 0)
    stagepos = jnp.where((pos >= 0) & ~trailing, pos + shift_ref[0], -1)
    carry_row = carry_row_ref[0]

    def chunk(kbase, acc):
        k = lax.broadcasted_iota(jnp.int32, (STAGE_CHUNK, LANES), 0) + kbase
        place_carry = jnp.where(carry_row == k, 1.0, 0.0).astype(BF16)
        rows = jnp.dot(_one_hot_t(stagepos, kbase), h, preferred_element_type=F32)
        rows = rows + jnp.dot(place_carry, carry, preferred_element_type=F32)
        words = _pack_rows(rows)
        for j in range(PACKED_SUB):
            stage_ref[slot, pl.ds(kbase * PACKED_SUB + j, STAGE_CHUNK, stride=PACKED_SUB), :] = (
                words[:, j * LANES:(j + 1) * LANES])
        return acc

    _for_each_chunk(offs[-1], chunk, 0)

    er = lax.broadcasted_iota(jnp.int32, (N_EXPERTS, LANES), 0)
    ec = lax.broadcasted_iota(jnp.int32, (N_EXPERTS, LANES), 1)
    keep = jnp.where((er == ec) & (keep_ref[0] > 0), 1.0, 0.0).astype(BF16)
    take = jnp.where(trailing, 1.0, 0.0).astype(BF16)
    new_carry = (jnp.dot(take, h, preferred_element_type=F32)
                 + jnp.dot(keep, carry, preferred_element_type=F32))
    carry_ref[:N_EXPERTS, :] = new_carry.astype(BF16)

    for e in range(N_EXPERTS):
        @pl.when(counts[e] > 0)
        def _(e=e):
            _pairs_copy(stage_ref.at[slot], xe_hbm, sem.at[slot], offs[e],
                        e * cap + s0_ref[e * ntile + i], counts[e]).start()

    @pl.when(i == ntile - 1)
    def _():
        wait_writes(i, slot)
        if ntile > 1:
            wait_writes(i - 1, 1 - slot)


def _dispatch(tab, pos, h1b, cap):
    e, t = pos.shape
    tile = ROUTE_TILE
    ntile = t // tile
    assert cap % 2 == 0 and e % (2 * SUBLANES) == 0
    col = pl.BlockSpec((1, e, 1), lambda i, *_: (i, 0, 0))
    return pl.pallas_call(
        functools.partial(_dispatch_kernel, ntile=ntile, cap=cap),
        grid_spec=pltpu.PrefetchScalarGridSpec(
            num_scalar_prefetch=2,
            grid=(ntile,),
            in_specs=[
                pl.BlockSpec((e, tile), lambda i, *_: (0, i)),
                col, col, col, col,
                pl.BlockSpec((1, 1, LANES), lambda i, *_: (i, 0, 0)),
                pl.BlockSpec((tile, D_MODEL), lambda i, *_: (i, 0)),
            ],
            out_specs=pl.BlockSpec(memory_space=pl.ANY),
            scratch_shapes=[pltpu.VMEM((2, _stage_rows(e, tile) * PACKED_SUB, LANES), jnp.uint32),
                            pltpu.VMEM((LANES, D_MODEL), BF16),
                            pltpu.SemaphoreType.DMA((2,))],
        ),
        out_shape=jax.ShapeDtypeStruct((e * cap * PACKED_SUB, LANES), jnp.uint32),
        compiler_params=_params("arbitrary"),
        name="route_dispatch",
    )(tab["count"], tab["start"], pos, tab["shift"], tab["end"], tab["dangling"], tab["keep"],
      tab["carry_row"], h1b)


def _combine_kernel(n_ref, s0_ref, pos_ref, aff_ref, shift_ref, h_ref, g_ref, b_ref, ye_hbm, o_ref,
                    stage_ref, sem, *, ntile, cap):
    i = pl.program_id(0)
    slot = i % 2

    def fetch(tile, slot):
        counts, offs = _tile_counts(n_ref, tile, ntile)
        for e in range(N_EXPERTS):
            @pl.when(counts[e] > 0)
            def _(e=e):
                _pairs_copy(ye_hbm, stage_ref.at[slot], sem.at[slot],
                            e * cap + s0_ref[e * ntile + tile], offs[e], counts[e]).start()

    @pl.when(i == 0)
    def _():
        stage_ref[...] = jnp.zeros_like(stage_ref)
        fetch(0, 0)

    @pl.when(i + 1 < ntile)
    def _():
        fetch(i + 1, 1 - slot)

    _, offs = _tile_counts(n_ref, i, ntile)

    @pl.when(offs[-1] > 0)
    def _():
        _pairs_copy(ye_hbm, stage_ref.at[slot], sem.at[slot], 0, 0, offs[-1]).wait()

    pos = pos_ref[...]
    stagepos = jnp.where(pos >= 0, pos + shift_ref[0], -1)
    gates = aff_ref[...].astype(BF16)

    def chunk(kbase, acc):
        ye = _unpack_rows(jnp.concatenate(
            [stage_ref[slot, pl.ds(kbase * PACKED_SUB + j, STAGE_CHUNK, stride=PACKED_SUB), :]
             for j in range(PACKED_SUB)], axis=1))
        return acc + lax.dot_general(_one_hot_t(stagepos, kbase, gates), ye,
                                     (((0,), (0,)), ((), ())), preferred_element_type=F32)

    z = _for_each_chunk(offs[-1], chunk, ALPHA * h_ref[...])
    mu = jnp.mean(z, axis=-1, keepdims=True)
    zc = z - mu
    var = jnp.mean(zc * zc, axis=-1, keepdims=True)
    o_ref[...] = zc * lax.rsqrt(var + LN_EPS) * g_ref[...] + b_ref[...]


def _combine(tab, pos, aff_t, h1, g2, b2, ye, cap):
    e, t = pos.shape
    tile = ROUTE_TILE
    ntile = t // tile
    lanes = pl.BlockSpec((e, tile), lambda i, *_: (0, i))
    vec = pl.BlockSpec((1, D_MODEL), lambda i, *_: (0, 0))
    row = pl.BlockSpec((tile, D_MODEL), lambda i, *_: (i, 0))
    return pl.pallas_call(
        functools.partial(_combine_kernel, ntile=ntile, cap=cap),
        grid_spec=pltpu.PrefetchScalarGridSpec(
            num_scalar_prefetch=2,
            grid=(ntile,),
            in_specs=[lanes, lanes, pl.BlockSpec((1, e, 1), lambda i, *_: (i, 0, 0)), row, vec, vec,
                      pl.BlockSpec(memory_space=pl.ANY)],
            out_specs=row,
            scratch_shapes=[pltpu.VMEM((2, _stage_rows(e, tile, 2 * e) * PACKED_SUB, LANES),
                                       jnp.uint32),
                            pltpu.SemaphoreType.DMA((2,))],
        ),
        out_shape=jax.ShapeDtypeStruct((t, D_MODEL), F32),
        compiler_params=_params("arbitrary"),
        name="route_combine_ln2",
    )(tab["count"], tab["start"], pos, aff_t, tab["shift"], h1, g2, b2, ye)


def _layer(x, p):
    b, l, _ = x.shape
    t = b * l
    cap = CAPACITY_FACTOR * t // N_EXPERTS
    x2 = x.reshape(t, D_MODEL)
    q, k, v, xc, gate = _in_proj(x2, p["w_in"], p["conv_w"], p["conv_b"], l)
    attn = _attention(q.reshape(b, l, -1), k.reshape(b, l, -1), v.reshape(b, l, -1),
                      p["bias"], p["sink_row"], p["attn_g_col"])
    xc3 = xc.reshape(b, l, RNN_WIDTH)
    hs = [_lru(xc3, p["wa"][d], p["wx"][d], p["ba"][d], p["bx"][d], p["lam"][d], reverse=bool(d))
          for d in range(2)]
    h1, h1b, aff_t = _mix_out(attn.reshape(t, -1), hs[0].reshape(t, -1), hs[1].reshape(t, -1), gate,
                              x2, p["w_out"], p["rnn_g"], p["ln1_g"], p["ln1_b"], p["wr_t"])
    pos = _route(aff_t, cap)
    dispatch_tab, combine_tab = _tile_tables(pos, ROUTE_TILE)
    xe = _dispatch(dispatch_tab, pos, h1b, cap)
    ye = _ffn(xe, p["wg"], p["wu"], p["wd"], cap)
    out = _combine(combine_tab, pos, aff_t, h1, p["ln2_g"], p["ln2_b"], ye, cap)
    return out.reshape(b, l, D_MODEL)


def _layer_params(li, w_in, attn_sink, attn_norm_g, rnn_norm_g, conv_w, conv_b, lru_w_a, lru_b_a,
                  lru_w_x, lru_b_x, lru_lambda, w_out, ln1_g, ln1_b, w_router, w_gate, w_up, w_down,
                  ln2_g, ln2_b):
    vec = lambda a: a[li].reshape(1, -1)
    per_dir = lambda f: [f(d) for d in range(2)]
    return dict(
        w_in=w_in[li].astype(BF16),
        bias=_alibi_bias(),
        sink_row=jnp.repeat(attn_sink[li].astype(F32), BLOCK).reshape(N_KV_HEADS, 1, Q_PER_KV * BLOCK),
        attn_g_col=jnp.broadcast_to(attn_norm_g[li].astype(F32)[:, None], (ATTN_WIDTH, BLOCK)),
        rnn_g=vec(rnn_norm_g),
        conv_w=conv_w[li], conv_b=vec(conv_b),
        wa=per_dir(lambda d: (0.5 * _block_diag(lru_w_a[li, d])).astype(BF16)),
        wx=per_dir(lambda d: (0.5 * _block_diag(lru_w_x[li, d])).astype(BF16)),
        ba=per_dir(lambda d: 0.5 * lru_b_a[li, d].reshape(1, -1)),
        bx=per_dir(lambda d: 0.5 * lru_b_x[li, d].reshape(1, -1)),
        lam=per_dir(lambda d: lru_lambda[li, d].reshape(1, -1)),
        w_out=w_out[li].astype(BF16),
        ln1_g=vec(ln1_g), ln1_b=vec(ln1_b),
        wr_t=w_router[li].T.astype(BF16),
        wg=w_gate[li].astype(BF16), wu=w_up[li].astype(BF16), wd=w_down[li].astype(BF16),
        ln2_g=vec(ln2_g), ln2_b=vec(ln2_b),
    )


def kernel(x_prompt, x_sample, w_in, attn_sink, attn_norm_g, rnn_norm_g, conv_w, conv_b, lru_w_a,
           lru_b_a, lru_w_x, lru_b_x, lru_lambda, w_out, ln1_g, ln1_b, w_router, w_gate, w_up,
           w_down, ln2_g, ln2_b):
    layers = [
        _layer_params(li, w_in, attn_sink, attn_norm_g, rnn_norm_g, conv_w, conv_b, lru_w_a, lru_b_a,
                      lru_w_x, lru_b_x, lru_lambda, w_out, ln1_g, ln1_b, w_router, w_gate, w_up,
                      w_down, ln2_g, ln2_b)
        for li in range(w_in.shape[0])]
    ys = []
    for x in (x_prompt, x_sample):
        for p in layers:
            x = _layer(x, p)
        ys.append(x)
    return tuple(ys)
```

```python
import functools
import math

import jax
import jax.numpy as jnp
from jax import lax
from jax.experimental import pallas as pl
from jax.experimental.pallas import tpu as pltpu

D_MODEL = 1024
HEAD_DIM = 64
N_Q_HEADS = 8
N_KV_HEADS = 2
Q_PER_KV = N_Q_HEADS // N_KV_HEADS
ATTN_WIDTH = N_Q_HEADS * HEAD_DIM
KV_WIDTH = N_KV_HEADS * HEAD_DIM
BLOCK = 128
RNN_WIDTH = 512
RNN_BLOCKS = 8
RNN_BLOCK_W = RNN_WIDTH // RNN_BLOCKS
CONV_WIDTH = 4
LRU_C = 8.0
N_EXPERTS = 16
EXPERT_FF = 2048
CAPACITY_FACTOR = 2
ALPHA = 2.0 ** 0.25
LN_EPS = 1e-5
RMS_EPS = 1e-6
MASKED = -1e30
TINY = 1e-37

SUBLANES = 8
LANES = 128
PACKED_SUB = 4
VMEM_LIMIT = 56 * 1024 * 1024

ROW_TILE = 512
HALO = 2 * SUBLANES
ATTN_QBLOCKS = 8
LRU_CHUNK = 256
LRU_BATCH = SUBLANES
LRU_UNROLL = 8
FFN_ROWS = 512
FFN_FCHUNK = 512
POS_BLOCK = 2048
ROUTE_TILE = 256
STAGE_CHUNK = 192
STATIC_CHUNKS = 3

BF16 = jnp.bfloat16
F32 = jnp.float32


def _params(*sem):
    return pltpu.CompilerParams(dimension_semantics=sem, vmem_limit_bytes=VMEM_LIMIT)


def _in_proj_kernel(x_ref, xp_ref, xn_ref, w_ref, cw_ref, cb_ref, q_ref, k_ref, v_ref, xc_ref, gate_ref,
                    *, tiles_per_seq):
    i = pl.program_id(0)
    xb = x_ref[...].astype(BF16)

    def proj(lo, hi):
        return jnp.dot(xb, w_ref[:, lo:hi], preferred_element_type=F32)

    q_lo, k_lo, v_lo = 0, ATTN_WIDTH, ATTN_WIDTH + KV_WIDTH
    r_lo = v_lo + KV_WIDTH
    g_lo = r_lo + RNN_WIDTH

    ext_b = jnp.concatenate([xp_ref[...].astype(BF16), xb, xn_ref[...].astype(BF16)], axis=0)
    ext = jnp.dot(ext_b, w_ref[:, r_lo:g_lo], preferred_element_type=F32)
    first = i % tiles_per_seq == 0
    last = i % tiles_per_seq == tiles_per_seq - 1
    n = ROW_TILE + 2 * HALO
    ext = jnp.concatenate([jnp.where(first, 0.0, ext[:HALO]), ext[HALO:HALO + ROW_TILE],
                           jnp.where(last, 0.0, ext[HALO + ROW_TILE:])], axis=0)
    mid = slice(HALO, HALO + ROW_TILE)
    w = cw_ref[...]
    xc = pltpu.roll(ext, 2, 0)[mid] * w[0:1] + pltpu.roll(ext, 1, 0)[mid] * w[1:2]
    xc = xc + ext[mid] * w[2:3]
    xc = xc + pltpu.roll(ext, n - 1, 0)[mid] * w[3:4]
    xc_ref[...] = xc + cb_ref[...]

    q_ref[...] = (proj(q_lo, k_lo) * (HEAD_DIM ** -0.5)).astype(BF16)
    k_ref[...] = proj(k_lo, v_lo).astype(BF16)
    v_ref[...] = proj(v_lo, r_lo).astype(BF16)
    gate_ref[...] = proj(g_lo, g_lo + RNN_WIDTH)


def _in_proj(x2, w_bf16, conv_w, conv_b, seq_len):
    t = x2.shape[0]
    in_w = w_bf16.shape[1]
    assert seq_len % ROW_TILE == 0 and ROW_TILE % HALO == 0
    per = ROW_TILE // HALO
    row = lambda w: pl.BlockSpec((ROW_TILE, w), lambda i: (i, 0))
    const = lambda i: (0, 0)
    return pl.pallas_call(
        functools.partial(_in_proj_kernel, tiles_per_seq=seq_len // ROW_TILE),
        grid=(t // ROW_TILE,),
        in_specs=[
            row(D_MODEL),
            pl.BlockSpec((HALO, D_MODEL), lambda i: (jnp.maximum(i * per - 1, 0), 0)),
            pl.BlockSpec((HALO, D_MODEL), lambda i: (jnp.minimum((i + 1) * per, t // HALO - 1), 0)),
            pl.BlockSpec((D_MODEL, in_w), const),
            pl.BlockSpec((CONV_WIDTH, RNN_WIDTH), const),
            pl.BlockSpec((1, RNN_WIDTH), const),
        ],
        out_specs=[row(ATTN_WIDTH), row(KV_WIDTH), row(KV_WIDTH), row(RNN_WIDTH), row(RNN_WIDTH)],
        out_shape=[
            jax.ShapeDtypeStruct((t, ATTN_WIDTH), BF16),
            jax.ShapeDtypeStruct((t, KV_WIDTH), BF16),
            jax.ShapeDtypeStruct((t, KV_WIDTH), BF16),
            jax.ShapeDtypeStruct((t, RNN_WIDTH), F32),
            jax.ShapeDtypeStruct((t, RNN_WIDTH), F32),
        ],
        compiler_params=_params("parallel"),
        name="in_proj",
    )(x2, x2, x2, w_bf16, conv_w, conv_b)


def _attn_kernel(q_ref, kp_ref, kc_ref, kn_ref, vp_ref, vc_ref, vn_ref, bias_first_ref, bias_mid_ref,
                 bias_last_ref, sink_ref, g_ref, o_ref):
    k = jnp.concatenate([kp_ref[0], kc_ref[0], kn_ref[0]], axis=0)
    v = jnp.concatenate([vp_ref[0], vc_ref[0], vn_ref[0]], axis=0)
    v_t = v.astype(F32).T
    pad_rows = 2 * SUBLANES
    ones_row = jnp.where(
        lax.broadcasted_iota(jnp.int32, (pad_rows, 3 * BLOCK), 0) == 0, 1.0, 0.0)
    for qb in range(ATTN_QBLOCKS):
        q = q_ref[0, qb * BLOCK:(qb + 1) * BLOCK, :]
        keys = slice(qb * BLOCK, (qb + 3) * BLOCK)
        bias_ref = (bias_first_ref if qb == 0 else
                    bias_last_ref if qb == ATTN_QBLOCKS - 1 else bias_mid_ref)
        heads = []
        for g in range(N_KV_HEADS):
            qs = jnp.concatenate(
                [q[:, (Q_PER_KV * g + j) * HEAD_DIM:(Q_PER_KV * g + j + 1) * HEAD_DIM]
                 for j in range(Q_PER_KV)], axis=0)
            kg = k[keys, g * HEAD_DIM:(g + 1) * HEAD_DIM]
            s = lax.dot_general(kg, qs, (((1,), (1,)), ((), ())), preferred_element_type=F32)
            s = s + bias_ref[0, g]
            sink = sink_ref[g]
            m = jnp.maximum(jnp.max(s, axis=0, keepdims=True), sink)
            p = jnp.exp(s - m).astype(BF16)
            lhs = jnp.concatenate([v_t[g * HEAD_DIM:(g + 1) * HEAD_DIM, keys], ones_row],
                                  axis=0).astype(BF16)
            o_aug = jnp.dot(lhs, p, preferred_element_type=F32)
            denom = o_aug[HEAD_DIM:HEAD_DIM + 1] + jnp.exp(sink - m)
            o = o_aug[:HEAD_DIM] * (1.0 / denom)
            heads += [o[:, j * BLOCK:(j + 1) * BLOCK] for j in range(Q_PER_KV)]
        y_t = jnp.concatenate(heads, axis=0)
        ms = jnp.mean(y_t * y_t, axis=0, keepdims=True)
        o_ref[0, qb * BLOCK:(qb + 1) * BLOCK, :] = (
            y_t * lax.rsqrt(ms + RMS_EPS) * g_ref[...]).T.astype(BF16)


def _attention(q, k, v, bias, sink_row, attn_g_col):
    b, l, _ = q.shape
    span = ATTN_QBLOCKS * BLOCK
    assert ATTN_QBLOCKS >= 2 and l % span == 0
    ns = l // span
    nb = l // BLOCK
    edge = lambda f: pl.BlockSpec((1, BLOCK, KV_WIDTH), f)
    prev = lambda bi, i: (bi, jnp.maximum(i * ATTN_QBLOCKS - 1, 0), 0)
    cur = lambda bi, i: (bi, i, 0)
    nxt = lambda bi, i: (bi, jnp.minimum((i + 1) * ATTN_QBLOCKS, nb - 1), 0)
    mid = pl.BlockSpec((1, span, KV_WIDTH), cur)
    bias_spec = lambda f: pl.BlockSpec((1,) + bias.shape[1:], f)
    first = lambda bi, i: ((i == 0).astype(jnp.int32), 0, 0, 0)
    last = lambda bi, i: (2 * (i == ns - 1).astype(jnp.int32), 0, 0, 0)
    return pl.pallas_call(
        _attn_kernel,
        grid=(b, ns),
        in_specs=[
            pl.BlockSpec((1, span, ATTN_WIDTH), cur),
            edge(prev), mid, edge(nxt),
            edge(prev), mid, edge(nxt),
            bias_spec(first), bias_spec(lambda bi, i: (0, 0, 0, 0)), bias_spec(last),
            pl.BlockSpec(sink_row.shape, lambda bi, i: (0, 0, 0)),
            pl.BlockSpec(attn_g_col.shape, lambda bi, i: (0, 0)),
        ],
        out_specs=pl.BlockSpec((1, span, ATTN_WIDTH), cur),
        out_shape=jax.ShapeDtypeStruct((b, l, ATTN_WIDTH), BF16),
        compiler_params=_params("parallel", "parallel"),
        name="banded_attention",
    )(q, k, k, k, v, v, v, bias, bias, bias, sink_row, attn_g_col)


def _alibi_bias():
    qi = jnp.arange(BLOCK)[None, :]
    sj = jnp.arange(3 * BLOCK)[:, None]
    rel = sj - BLOCK - qi
    dist = jnp.abs(rel).astype(F32)
    slopes = jnp.asarray([2.0 ** (-8.0 * (h + 1) / N_Q_HEADS) for h in range(N_Q_HEADS)], F32)
    bias = jnp.where((jnp.abs(rel) <= BLOCK)[None], -slopes[:, None, None] * dist[None], MASKED)
    bias = bias.reshape(N_KV_HEADS, Q_PER_KV, 3 * BLOCK, BLOCK).transpose(0, 2, 1, 3)
    bias = bias.reshape(N_KV_HEADS, 3 * BLOCK, Q_PER_KV * BLOCK)
    no_prev = (sj < BLOCK)[None]
    no_next = (sj >= 2 * BLOCK)[None]
    return jnp.stack([
        bias,
        jnp.where(no_prev, MASKED, bias),
        jnp.where(no_next, MASKED, bias),
    ])


def _lru_kernel(x_ref, other_ref, wa_ref, wx_ref, ba_ref, bx_ref, lam_ref, h_ref, a_scr, u_scr,
                h_scr, carry_ref, *, lc, reverse):
    pitch = lc + SUBLANES
    slabs = RNN_WIDTH // LANES

    @pl.when(pl.program_id(1) == 0)
    def _():
        carry_ref[...] = jnp.zeros_like(carry_ref)

    lam = lam_ref[...]
    softplus_neg_lam = jnp.maximum(-lam, 0.0) + jnp.log1p(jnp.exp(-jnp.abs(lam)))
    half_rate = (-0.5 * LRU_C) * softplus_neg_lam
    for b in range(LRU_BATCH):
        xc = x_ref[b]
        xcb = xc.astype(BF16)
        th_r = jnp.tanh(jnp.dot(xcb, wa_ref[...], preferred_element_type=F32) + ba_ref[...])
        th_i = jnp.tanh(jnp.dot(xcb, wx_ref[...], preferred_element_type=F32) + bx_ref[...])
        log_a = th_r * half_rate + half_rate
        a = jnp.exp(log_a)
        z = jnp.tanh(log_a) * (-1.0 - a * a)
        root = z * lax.rsqrt(jnp.maximum(z, TINY))
        u = root * ((0.5 * th_i + 0.5) * xc)
        for j in range(slabs):
            a_scr[j, b * pitch:b * pitch + lc, :] = a[:, j * LANES:(j + 1) * LANES]
            u_scr[j, b * pitch:b * pitch + lc, :] = u[:, j * LANES:(j + 1) * LANES]

    def step(i, hs):
        t = lc - 1 - i if reverse else i
        out = []
        for j in range(slabs):
            rows = pl.ds(t, LRU_BATCH, stride=pitch)
            h = a_scr[j, rows, :] * hs[j] + u_scr[j, rows, :]
            h_scr[j, rows, :] = h
            out.append(h)
        return tuple(out)

    hs = lax.fori_loop(0, lc, step, tuple(carry_ref[j] for j in range(slabs)), unroll=LRU_UNROLL)
    for j in range(slabs):
        carry_ref[j] = hs[j]
        for b in range(LRU_BATCH):
            lanes = slice(j * LANES, (j + 1) * LANES)
            h = h_scr[j, b * pitch:b * pitch + lc, :]
            h_ref[b, :, lanes] = h if other_ref is None else other_ref[b, :, lanes] + h


def _lru(xc, wa, wx, ba, bx, lam, *, reverse, other=None):
    b, l, _ = xc.shape
    lc = LRU_CHUNK
    assert b % LRU_BATCH == 0 and l % lc == 0
    nc = l // lc
    pos = (lambda c: nc - 1 - c) if reverse else (lambda c: c)
    const = lambda bi, c: (0, 0)
    vec = pl.BlockSpec((1, RNN_WIDTH), const)
    mat = pl.BlockSpec((RNN_WIDTH, RNN_WIDTH), const)
    chunk = pl.BlockSpec((LRU_BATCH, lc, RNN_WIDTH), lambda bi, c: (bi, pos(c), 0))
    slabs = RNN_WIDTH // LANES
    scratch = pltpu.VMEM((slabs, LRU_BATCH * (lc + SUBLANES), LANES), F32)
    body = functools.partial(_lru_kernel, lc=lc, reverse=reverse)
    if other is None:
        kernel_fn = lambda x_ref, *refs: body(x_ref, None, *refs)
        operands, specs = (xc,), [chunk]
    else:
        kernel_fn = body
        operands, specs = (xc, other), [chunk, chunk]
    return pl.pallas_call(
        kernel_fn,
        grid=(b // LRU_BATCH, nc),
        in_specs=specs + [mat, mat, vec, vec, vec],
        out_specs=chunk,
        out_shape=jax.ShapeDtypeStruct((b, l, RNN_WIDTH), F32),
        scratch_shapes=[scratch, scratch, scratch, pltpu.VMEM((slabs, LRU_BATCH, LANES), F32)],
        compiler_params=_params("parallel", "arbitrary"),
        name="rg_lru_bwd" if reverse else "rg_lru_fwd",
    )(*operands, wa, wx, ba, bx, lam)


def _block_diag(w):
    eye = jnp.eye(RNN_BLOCKS, dtype=w.dtype)
    return jnp.einsum('hij,hk->hikj', w, eye).reshape(RNN_WIDTH, RNN_WIDTH)


def _mix_out_kernel(attn_ref, hsum_ref, gate_ref, x_ref, wo_ref, rg_ref, g1_ref, b1_ref,
                    wr_ref, h_ref, aff_ref):
    gt = gate_ref[...]
    gelu = 0.5 * gt * (1.0 + jnp.tanh(math.sqrt(2.0 / math.pi) * (gt + 0.044715 * (gt * gt * gt))))
    yr = hsum_ref[...] * gelu
    ms = jnp.mean(yr * yr, axis=-1, keepdims=True)
    yn = (yr * lax.rsqrt(ms + RMS_EPS) * rg_ref[...]).astype(BF16)
    mix = jnp.dot(attn_ref[...], wo_ref[:ATTN_WIDTH, :], preferred_element_type=F32)
    mix = mix + jnp.dot(yn, wo_ref[ATTN_WIDTH:, :], preferred_element_type=F32)
    z = ALPHA * x_ref[...] + mix
    mu = jnp.mean(z, axis=-1, keepdims=True)
    zc = z - mu
    var = jnp.mean(zc * zc, axis=-1, keepdims=True)
    h = zc * lax.rsqrt(var + LN_EPS) * g1_ref[...] + b1_ref[...]
    h_ref[...] = h
    logits = lax.dot_general(wr_ref[...], h.astype(BF16), (((1,), (1,)), ((), ())),
                             preferred_element_type=F32)
    mx = jnp.max(logits, axis=0, keepdims=True)
    e = jnp.exp(logits - mx)
    aff_ref[...] = e / jnp.sum(e, axis=0, keepdims=True)


def _mix_out(attn, hsum, gate, x2, wo, rnn_g, g1, b1, wr_t):
    t = x2.shape[0]
    row = lambda w: pl.BlockSpec((ROW_TILE, w), lambda i: (i, 0))
    const = lambda i: (0, 0)
    return pl.pallas_call(
        _mix_out_kernel,
        grid=(t // ROW_TILE,),
        in_specs=[
            row(ATTN_WIDTH), row(RNN_WIDTH), row(RNN_WIDTH), row(D_MODEL),
            pl.BlockSpec((D_MODEL, D_MODEL), const),
            pl.BlockSpec((1, RNN_WIDTH), const),
            pl.BlockSpec((1, D_MODEL), const),
            pl.BlockSpec((1, D_MODEL), const),
            pl.BlockSpec((N_EXPERTS, D_MODEL), const),
        ],
        out_specs=[row(D_MODEL), pl.BlockSpec((N_EXPERTS, ROW_TILE), lambda i: (0, i))],
        out_shape=[
            jax.ShapeDtypeStruct((t, D_MODEL), F32),
            jax.ShapeDtypeStruct((N_EXPERTS, t), F32),
        ],
        compiler_params=_params("parallel"),
        name="mix_out_ln1_router",
    )(attn, hsum, gate, x2, wo, rnn_g, g1, b1, wr_t)


def _pack_rows(rows):
    half = D_MODEL // 2
    lo = pltpu.bitcast(rows[:, :half].astype(BF16).astype(F32), jnp.uint32)
    hi = pltpu.bitcast(rows[:, half:].astype(BF16).astype(F32), jnp.uint32)
    return hi | (lo >> 16)


def _unpack_rows(words):
    lo = pltpu.bitcast(words << 16, F32)
    hi = pltpu.bitcast(words & jnp.uint32(0xFFFF0000), F32)
    return jnp.concatenate([lo, hi], axis=1).astype(BF16)


def _ffn_kernel(x_ref, wg_ref, wu_ref, wd_ref, o_ref):
    x = _unpack_rows(jnp.concatenate(
        [x_ref[pl.ds(j, FFN_ROWS, stride=PACKED_SUB), :] for j in range(PACKED_SUB)], axis=1))
    acc = jnp.zeros((FFN_ROWS, D_MODEL), F32)
    for f in range(EXPERT_FF // FFN_FCHUNK):
        cols = slice(f * FFN_FCHUNK, (f + 1) * FFN_FCHUNK)
        gt = jnp.dot(x, wg_ref[0, :, cols], preferred_element_type=F32)
        up = jnp.dot(x, wu_ref[0, :, cols], preferred_element_type=F32)
        hid = (gt * jax.nn.sigmoid(gt) * up).astype(BF16)
        acc = acc + jnp.dot(hid, wd_ref[0, cols, :], preferred_element_type=F32)
    words = _pack_rows(acc)
    for j in range(PACKED_SUB):
        o_ref[pl.ds(j, FFN_ROWS, stride=PACKED_SUB), :] = words[:, j * LANES:(j + 1) * LANES]


def _ffn(xe, wg, wu, wd, cap):
    e = wg.shape[0]
    per_e = cap // FFN_ROWS
    slots = pl.BlockSpec((FFN_ROWS * PACKED_SUB, LANES), lambda ei, j: (ei * per_e + j, 0))
    wspec = lambda a, b: pl.BlockSpec((1, a, b), lambda ei, j: (ei, 0, 0))
    return pl.pallas_call(
        _ffn_kernel,
        grid=(e, per_e),
        in_specs=[slots, wspec(D_MODEL, EXPERT_FF), wspec(D_MODEL, EXPERT_FF),
                  wspec(EXPERT_FF, D_MODEL)],
        out_specs=slots,
        out_shape=jax.ShapeDtypeStruct(xe.shape, jnp.uint32),
        compiler_params=_params("parallel", "arbitrary"),
        name="expert_ffn",
    )(xe, wg, wu, wd)


def _threshold_kernel(aff_ref, thr_ref, need_ref, *, cap):
    def count(mask):
        return jnp.sum(jnp.where(mask, 1.0, 0.0), axis=1, keepdims=True)

    def body(b, thr):
        cand = thr | jnp.left_shift(jnp.int32(1), 30 - b)
        bits = pltpu.bitcast(aff_ref[...], jnp.int32)
        return jnp.where(count(bits >= cand) >= cap, cand, thr)

    thr = lax.fori_loop(0, 31, body, jnp.zeros((N_EXPERTS, 1), jnp.int32))
    bits = pltpu.bitcast(aff_ref[...], jnp.int32)
    thr_ref[...] = thr
    need_ref[...] = cap - count(bits > thr)


def _positions_kernel(aff_ref, thr_ref, need_ref, pos_ref, tie_run_ref, sel_run_ref, *, block):
    @pl.when(pl.program_id(0) == 0)
    def _():
        tie_run_ref[...] = jnp.zeros_like(tie_run_ref)
        sel_run_ref[...] = jnp.zeros_like(sel_run_ref)

    thr = thr_ref[...]
    need = need_ref[...]
    r = lax.broadcasted_iota(jnp.int32, (LANES, LANES), 0)
    c = lax.broadcasted_iota(jnp.int32, (LANES, LANES), 1)
    before = jnp.where(r < c, 1.0, 0.0).astype(BF16)
    tie_run = tie_run_ref[...]
    sel_run = sel_run_ref[...]
    for s in range(block // LANES):
        lanes = slice(s * LANES, (s + 1) * LANES)
        bits = pltpu.bitcast(aff_ref[:, lanes], jnp.int32)
        tie = jnp.where(bits == thr, 1.0, 0.0)
        tie_before = jnp.dot(tie.astype(BF16), before, preferred_element_type=F32) + tie_run
        sel = (bits > thr) | ((bits == thr) & (tie_before < need))
        picked = jnp.where(sel, 1.0, 0.0)
        sel_before = jnp.dot(picked.astype(BF16), before, preferred_element_type=F32) + sel_run
        pos_ref[:, lanes] = jnp.where(sel, sel_before.astype(jnp.int32), -1)
        tie_run = tie_run + jnp.sum(tie, axis=1, keepdims=True)
        sel_run = sel_run + jnp.sum(picked, axis=1, keepdims=True)
    tie_run_ref[...] = tie_run
    sel_run_ref[...] = sel_run


def _route(aff_t, cap):
    e, t = aff_t.shape
    col = jax.ShapeDtypeStruct((e, 1), jnp.int32)
    thr, need = pl.pallas_call(
        functools.partial(_threshold_kernel, cap=cap),
        out_shape=[col, jax.ShapeDtypeStruct((e, 1), F32)],
        compiler_params=_params(),
        name="route_threshold",
    )(aff_t)
    block = min(POS_BLOCK, t)
    cspec = pl.BlockSpec((e, 1), lambda i: (0, 0))
    return pl.pallas_call(
        functools.partial(_positions_kernel, block=block),
        grid=(t // block,),
        in_specs=[pl.BlockSpec((e, block), lambda i: (0, i)), cspec, cspec],
        out_specs=pl.BlockSpec((e, block), lambda i: (0, i)),
        out_shape=jax.ShapeDtypeStruct((e, t), jnp.int32),
        scratch_shapes=[pltpu.VMEM((e, 1), F32), pltpu.VMEM((e, 1), F32)],
        compiler_params=_params("arbitrary"),
        name="route_positions",
    )(aff_t, thr, need)


def _tile_tables(pos, tile):
    e, t = pos.shape
    ntile = t // tile
    n = jnp.sum((pos >= 0).reshape(e, ntile, tile), axis=-1, dtype=jnp.int32)
    s0 = jnp.cumsum(n, axis=1) - n
    end = s0 + n
    s0a = s0 - s0 % 2
    per_tile = lambda a: a.T.reshape(ntile, e, 1)

    wr = end - end % 2 - s0a
    wr_off = jnp.cumsum(wr, axis=0) - wr
    carried = s0 % 2 == 1
    carry_row = jnp.where(carried & (n > 0), wr_off, -1)
    lanes = jnp.full((ntile, 1, LANES), -1, jnp.int32).at[:, 0, :e].set(carry_row.T)
    dispatch = dict(
        count=wr.reshape(-1), start=s0a.reshape(-1), shift=per_tile(wr_off - s0a),
        end=per_tile(end), dangling=per_tile((end % 2 == 1).astype(jnp.int32)),
        keep=per_tile((carried & (n == 0)).astype(jnp.int32)), carry_row=lanes)

    rd = jnp.where(n > 0, (end + 1) // 2 * 2 - s0a, 0)
    rd_off = jnp.cumsum(rd, axis=0) - rd
    combine = dict(count=rd.reshape(-1), start=s0a.reshape(-1), shift=per_tile(rd_off - s0a))
    return dispatch, combine


def _tile_counts(n_ref, tile, ntile):
    counts = [n_ref[e * ntile + tile] for e in range(N_EXPERTS)]
    offs = [jnp.int32(0)]
    for c in counts:
        offs.append(offs[-1] + c)
    return counts, offs


def _one_hot_t(stagepos, kbase, values=None):
    rel = stagepos - kbase
    rel = jnp.where((rel >= 0) & (rel < STAGE_CHUNK), rel, -1).astype(F32).astype(BF16)
    k = lax.broadcasted_iota(jnp.int32, (STAGE_CHUNK, stagepos.shape[1]), 0).astype(F32).astype(BF16)
    one = jnp.ones((1, stagepos.shape[1]), BF16)
    pt = jnp.zeros(k.shape, BF16)
    for e in range(N_EXPERTS):
        hit = rel[e:e + 1, :] == k
        pt = jnp.where(hit, one if values is None else values[e:e + 1, :], pt)
    return pt


def _stage_rows(e, tile, extra=0):
    chunks = max(STATIC_CHUNKS, -(-(e * tile + extra) // STAGE_CHUNK))
    return chunks * STAGE_CHUNK


def _for_each_chunk(total, body, init):
    carry = init
    for c in range(STATIC_CHUNKS):
        carry = body(c * STAGE_CHUNK, carry)
    return lax.fori_loop(
        STATIC_CHUNKS, (total + STAGE_CHUNK - 1) // STAGE_CHUNK,
        lambda c, carry: body(pl.multiple_of(c * STAGE_CHUNK, STAGE_CHUNK), carry), carry)


def _pairs_copy(src, dst, sem, src_row, dst_row, n):
    return pltpu.make_async_copy(
        src.at[pl.ds(pl.multiple_of(src_row * PACKED_SUB, SUBLANES), n // 2 * SUBLANES)],
        dst.at[pl.ds(pl.multiple_of(dst_row * PACKED_SUB, SUBLANES), n // 2 * SUBLANES)], sem)


def _dispatch_kernel(n_ref, s0_ref, pos_ref, shift_ref, end_ref, dangling_ref, keep_ref,
                     carry_row_ref, h_ref, xe_hbm, stage_ref, carry_ref, sem, *, ntile, cap):
    i = pl.program_id(0)
    slot = i % 2

    def wait_writes(tile, slot):
        _, offs = _tile_counts(n_ref, tile, ntile)

        @pl.when(offs[-1] > 0)
        def _():
            _pairs_copy(stage_ref.at[slot], xe_hbm, sem.at[slot], 0, 0, offs[-1]).wait()

    @pl.when(i >= 2)
    def _():
        wait_writes(i - 2, slot)

    @pl.when(i == 0)
    def _():
        carry_ref[...] = jnp.zeros_like(carry_ref)

    counts, offs = _tile_counts(n_ref, i, ntile)
    pos = pos_ref[...]
    h = h_ref[...].astype(BF16)
    carry = carry_ref[...]
    trailing = (pos == end_ref[0] - 1) & (dangling_ref[0] > 0)
    stagepos = jnp.where((pos >= 0) & ~trailing, pos + shift_ref[0], -1)
    carry_row = carry_row_ref[0]

    def chunk(kbase, acc):
        k = lax.broadcasted_iota(jnp.int32, (STAGE_CHUNK, LANES), 0) + kbase
        place_carry = jnp.where(carry_row == k, 1.0, 0.0).astype(BF16)
        rows = jnp.dot(_one_hot_t(stagepos, kbase), h, preferred_element_type=F32)
        rows = rows + jnp.dot(place_carry, carry, preferred_element_type=F32)
        words = _pack_rows(rows)
        for j in range(PACKED_SUB):
            stage_ref[slot, pl.ds(kbase * PACKED_SUB + j, STAGE_CHUNK, stride=PACKED_SUB), :] = (
                words[:, j * LANES:(j + 1) * LANES])
        return acc

    _for_each_chunk(offs[-1], chunk, 0)

    er = lax.broadcasted_iota(jnp.int32, (N_EXPERTS, LANES), 0)
    ec = lax.broadcasted_iota(jnp.int32, (N_EXPERTS, LANES), 1)
    keep = jnp.where((er == ec) & (keep_ref[0] > 0), 1.0, 0.0).astype(BF16)
    take = jnp.where(trailing, 1.0, 0.0).astype(BF16)
    new_carry = (jnp.dot(take, h, preferred_element_type=F32)
                 + jnp.dot(keep, carry, preferred_element_type=F32))
    carry_ref[:N_EXPERTS, :] = new_carry.astype(BF16)

    for e in range(N_EXPERTS):
        @pl.when(counts[e] > 0)
        def _(e=e):
            _pairs_copy(stage_ref.at[slot], xe_hbm, sem.at[slot], offs[e],
                        e * cap + s0_ref[e * ntile + i], counts[e]).start()

    @pl.when(i == ntile - 1)
    def _():
        wait_writes(i, slot)
        if ntile > 1:
            wait_writes(i - 1, 1 - slot)


def _dispatch(tab, pos, h1, cap):
    e, t = pos.shape
    tile = ROUTE_TILE
    ntile = t // tile
    assert cap % 2 == 0 and e % (2 * SUBLANES) == 0
    col = pl.BlockSpec((1, e, 1), lambda i, *_: (i, 0, 0))
    return pl.pallas_call(
        functools.partial(_dispatch_kernel, ntile=ntile, cap=cap),
        grid_spec=pltpu.PrefetchScalarGridSpec(
            num_scalar_prefetch=2,
            grid=(ntile,),
            in_specs=[
                pl.BlockSpec((e, tile), lambda i, *_: (0, i)),
                col, col, col, col,
                pl.BlockSpec((1, 1, LANES), lambda i, *_: (i, 0, 0)),
                pl.BlockSpec((tile, D_MODEL), lambda i, *_: (i, 0)),
            ],
            out_specs=pl.BlockSpec(memory_space=pl.ANY),
            scratch_shapes=[pltpu.VMEM((2, _stage_rows(e, tile) * PACKED_SUB, LANES), jnp.uint32),
                            pltpu.VMEM((LANES, D_MODEL), BF16),
                            pltpu.SemaphoreType.DMA((2,))],
        ),
        out_shape=jax.ShapeDtypeStruct((e * cap * PACKED_SUB, LANES), jnp.uint32),
        compiler_params=_params("arbitrary"),
        name="route_dispatch",
    )(tab["count"], tab["start"], pos, tab["shift"], tab["end"], tab["dangling"], tab["keep"],
      tab["carry_row"], h1)


def _combine_kernel(n_ref, s0_ref, pos_ref, aff_ref, shift_ref, h_ref, g_ref, b_ref, ye_hbm, o_ref,
                    stage_ref, sem, *, ntile, cap):
    i = pl.program_id(0)
    slot = i % 2

    def fetch(tile, slot):
        counts, offs = _tile_counts(n_ref, tile, ntile)
        for e in range(N_EXPERTS):
            @pl.when(counts[e] > 0)
            def _(e=e):
                _pairs_copy(ye_hbm, stage_ref.at[slot], sem.at[slot],
                            e * cap + s0_ref[e * ntile + tile], offs[e], counts[e]).start()

    @pl.when(i == 0)
    def _():
        stage_ref[...] = jnp.zeros_like(stage_ref)
        fetch(0, 0)

    @pl.when(i + 1 < ntile)
    def _():
        fetch(i + 1, 1 - slot)

    _, offs = _tile_counts(n_ref, i, ntile)

    @pl.when(offs[-1] > 0)
    def _():
        _pairs_copy(ye_hbm, stage_ref.at[slot], sem.at[slot], 0, 0, offs[-1]).wait()

    pos = pos_ref[...]
    stagepos = jnp.where(pos >= 0, pos + shift_ref[0], -1)
    gates = aff_ref[...].astype(BF16)

    def chunk(kbase, acc):
        ye = _unpack_rows(jnp.concatenate(
            [stage_ref[slot, pl.ds(kbase * PACKED_SUB + j, STAGE_CHUNK, stride=PACKED_SUB), :]
             for j in range(PACKED_SUB)], axis=1))
        return acc + lax.dot_general(_one_hot_t(stagepos, kbase, gates), ye,
                                     (((0,), (0,)), ((), ())), preferred_element_type=F32)

    z = _for_each_chunk(offs[-1], chunk, ALPHA * h_ref[...])
    mu = jnp.mean(z, axis=-1, keepdims=True)
    zc = z - mu
    var = jnp.mean(zc * zc, axis=-1, keepdims=True)
    o_ref[...] = zc * lax.rsqrt(var + LN_EPS) * g_ref[...] + b_ref[...]


def _combine(tab, pos, aff_t, h1, g2, b2, ye, cap):
    e, t = pos.shape
    tile = ROUTE_TILE
    ntile = t // tile
    lanes = pl.BlockSpec((e, tile), lambda i, *_: (0, i))
    vec = pl.BlockSpec((1, D_MODEL), lambda i, *_: (0, 0))
    row = pl.BlockSpec((tile, D_MODEL), lambda i, *_: (i, 0))
    return pl.pallas_call(
        functools.partial(_combine_kernel, ntile=ntile, cap=cap),
        grid_spec=pltpu.PrefetchScalarGridSpec(
            num_scalar_prefetch=2,
            grid=(ntile,),
            in_specs=[lanes, lanes, pl.BlockSpec((1, e, 1), lambda i, *_: (i, 0, 0)), row, vec, vec,
                      pl.BlockSpec(memory_space=pl.ANY)],
            out_specs=row,
            scratch_shapes=[pltpu.VMEM((2, _stage_rows(e, tile, 2 * e) * PACKED_SUB, LANES),
                                       jnp.uint32),
                            pltpu.SemaphoreType.DMA((2,))],
        ),
        out_shape=jax.ShapeDtypeStruct((t, D_MODEL), F32),
        compiler_params=_params("arbitrary"),
        name="route_combine_ln2",
    )(tab["count"], tab["start"], pos, aff_t, tab["shift"], h1, g2, b2, ye)


def _layer(x, p):
    b, l, _ = x.shape
    t = b * l
    cap = CAPACITY_FACTOR * t // N_EXPERTS
    x2 = x.reshape(t, D_MODEL)
    q, k, v, xc, gate = _in_proj(x2, p["w_in"], p["conv_w"], p["conv_b"], l)
    attn = _attention(q.reshape(b, l, -1), k.reshape(b, l, -1), v.reshape(b, l, -1),
                      p["bias"], p["sink_row"], p["attn_g_col"])
    xc3 = xc.reshape(b, l, RNN_WIDTH)
    gates = lambda d: (p["wa"][d], p["wx"][d], p["ba"][d], p["bx"][d], p["lam"][d])
    h_fwd = _lru(xc3, *gates(0), reverse=False)
    hsum = _lru(xc3, *gates(1), reverse=True, other=h_fwd)
    h1, aff_t = _mix_out(attn.reshape(t, -1), hsum.reshape(t, -1), gate, x2, p["w_out"], p["rnn_g"],
                         p["ln1_g"], p["ln1_b"], p["wr_t"])
    pos = _route(aff_t, cap)
    dispatch_tab, combine_tab = _tile_tables(pos, ROUTE_TILE)
    xe = _dispatch(dispatch_tab, pos, h1, cap)
    ye = _ffn(xe, p["wg"], p["wu"], p["wd"], cap)
    out = _combine(combine_tab, pos, aff_t, h1, p["ln2_g"], p["ln2_b"], ye, cap)
    return out.reshape(b, l, D_MODEL)


def _layer_params(li, w_in, attn_sink, attn_norm_g, rnn_norm_g, conv_w, conv_b, lru_w_a, lru_b_a,
                  lru_w_x, lru_b_x, lru_lambda, w_out, ln1_g, ln1_b, w_router, w_gate, w_up, w_down,
                  ln2_g, ln2_b):
    vec = lambda a: a[li].reshape(1, -1)
    per_dir = lambda f: [f(d) for d in range(2)]
    return dict(
        w_in=w_in[li].astype(BF16),
        bias=_alibi_bias(),
        sink_row=jnp.repeat(attn_sink[li].astype(F32), BLOCK).reshape(N_KV_HEADS, 1, Q_PER_KV * BLOCK),
        attn_g_col=jnp.broadcast_to(attn_norm_g[li].astype(F32)[:, None], (ATTN_WIDTH, BLOCK)),
        rnn_g=vec(rnn_norm_g),
        conv_w=conv_w[li], conv_b=vec(conv_b),
        wa=per_dir(lambda d: (0.5 * _block_diag(lru_w_a[li, d])).astype(BF16)),
        wx=per_dir(lambda d: (0.5 * _block_diag(lru_w_x[li, d])).astype(BF16)),
        ba=per_dir(lambda d: 0.5 * lru_b_a[li, d].reshape(1, -1)),
        bx=per_dir(lambda d: 0.5 * lru_b_x[li, d].reshape(1, -1)),
        lam=per_dir(lambda d: lru_lambda[li, d].reshape(1, -1)),
        w_out=w_out[li].astype(BF16),
        ln1_g=vec(ln1_g), ln1_b=vec(ln1_b),
        wr_t=w_router[li].T.astype(BF16),
        wg=w_gate[li].astype(BF16), wu=w_up[li].astype(BF16), wd=w_down[li].astype(BF16),
        ln2_g=vec(ln2_g), ln2_b=vec(ln2_b),
    )


def kernel(x_prompt, x_sample, w_in, attn_sink, attn_norm_g, rnn_norm_g, conv_w, conv_b, lru_w_a,
           lru_b_a, lru_w_x, lru_b_x, lru_lambda, w_out, ln1_g, ln1_b, w_router, w_gate, w_up,
           w_down, ln2_g, ln2_b):
    layers = [
        _layer_params(li, w_in, attn_sink, attn_norm_g, rnn_norm_g, conv_w, conv_b, lru_w_a, lru_b_a,
                      lru_w_x, lru_b_x, lru_lambda, w_out, ln1_g, ln1_b, w_router, w_gate, w_up,
                      w_down, ln2_g, ln2_b)
        for li in range(w_in.shape[0])]
    ys = []
    for x in (x_prompt, x_sample):
        for p in layers:
            x = _layer(x, p)
        ys.append(x)
    return tuple(ys)
```

```python
import functools
import math

import jax
import jax.numpy as jnp
from jax import lax
from jax.experimental import pallas as pl
from jax.experimental.pallas import tpu as pltpu

D_MODEL = 1024
HEAD_DIM = 64
N_Q_HEADS = 8
N_KV_HEADS = 2
Q_PER_KV = N_Q_HEADS // N_KV_HEADS
ATTN_WIDTH = N_Q_HEADS * HEAD_DIM
KV_WIDTH = N_KV_HEADS * HEAD_DIM
BLOCK = 128
RNN_WIDTH = 512
RNN_BLOCKS = 8
RNN_BLOCK_W = RNN_WIDTH // RNN_BLOCKS
CONV_WIDTH = 4
LRU_C = 8.0
N_EXPERTS = 16
EXPERT_FF = 2048
CAPACITY_FACTOR = 2
ALPHA = 2.0 ** 0.25
LN_EPS = 1e-5
RMS_EPS = 1e-6
MASKED = -1e30
TINY = 1e-37

SUBLANES = 8
LANES = 128
VMEM_LIMIT = 56 * 1024 * 1024

ROW_TILE = 512
HALO = 2 * SUBLANES
ATTN_QBLOCKS = 16
LRU_CHUNK = 256
LRU_BATCH = SUBLANES
LRU_UNROLL = 8
FFN_ROWS = 512
FFN_FCHUNK = 256
POS_BLOCK = 2048
ROUTE_TILE = 256
STAGE_CHUNK = 192
STATIC_CHUNKS = 3

BF16 = jnp.bfloat16
F32 = jnp.float32


def _params(*sem):
    return pltpu.CompilerParams(dimension_semantics=sem, vmem_limit_bytes=VMEM_LIMIT)


def _in_proj_kernel(x_ref, xp_ref, xn_ref, w_ref, cw_ref, cb_ref, q_ref, k_ref, v_ref, xc_ref, gate_ref,
                    *, tiles_per_seq):
    i = pl.program_id(0)
    xb = x_ref[...].astype(BF16)

    def proj(lo, hi):
        return jnp.dot(xb, w_ref[:, lo:hi], preferred_element_type=F32)

    q_lo, k_lo, v_lo = 0, ATTN_WIDTH, ATTN_WIDTH + KV_WIDTH
    r_lo = v_lo + KV_WIDTH
    g_lo = r_lo + RNN_WIDTH

    ext_b = jnp.concatenate([xp_ref[...].astype(BF16), xb, xn_ref[...].astype(BF16)], axis=0)
    ext = jnp.dot(ext_b, w_ref[:, r_lo:g_lo], preferred_element_type=F32)
    first = i % tiles_per_seq == 0
    last = i % tiles_per_seq == tiles_per_seq - 1
    n = ROW_TILE + 2 * HALO
    ext = jnp.concatenate([jnp.where(first, 0.0, ext[:HALO]), ext[HALO:HALO + ROW_TILE],
                           jnp.where(last, 0.0, ext[HALO + ROW_TILE:])], axis=0)
    mid = slice(HALO, HALO + ROW_TILE)
    w = cw_ref[...]
    xc = pltpu.roll(ext, 2, 0)[mid] * w[0:1] + pltpu.roll(ext, 1, 0)[mid] * w[1:2]
    xc = xc + ext[mid] * w[2:3]
    xc = xc + pltpu.roll(ext, n - 1, 0)[mid] * w[3:4]
    xc_ref[...] = xc + cb_ref[...]

    q_ref[...] = (proj(q_lo, k_lo) * (HEAD_DIM ** -0.5)).astype(BF16)
    k_ref[...] = proj(k_lo, v_lo).astype(BF16)
    v_ref[...] = proj(v_lo, r_lo).astype(BF16)
    gate_ref[...] = proj(g_lo, g_lo + RNN_WIDTH)


def _in_proj(x2, w_bf16, conv_w, conv_b, seq_len):
    t = x2.shape[0]
    in_w = w_bf16.shape[1]
    assert seq_len % ROW_TILE == 0 and ROW_TILE % HALO == 0
    per = ROW_TILE // HALO
    row = lambda w: pl.BlockSpec((ROW_TILE, w), lambda i: (i, 0))
    const = lambda i: (0, 0)
    return pl.pallas_call(
        functools.partial(_in_proj_kernel, tiles_per_seq=seq_len // ROW_TILE),
        grid=(t // ROW_TILE,),
        in_specs=[
            row(D_MODEL),
            pl.BlockSpec((HALO, D_MODEL), lambda i: (jnp.maximum(i * per - 1, 0), 0)),
            pl.BlockSpec((HALO, D_MODEL), lambda i: (jnp.minimum((i + 1) * per, t // HALO - 1), 0)),
            pl.BlockSpec((D_MODEL, in_w), const),
            pl.BlockSpec((CONV_WIDTH, RNN_WIDTH), const),
            pl.BlockSpec((1, RNN_WIDTH), const),
        ],
        out_specs=[row(ATTN_WIDTH), row(KV_WIDTH), row(KV_WIDTH), row(RNN_WIDTH), row(RNN_WIDTH)],
        out_shape=[
            jax.ShapeDtypeStruct((t, ATTN_WIDTH), BF16),
            jax.ShapeDtypeStruct((t, KV_WIDTH), BF16),
            jax.ShapeDtypeStruct((t, KV_WIDTH), BF16),
            jax.ShapeDtypeStruct((t, RNN_WIDTH), F32),
            jax.ShapeDtypeStruct((t, RNN_WIDTH), F32),
        ],
        compiler_params=_params("parallel"),
        name="in_proj",
    )(x2, x2, x2, w_bf16, conv_w, conv_b)


def _attn_kernel(q_ref, kp_ref, kc_ref, kn_ref, vp_ref, vc_ref, vn_ref, bias_first_ref, bias_mid_ref,
                 bias_last_ref, sink_ref, g_ref, o_ref):
    k = jnp.concatenate([kp_ref[0], kc_ref[0], kn_ref[0]], axis=0)
    v = jnp.concatenate([vp_ref[0], vc_ref[0], vn_ref[0]], axis=0)
    v_t = v.astype(F32).T
    pad_rows = 2 * SUBLANES
    ones_row = jnp.where(
        lax.broadcasted_iota(jnp.int32, (pad_rows, 3 * BLOCK), 0) == 0, 1.0, 0.0)
    for qb in range(ATTN_QBLOCKS):
        q = q_ref[0, qb * BLOCK:(qb + 1) * BLOCK, :]
        keys = slice(qb * BLOCK, (qb + 3) * BLOCK)
        bias_ref = (bias_first_ref if qb == 0 else
                    bias_last_ref if qb == ATTN_QBLOCKS - 1 else bias_mid_ref)
        heads = []
        for g in range(N_KV_HEADS):
            qs = jnp.concatenate(
                [q[:, (Q_PER_KV * g + j) * HEAD_DIM:(Q_PER_KV * g + j + 1) * HEAD_DIM]
                 for j in range(Q_PER_KV)], axis=0)
            kg = k[keys, g * HEAD_DIM:(g + 1) * HEAD_DIM]
            s = lax.dot_general(kg, qs, (((1,), (1,)), ((), ())), preferred_element_type=F32)
            s = s + bias_ref[0, g]
            sink = sink_ref[g]
            m = jnp.maximum(jnp.max(s, axis=0, keepdims=True), sink)
            p = jnp.exp(s - m).astype(BF16)
            lhs = jnp.concatenate([v_t[g * HEAD_DIM:(g + 1) * HEAD_DIM, keys], ones_row],
                                  axis=0).astype(BF16)
            o_aug = jnp.dot(lhs, p, preferred_element_type=F32)
            denom = o_aug[HEAD_DIM:HEAD_DIM + 1] + jnp.exp(sink - m)
            o = o_aug[:HEAD_DIM] * (1.0 / denom)
            heads += [o[:, j * BLOCK:(j + 1) * BLOCK] for j in range(Q_PER_KV)]
        y_t = jnp.concatenate(heads, axis=0)
        ms = jnp.mean(y_t * y_t, axis=0, keepdims=True)
        o_ref[0, qb * BLOCK:(qb + 1) * BLOCK, :] = (
            y_t * lax.rsqrt(ms + RMS_EPS) * g_ref[...]).T.astype(BF16)


def _attention(q, k, v, bias, sink_row, attn_g_col):
    b, l, _ = q.shape
    span = ATTN_QBLOCKS * BLOCK
    assert ATTN_QBLOCKS >= 2 and l % span == 0
    ns = l // span
    nb = l // BLOCK
    edge = lambda f: pl.BlockSpec((1, BLOCK, KV_WIDTH), f)
    prev = lambda bi, i: (bi, jnp.maximum(i * ATTN_QBLOCKS - 1, 0), 0)
    cur = lambda bi, i: (bi, i, 0)
    nxt = lambda bi, i: (bi, jnp.minimum((i + 1) * ATTN_QBLOCKS, nb - 1), 0)
    mid = pl.BlockSpec((1, span, KV_WIDTH), cur)
    bias_spec = lambda f: pl.BlockSpec((1,) + bias.shape[1:], f)
    first = lambda bi, i: ((i == 0).astype(jnp.int32), 0, 0, 0)
    last = lambda bi, i: (2 * (i == ns - 1).astype(jnp.int32), 0, 0, 0)
    return pl.pallas_call(
        _attn_kernel,
        grid=(b, ns),
        in_specs=[
            pl.BlockSpec((1, span, ATTN_WIDTH), cur),
            edge(prev), mid, edge(nxt),
            edge(prev), mid, edge(nxt),
            bias_spec(first), bias_spec(lambda bi, i: (0, 0, 0, 0)), bias_spec(last),
            pl.BlockSpec(sink_row.shape, lambda bi, i: (0, 0, 0)),
            pl.BlockSpec(attn_g_col.shape, lambda bi, i: (0, 0)),
        ],
        out_specs=pl.BlockSpec((1, span, ATTN_WIDTH), cur),
        out_shape=jax.ShapeDtypeStruct((b, l, ATTN_WIDTH), BF16),
        compiler_params=_params("parallel", "parallel"),
        name="banded_attention",
    )(q, k, k, k, v, v, v, bias, bias, bias, sink_row, attn_g_col)


def _alibi_bias():
    qi = jnp.arange(BLOCK)[None, :]
    sj = jnp.arange(3 * BLOCK)[:, None]
    rel = sj - BLOCK - qi
    dist = jnp.abs(rel).astype(F32)
    slopes = jnp.asarray([2.0 ** (-8.0 * (h + 1) / N_Q_HEADS) for h in range(N_Q_HEADS)], F32)
    bias = jnp.where((jnp.abs(rel) <= BLOCK)[None], -slopes[:, None, None] * dist[None], MASKED)
    bias = bias.reshape(N_KV_HEADS, Q_PER_KV, 3 * BLOCK, BLOCK).transpose(0, 2, 1, 3)
    bias = bias.reshape(N_KV_HEADS, 3 * BLOCK, Q_PER_KV * BLOCK)
    no_prev = (sj < BLOCK)[None]
    no_next = (sj >= 2 * BLOCK)[None]
    return jnp.stack([
        bias,
        jnp.where(no_prev, MASKED, bias),
        jnp.where(no_next, MASKED, bias),
    ])


def _lru_kernel(x_ref, wa_ref, wx_ref, ba_ref, bx_ref, lam_ref, h_ref, a_scr, u_scr, h_scr,
                carry_ref, *, lc, reverse):
    pitch = lc + SUBLANES
    slabs = RNN_WIDTH // LANES

    @pl.when(pl.program_id(1) == 0)
    def _():
        carry_ref[...] = jnp.zeros_like(carry_ref)

    lam = lam_ref[...]
    softplus_neg_lam = jnp.maximum(-lam, 0.0) + jnp.log1p(jnp.exp(-jnp.abs(lam)))
    half_rate = (-0.5 * LRU_C) * softplus_neg_lam
    for b in range(LRU_BATCH):
        xc = x_ref[b]
        xcb = xc.astype(BF16)
        th_r = jnp.tanh(jnp.dot(xcb, wa_ref[...], preferred_element_type=F32) + ba_ref[...])
        th_i = jnp.tanh(jnp.dot(xcb, wx_ref[...], preferred_element_type=F32) + bx_ref[...])
        log_a = th_r * half_rate + half_rate
        a = jnp.exp(log_a)
        z = jnp.tanh(log_a) * (-1.0 - a * a)
        root = z * lax.rsqrt(jnp.maximum(z, TINY))
        u = root * ((0.5 * th_i + 0.5) * xc)
        for j in range(slabs):
            a_scr[j, b * pitch:b * pitch + lc, :] = a[:, j * LANES:(j + 1) * LANES]
            u_scr[j, b * pitch:b * pitch + lc, :] = u[:, j * LANES:(j + 1) * LANES]

    def step(i, hs):
        t = lc - 1 - i if reverse else i
        out = []
        for j in range(slabs):
            rows = pl.ds(t, LRU_BATCH, stride=pitch)
            h = a_scr[j, rows, :] * hs[j] + u_scr[j, rows, :]
            h_scr[j, rows, :] = h
            out.append(h)
        return tuple(out)

    hs = lax.fori_loop(0, lc, step, tuple(carry_ref[j] for j in range(slabs)), unroll=LRU_UNROLL)
    for j in range(slabs):
        carry_ref[j] = hs[j]
        for b in range(LRU_BATCH):
            h_ref[b, :, j * LANES:(j + 1) * LANES] = h_scr[j, b * pitch:b * pitch + lc, :]


def _lru(xc, wa, wx, ba, bx, lam, *, reverse):
    b, l, _ = xc.shape
    lc = LRU_CHUNK
    assert b % LRU_BATCH == 0 and l % lc == 0
    nc = l // lc
    pos = (lambda c: nc - 1 - c) if reverse else (lambda c: c)
    const = lambda bi, c: (0, 0)
    vec = pl.BlockSpec((1, RNN_WIDTH), const)
    mat = pl.BlockSpec((RNN_WIDTH, RNN_WIDTH), const)
    chunk = pl.BlockSpec((LRU_BATCH, lc, RNN_WIDTH), lambda bi, c: (bi, pos(c), 0))
    slabs = RNN_WIDTH // LANES
    scratch = pltpu.VMEM((slabs, LRU_BATCH * (lc + SUBLANES), LANES), F32)
    return pl.pallas_call(
        functools.partial(_lru_kernel, lc=lc, reverse=reverse),
        grid=(b // LRU_BATCH, nc),
        in_specs=[chunk, mat, mat, vec, vec, vec],
        out_specs=chunk,
        out_shape=jax.ShapeDtypeStruct((b, l, RNN_WIDTH), F32),
        scratch_shapes=[scratch, scratch, scratch, pltpu.VMEM((slabs, LRU_BATCH, LANES), F32)],
        compiler_params=_params("parallel", "arbitrary"),
        name="rg_lru_bwd" if reverse else "rg_lru_fwd",
    )(xc, wa, wx, ba, bx, lam)


def _block_diag(w):
    eye = jnp.eye(RNN_BLOCKS, dtype=w.dtype)
    return jnp.einsum('hij,hk->hikj', w, eye).reshape(RNN_WIDTH, RNN_WIDTH)


def _mix_out_kernel(attn_ref, hf_ref, hb_ref, gate_ref, x_ref, wo_ref, rg_ref, g1_ref, b1_ref,
                    wr_ref, h_ref, hb16_ref, aff_ref):
    gt = gate_ref[...]
    gelu = 0.5 * gt * (1.0 + jnp.tanh(math.sqrt(2.0 / math.pi) * (gt + 0.044715 * (gt * gt * gt))))
    yr = (hf_ref[...] + hb_ref[...]) * gelu
    ms = jnp.mean(yr * yr, axis=-1, keepdims=True)
    yn = (yr * lax.rsqrt(ms + RMS_EPS) * rg_ref[...]).astype(BF16)
    mix = jnp.dot(attn_ref[...], wo_ref[:ATTN_WIDTH, :], preferred_element_type=F32)
    mix = mix + jnp.dot(yn, wo_ref[ATTN_WIDTH:, :], preferred_element_type=F32)
    z = ALPHA * x_ref[...] + mix
    mu = jnp.mean(z, axis=-1, keepdims=True)
    zc = z - mu
    var = jnp.mean(zc * zc, axis=-1, keepdims=True)
    h = zc * lax.rsqrt(var + LN_EPS) * g1_ref[...] + b1_ref[...]
    h_ref[...] = h
    hb = h.astype(BF16)
    hb16_ref[...] = hb
    logits = lax.dot_general(wr_ref[...], hb, (((1,), (1,)), ((), ())),
                             preferred_element_type=F32)
    mx = jnp.max(logits, axis=0, keepdims=True)
    e = jnp.exp(logits - mx)
    aff_ref[...] = e / jnp.sum(e, axis=0, keepdims=True)


def _mix_out(attn, hf, hb, gate, x2, wo, rnn_g, g1, b1, wr_t):
    t = x2.shape[0]
    row = lambda w: pl.BlockSpec((ROW_TILE, w), lambda i: (i, 0))
    const = lambda i: (0, 0)
    return pl.pallas_call(
        _mix_out_kernel,
        grid=(t // ROW_TILE,),
        in_specs=[
            row(ATTN_WIDTH), row(RNN_WIDTH), row(RNN_WIDTH), row(RNN_WIDTH), row(D_MODEL),
            pl.BlockSpec((D_MODEL, D_MODEL), const),
            pl.BlockSpec((1, RNN_WIDTH), const),
            pl.BlockSpec((1, D_MODEL), const),
            pl.BlockSpec((1, D_MODEL), const),
            pl.BlockSpec((N_EXPERTS, D_MODEL), const),
        ],
        out_specs=[row(D_MODEL), row(D_MODEL), pl.BlockSpec((N_EXPERTS, ROW_TILE), lambda i: (0, i))],
        out_shape=[
            jax.ShapeDtypeStruct((t, D_MODEL), F32),
            jax.ShapeDtypeStruct((t, D_MODEL), BF16),
            jax.ShapeDtypeStruct((N_EXPERTS, t), F32),
        ],
        compiler_params=_params("parallel"),
        name="mix_out_ln1_router",
    )(attn, hf, hb, gate, x2, wo, rnn_g, g1, b1, wr_t)


def _load_rows(ref, start, n, lead=()):
    return jnp.concatenate(
        [ref[lead + (pl.ds(start * SUBLANES + j, n, stride=SUBLANES), slice(None))]
         for j in range(D_MODEL // LANES)], axis=1)


def _store_rows(ref, start, rows, lead=()):
    n = rows.shape[0]
    for j in range(D_MODEL // LANES):
        ref[lead + (pl.ds(start * SUBLANES + j, n, stride=SUBLANES), slice(None))] = (
            rows[:, j * LANES:(j + 1) * LANES])


def _ffn_kernel(x_ref, wg_ref, wu_ref, wd_ref, o_ref):
    x = _load_rows(x_ref, 0, FFN_ROWS).astype(BF16)
    acc = jnp.zeros((FFN_ROWS, D_MODEL), F32)
    for f in range(EXPERT_FF // FFN_FCHUNK):
        cols = slice(f * FFN_FCHUNK, (f + 1) * FFN_FCHUNK)
        gt = jnp.dot(x, wg_ref[0, :, cols], preferred_element_type=F32)
        up = jnp.dot(x, wu_ref[0, :, cols], preferred_element_type=F32)
        hid = (gt * jax.nn.sigmoid(gt) * up).astype(BF16)
        acc = acc + jnp.dot(hid, wd_ref[0, cols, :], preferred_element_type=F32)
    _store_rows(o_ref, 0, acc)


def _ffn(xe, wg, wu, wd, cap):
    e = wg.shape[0]
    per_e = cap // FFN_ROWS
    slots = pl.BlockSpec((FFN_ROWS * SUBLANES, LANES), lambda ei, j: (ei * per_e + j, 0))
    wspec = lambda a, b: pl.BlockSpec((1, a, b), lambda ei, j: (ei, 0, 0))
    return pl.pallas_call(
        _ffn_kernel,
        grid=(e, per_e),
        in_specs=[slots, wspec(D_MODEL, EXPERT_FF), wspec(D_MODEL, EXPERT_FF),
                  wspec(EXPERT_FF, D_MODEL)],
        out_specs=slots,
        out_shape=jax.ShapeDtypeStruct(xe.shape, F32),
        compiler_params=_params("parallel", "arbitrary"),
        name="expert_ffn",
    )(xe, wg, wu, wd)


def _threshold_kernel(aff_ref, thr_ref, need_ref, *, cap):
    def count(mask):
        return jnp.sum(jnp.where(mask, 1.0, 0.0), axis=1, keepdims=True)

    def count_ge(c):
        return count(aff_ref[...] >= c)

    p = jnp.full((N_EXPERTS, 1), 2.0, F32)
    for step in (64, 32, 16, 8, 4, 2, 1):
        trial = p * (2.0 ** -step)
        p = jnp.where((trial > 0.0) & (count_ge(trial) < cap), trial, p)
    base = p * 0.5

    def body(_, carry):
        thr, inc = carry
        cand = thr + inc
        return jnp.where(count_ge(cand) >= cap, cand, thr), inc * 0.5

    thr, _ = lax.fori_loop(0, 52, body, (base, base * 0.5))
    thr_ref[...] = thr
    need_ref[...] = cap - count(aff_ref[...] > thr)


def _positions_kernel(aff_ref, thr_ref, need_ref, pos_ref, tie_run_ref, sel_run_ref, *, block):
    @pl.when(pl.program_id(0) == 0)
    def _():
        tie_run_ref[...] = jnp.zeros_like(tie_run_ref)
        sel_run_ref[...] = jnp.zeros_like(sel_run_ref)

    thr = thr_ref[...]
    need = need_ref[...]
    r = lax.broadcasted_iota(jnp.int32, (LANES, LANES), 0)
    c = lax.broadcasted_iota(jnp.int32, (LANES, LANES), 1)
    before = jnp.where(r < c, 1.0, 0.0).astype(BF16)
    tie_run = tie_run_ref[...]
    sel_run = sel_run_ref[...]
    for s in range(block // LANES):
        lanes = slice(s * LANES, (s + 1) * LANES)
        aff = aff_ref[:, lanes]
        tie = jnp.where(aff == thr, 1.0, 0.0)
        tie_before = jnp.dot(tie.astype(BF16), before, preferred_element_type=F32) + tie_run
        sel = (aff > thr) | ((aff == thr) & (tie_before < need))
        picked = jnp.where(sel, 1.0, 0.0)
        sel_before = jnp.dot(picked.astype(BF16), before, preferred_element_type=F32) + sel_run
        pos_ref[:, lanes] = jnp.where(sel, sel_before.astype(jnp.int32), -1)
        tie_run = tie_run + jnp.sum(tie, axis=1, keepdims=True)
        sel_run = sel_run + jnp.sum(picked, axis=1, keepdims=True)
    tie_run_ref[...] = tie_run
    sel_run_ref[...] = sel_run


def _route(aff_t, cap):
    e, t = aff_t.shape
    col = jax.ShapeDtypeStruct((e, 1), F32)
    thr, need = pl.pallas_call(
        functools.partial(_threshold_kernel, cap=cap),
        out_shape=[col, jax.ShapeDtypeStruct((e, 1), F32)],
        compiler_params=_params(),
        name="route_threshold",
    )(aff_t)
    block = min(POS_BLOCK, t)
    cspec = pl.BlockSpec((e, 1), lambda i: (0, 0))
    return pl.pallas_call(
        functools.partial(_positions_kernel, block=block),
        grid=(t // block,),
        in_specs=[pl.BlockSpec((e, block), lambda i: (0, i)), cspec, cspec],
        out_specs=pl.BlockSpec((e, block), lambda i: (0, i)),
        out_shape=jax.ShapeDtypeStruct((e, t), jnp.int32),
        scratch_shapes=[pltpu.VMEM((e, 1), F32), pltpu.VMEM((e, 1), F32)],
        compiler_params=_params("arbitrary"),
        name="route_positions",
    )(aff_t, thr, need)


def _tile_tables(pos, tile):
    e, t = pos.shape
    n = jnp.sum((pos >= 0).reshape(e, t // tile, tile), axis=-1, dtype=jnp.int32)
    s0 = jnp.cumsum(n, axis=1) - n
    off = jnp.cumsum(n, axis=0) - n
    shift = (off - s0).T.reshape(t // tile, e, 1)
    return n.reshape(-1), s0.reshape(-1), shift


def _tile_counts(n_ref, tile, ntile):
    counts = [n_ref[e * ntile + tile] for e in range(N_EXPERTS)]
    offs = [jnp.int32(0)]
    for c in counts:
        offs.append(offs[-1] + c)
    return counts, offs


def _one_hot_t(stagepos, kbase, values=None):
    rel = stagepos - kbase
    rel = jnp.where((rel >= 0) & (rel < STAGE_CHUNK), rel, -1).astype(F32).astype(BF16)
    k = lax.broadcasted_iota(jnp.int32, (STAGE_CHUNK, stagepos.shape[1]), 0).astype(F32).astype(BF16)
    one = jnp.ones((1, stagepos.shape[1]), BF16)
    pt = jnp.zeros(k.shape, BF16)
    for e in range(N_EXPERTS):
        hit = rel[e:e + 1, :] == k
        pt = jnp.where(hit, one if values is None else values[e:e + 1, :], pt)
    return pt


def _stage_rows(e, tile):
    chunks = max(STATIC_CHUNKS, -(-e * tile // STAGE_CHUNK))
    return chunks * STAGE_CHUNK


def _for_each_chunk(total, body, init):
    carry = init
    for c in range(STATIC_CHUNKS):
        carry = body(c * STAGE_CHUNK, carry)
    return lax.fori_loop(
        STATIC_CHUNKS, (total + STAGE_CHUNK - 1) // STAGE_CHUNK,
        lambda c, carry: body(pl.multiple_of(c * STAGE_CHUNK, STAGE_CHUNK), carry), carry)


def _rows_copy(src, dst, sem, src_row, dst_row, n):
    return pltpu.make_async_copy(
        src.at[pl.ds(pl.multiple_of(src_row * SUBLANES, SUBLANES), n * SUBLANES)],
        dst.at[pl.ds(pl.multiple_of(dst_row * SUBLANES, SUBLANES), n * SUBLANES)], sem)


def _dispatch_kernel(n_ref, s0_ref, pos_ref, shift_ref, h_ref, xe_hbm, stage_ref, sem, *, ntile, cap):
    i = pl.program_id(0)
    slot = i % 2

    def wait_writes(tile, slot):
        _, offs = _tile_counts(n_ref, tile, ntile)

        @pl.when(offs[-1] > 0)
        def _():
            _rows_copy(stage_ref.at[slot], xe_hbm, sem.at[slot], 0, 0, offs[-1]).wait()

    @pl.when(i >= 2)
    def _():
        wait_writes(i - 2, slot)

    counts, offs = _tile_counts(n_ref, i, ntile)
    pos = pos_ref[...]
    stagepos = jnp.where(pos >= 0, pos + shift_ref[0], -1)

    def chunk(kbase, carry):
        rows = jnp.dot(_one_hot_t(stagepos, kbase), h_ref[...], preferred_element_type=F32)
        _store_rows(stage_ref, kbase, rows, lead=(slot,))
        return carry

    _for_each_chunk(offs[-1], chunk, 0)

    for e in range(N_EXPERTS):
        @pl.when(counts[e] > 0)
        def _(e=e):
            _rows_copy(stage_ref.at[slot], xe_hbm, sem.at[slot], offs[e],
                       e * cap + s0_ref[e * ntile + i], counts[e]).start()

    @pl.when(i == ntile - 1)
    def _():
        wait_writes(i, slot)
        if ntile > 1:
            wait_writes(i - 1, 1 - slot)


def _dispatch(n, s0, pos, shift, h1b, cap):
    e, t = pos.shape
    tile = ROUTE_TILE
    ntile = t // tile
    return pl.pallas_call(
        functools.partial(_dispatch_kernel, ntile=ntile, cap=cap),
        grid_spec=pltpu.PrefetchScalarGridSpec(
            num_scalar_prefetch=2,
            grid=(ntile,),
            in_specs=[
                pl.BlockSpec((e, tile), lambda i, *_: (0, i)),
                pl.BlockSpec((1, e, 1), lambda i, *_: (i, 0, 0)),
                pl.BlockSpec((tile, D_MODEL), lambda i, *_: (i, 0)),
            ],
            out_specs=pl.BlockSpec(memory_space=pl.ANY),
            scratch_shapes=[pltpu.VMEM((2, _stage_rows(e, tile) * SUBLANES, LANES), F32),
                            pltpu.SemaphoreType.DMA((2,))],
        ),
        out_shape=jax.ShapeDtypeStruct((e * cap * SUBLANES, LANES), F32),
        compiler_params=_params("arbitrary"),
        name="route_dispatch",
    )(n, s0, pos, shift, h1b)


def _combine_kernel(n_ref, s0_ref, pos_ref, aff_ref, shift_ref, h_ref, g_ref, b_ref, ye_hbm, o_ref,
                    stage_ref, sem, *, ntile, cap):
    i = pl.program_id(0)
    slot = i % 2

    def fetch(tile, slot):
        counts, offs = _tile_counts(n_ref, tile, ntile)
        for e in range(N_EXPERTS):
            @pl.when(counts[e] > 0)
            def _(e=e):
                _rows_copy(ye_hbm, stage_ref.at[slot], sem.at[slot],
                           e * cap + s0_ref[e * ntile + tile], offs[e], counts[e]).start()

    @pl.when(i == 0)
    def _():
        stage_ref[...] = jnp.zeros_like(stage_ref)
        fetch(0, 0)

    @pl.when(i + 1 < ntile)
    def _():
        fetch(i + 1, 1 - slot)

    _, offs = _tile_counts(n_ref, i, ntile)

    @pl.when(offs[-1] > 0)
    def _():
        _rows_copy(ye_hbm, stage_ref.at[slot], sem.at[slot], 0, 0, offs[-1]).wait()

    pos = pos_ref[...]
    stagepos = jnp.where(pos >= 0, pos + shift_ref[0], -1)
    gates = aff_ref[...].astype(BF16)

    def chunk(kbase, acc):
        ye = _load_rows(stage_ref, kbase, STAGE_CHUNK, lead=(slot,)).astype(BF16)
        return acc + lax.dot_general(_one_hot_t(stagepos, kbase, gates), ye,
                                     (((0,), (0,)), ((), ())), preferred_element_type=F32)

    z = _for_each_chunk(offs[-1], chunk, ALPHA * h_ref[...])
    mu = jnp.mean(z, axis=-1, keepdims=True)
    zc = z - mu
    var = jnp.mean(zc * zc, axis=-1, keepdims=True)
    o_ref[...] = zc * lax.rsqrt(var + LN_EPS) * g_ref[...] + b_ref[...]


def _combine(n, s0, pos, aff_t, shift, h1, g2, b2, ye, cap):
    e, t = pos.shape
    tile = ROUTE_TILE
    ntile = t // tile
    lanes = pl.BlockSpec((e, tile), lambda i, *_: (0, i))
    vec = pl.BlockSpec((1, D_MODEL), lambda i, *_: (0, 0))
    row = pl.BlockSpec((tile, D_MODEL), lambda i, *_: (i, 0))
    return pl.pallas_call(
        functools.partial(_combine_kernel, ntile=ntile, cap=cap),
        grid_spec=pltpu.PrefetchScalarGridSpec(
            num_scalar_prefetch=2,
            grid=(ntile,),
            in_specs=[lanes, lanes, pl.BlockSpec((1, e, 1), lambda i, *_: (i, 0, 0)), row, vec, vec,
                      pl.BlockSpec(memory_space=pl.ANY)],
            out_specs=row,
            scratch_shapes=[pltpu.VMEM((2, _stage_rows(e, tile) * SUBLANES, LANES), F32),
                            pltpu.SemaphoreType.DMA((2,))],
        ),
        out_shape=jax.ShapeDtypeStruct((t, D_MODEL), F32),
        compiler_params=_params("arbitrary"),
        name="route_combine_ln2",
    )(n, s0, pos, aff_t, shift, h1, g2, b2, ye)


def _layer(x, p):
    b, l, _ = x.shape
    t = b * l
    cap = CAPACITY_FACTOR * t // N_EXPERTS
    x2 = x.reshape(t, D_MODEL)
    q, k, v, xc, gate = _in_proj(x2, p["w_in"], p["conv_w"], p["conv_b"], l)
    attn = _attention(q.reshape(b, l, -1), k.reshape(b, l, -1), v.reshape(b, l, -1),
                      p["bias"], p["sink_row"], p["attn_g_col"])
    xc3 = xc.reshape(b, l, RNN_WIDTH)
    hs = [_lru(xc3, p["wa"][d], p["wx"][d], p["ba"][d], p["bx"][d], p["lam"][d], reverse=bool(d))
          for d in range(2)]
    h1, h1b, aff_t = _mix_out(attn.reshape(t, -1), hs[0].reshape(t, -1), hs[1].reshape(t, -1), gate,
                              x2, p["w_out"], p["rnn_g"], p["ln1_g"], p["ln1_b"], p["wr_t"])
    pos = _route(aff_t, cap)
    n, s0, shift = _tile_tables(pos, ROUTE_TILE)
    xe = _dispatch(n, s0, pos, shift, h1b, cap)
    ye = _ffn(xe, p["wg"], p["wu"], p["wd"], cap)
    out = _combine(n, s0, pos, aff_t, shift, h1, p["ln2_g"], p["ln2_b"], ye, cap)
    return out.reshape(b, l, D_MODEL)


def _layer_params(li, w_in, attn_sink, attn_norm_g, rnn_norm_g, conv_w, conv_b, lru_w_a, lru_b_a,
                  lru_w_x, lru_b_x, lru_lambda, w_out, ln1_g, ln1_b, w_router, w_gate, w_up, w_down,
                  ln2_g, ln2_b):
    vec = lambda a: a[li].reshape(1, -1)
    per_dir = lambda f: [f(d) for d in range(2)]
    return dict(
        w_in=w_in[li].astype(BF16),
        bias=_alibi_bias(),
        sink_row=jnp.repeat(attn_sink[li].astype(F32), BLOCK).reshape(N_KV_HEADS, 1, Q_PER_KV * BLOCK),
        attn_g_col=jnp.broadcast_to(attn_norm_g[li].astype(F32)[:, None], (ATTN_WIDTH, BLOCK)),
        rnn_g=vec(rnn_norm_g),
        conv_w=conv_w[li], conv_b=vec(conv_b),
        wa=per_dir(lambda d: (0.5 * _block_diag(lru_w_a[li, d])).astype(BF16)),
        wx=per_dir(lambda d: (0.5 * _block_diag(lru_w_x[li, d])).astype(BF16)),
        ba=per_dir(lambda d: 0.5 * lru_b_a[li, d].reshape(1, -1)),
        bx=per_dir(lambda d: 0.5 * lru_b_x[li, d].reshape(1, -1)),
        lam=per_dir(lambda d: lru_lambda[li, d].reshape(1, -1)),
        w_out=w_out[li].astype(BF16),
        ln1_g=vec(ln1_g), ln1_b=vec(ln1_b),
        wr_t=w_router[li].T.astype(BF16),
        wg=w_gate[li].astype(BF16), wu=w_up[li].astype(BF16), wd=w_down[li].astype(BF16),
        ln2_g=vec(ln2_g), ln2_b=vec(ln2_b),
    )


def kernel(x_prompt, x_sample, w_in, attn_sink, attn_norm_g, rnn_norm_g, conv_w, conv_b, lru_w_a,
           lru_b_a, lru_w_x, lru_b_x, lru_lambda, w_out, ln1_g, ln1_b, w_router, w_gate, w_up,
           w_down, ln2_g, ln2_b):
    layers = [
        _layer_params(li, w_in, attn_sink, attn_norm_g, rnn_norm_g, conv_w, conv_b, lru_w_a, lru_b_a,
                      lru_w_x, lru_b_x, lru_lambda, w_out, ln1_g, ln1_b, w_router, w_gate, w_up,
                      w_down, ln2_g, ln2_b)
        for li in range(w_in.shape[0])]
    ys = []
    for x in (x_prompt, x_sample):
        for p in layers:
            x = _layer(x, p)
        ys.append(x)
    return tuple(ys)
```

```python
import functools
import math

import jax
import jax.numpy as jnp
from jax import lax
from jax.experimental import pallas as pl
from jax.experimental.pallas import tpu as pltpu

D_MODEL = 1024
HEAD_DIM = 64
N_Q_HEADS = 8
N_KV_HEADS = 2
Q_PER_KV = N_Q_HEADS // N_KV_HEADS
ATTN_WIDTH = N_Q_HEADS * HEAD_DIM
KV_WIDTH = N_KV_HEADS * HEAD_DIM
BLOCK = 128
RNN_WIDTH = 512
RNN_BLOCKS = 8
RNN_BLOCK_W = RNN_WIDTH // RNN_BLOCKS
CONV_WIDTH = 4
LRU_C = 8.0
N_EXPERTS = 16
EXPERT_FF = 2048
CAPACITY_FACTOR = 2
ALPHA = 2.0 ** 0.25
LN_EPS = 1e-5
RMS_EPS = 1e-6
MASKED = -1e30
TINY = 1e-37

SUBLANES = 8
LANES = 128
VMEM_LIMIT = 56 * 1024 * 1024

ROW_TILE = 512
HALO = 2 * SUBLANES
ATTN_QBLOCKS = 16
LRU_CHUNK = 256
LRU_BATCH = SUBLANES
LRU_UNROLL = 8
FFN_ROWS = 512
FFN_FCHUNK = 256
POS_BLOCK = 2048
ROUTE_TILE = 256
STAGE_CHUNK = 192
STATIC_CHUNKS = 3

BF16 = jnp.bfloat16
F32 = jnp.float32


def _params(*sem):
    return pltpu.CompilerParams(dimension_semantics=sem, vmem_limit_bytes=VMEM_LIMIT)


def _in_proj_kernel(x_ref, xp_ref, xn_ref, w_ref, cw_ref, cb_ref, q_ref, k_ref, v_ref, xc_ref, gate_ref,
                    *, tiles_per_seq):
    i = pl.program_id(0)
    xb = x_ref[...].astype(BF16)

    def proj(lo, hi):
        return jnp.dot(xb, w_ref[:, lo:hi], preferred_element_type=F32)

    q_lo, k_lo, v_lo = 0, ATTN_WIDTH, ATTN_WIDTH + KV_WIDTH
    r_lo = v_lo + KV_WIDTH
    g_lo = r_lo + RNN_WIDTH

    ext_b = jnp.concatenate([xp_ref[...].astype(BF16), xb, xn_ref[...].astype(BF16)], axis=0)
    ext = jnp.dot(ext_b, w_ref[:, r_lo:g_lo], preferred_element_type=F32)
    first = i % tiles_per_seq == 0
    last = i % tiles_per_seq == tiles_per_seq - 1
    n = ROW_TILE + 2 * HALO
    ext = jnp.concatenate([jnp.where(first, 0.0, ext[:HALO]), ext[HALO:HALO + ROW_TILE],
                           jnp.where(last, 0.0, ext[HALO + ROW_TILE:])], axis=0)
    mid = slice(HALO, HALO + ROW_TILE)
    w = cw_ref[...]
    xc = pltpu.roll(ext, 2, 0)[mid] * w[0:1] + pltpu.roll(ext, 1, 0)[mid] * w[1:2]
    xc = xc + ext[mid] * w[2:3]
    xc = xc + pltpu.roll(ext, n - 1, 0)[mid] * w[3:4]
    xc_ref[...] = xc + cb_ref[...]

    q_ref[...] = (proj(q_lo, k_lo) * (HEAD_DIM ** -0.5)).astype(BF16)
    k_ref[...] = proj(k_lo, v_lo).astype(BF16)
    v_ref[...] = proj(v_lo, r_lo).astype(BF16)
    gate_ref[...] = proj(g_lo, g_lo + RNN_WIDTH)


def _in_proj(x2, w_bf16, conv_w, conv_b, seq_len):
    t = x2.shape[0]
    in_w = w_bf16.shape[1]
    assert seq_len % ROW_TILE == 0 and ROW_TILE % HALO == 0
    per = ROW_TILE // HALO
    row = lambda w: pl.BlockSpec((ROW_TILE, w), lambda i: (i, 0))
    const = lambda i: (0, 0)
    return pl.pallas_call(
        functools.partial(_in_proj_kernel, tiles_per_seq=seq_len // ROW_TILE),
        grid=(t // ROW_TILE,),
        in_specs=[
            row(D_MODEL),
            pl.BlockSpec((HALO, D_MODEL), lambda i: (jnp.maximum(i * per - 1, 0), 0)),
            pl.BlockSpec((HALO, D_MODEL), lambda i: (jnp.minimum((i + 1) * per, t // HALO - 1), 0)),
            pl.BlockSpec((D_MODEL, in_w), const),
            pl.BlockSpec((CONV_WIDTH, RNN_WIDTH), const),
            pl.BlockSpec((1, RNN_WIDTH), const),
        ],
        out_specs=[row(ATTN_WIDTH), row(KV_WIDTH), row(KV_WIDTH), row(RNN_WIDTH), row(RNN_WIDTH)],
        out_shape=[
            jax.ShapeDtypeStruct((t, ATTN_WIDTH), BF16),
            jax.ShapeDtypeStruct((t, KV_WIDTH), BF16),
            jax.ShapeDtypeStruct((t, KV_WIDTH), BF16),
            jax.ShapeDtypeStruct((t, RNN_WIDTH), F32),
            jax.ShapeDtypeStruct((t, RNN_WIDTH), F32),
        ],
        compiler_params=_params("parallel"),
        name="in_proj",
    )(x2, x2, x2, w_bf16, conv_w, conv_b)


def _attn_kernel(q_ref, kp_ref, kc_ref, kn_ref, vp_ref, vc_ref, vn_ref, bias_first_ref, bias_mid_ref,
                 bias_last_ref, sink_ref, g_ref, o_ref):
    k = jnp.concatenate([kp_ref[0], kc_ref[0], kn_ref[0]], axis=0)
    v = jnp.concatenate([vp_ref[0], vc_ref[0], vn_ref[0]], axis=0)
    v_t = v.astype(F32).T
    pad_rows = 2 * SUBLANES
    ones_row = jnp.where(
        lax.broadcasted_iota(jnp.int32, (pad_rows, 3 * BLOCK), 0) == 0, 1.0, 0.0)
    for qb in range(ATTN_QBLOCKS):
        q = q_ref[0, qb * BLOCK:(qb + 1) * BLOCK, :]
        keys = slice(qb * BLOCK, (qb + 3) * BLOCK)
        bias_ref = (bias_first_ref if qb == 0 else
                    bias_last_ref if qb == ATTN_QBLOCKS - 1 else bias_mid_ref)
        heads = []
        for g in range(N_KV_HEADS):
            qs = jnp.concatenate(
                [q[:, (Q_PER_KV * g + j) * HEAD_DIM:(Q_PER_KV * g + j + 1) * HEAD_DIM]
                 for j in range(Q_PER_KV)], axis=0)
            kg = k[keys, g * HEAD_DIM:(g + 1) * HEAD_DIM]
            s = lax.dot_general(kg, qs, (((1,), (1,)), ((), ())), preferred_element_type=F32)
            s = s + bias_ref[0, g]
            sink = sink_ref[g]
            m = jnp.maximum(jnp.max(s, axis=0, keepdims=True), sink)
            p = jnp.exp(s - m).astype(BF16)
            lhs = jnp.concatenate([v_t[g * HEAD_DIM:(g + 1) * HEAD_DIM, keys], ones_row],
                                  axis=0).astype(BF16)
            o_aug = jnp.dot(lhs, p, preferred_element_type=F32)
            denom = o_aug[HEAD_DIM:HEAD_DIM + 1] + jnp.exp(sink - m)
            o = o_aug[:HEAD_DIM] * (1.0 / denom)
            heads += [o[:, j * BLOCK:(j + 1) * BLOCK] for j in range(Q_PER_KV)]
        y_t = jnp.concatenate(heads, axis=0)
        ms = jnp.mean(y_t * y_t, axis=0, keepdims=True)
        o_ref[0, qb * BLOCK:(qb + 1) * BLOCK, :] = (
            y_t * lax.rsqrt(ms + RMS_EPS) * g_ref[...]).T.astype(BF16)


def _attention(q, k, v, bias, sink_row, attn_g_col):
    b, l, _ = q.shape
    span = ATTN_QBLOCKS * BLOCK
    assert ATTN_QBLOCKS >= 2 and l % span == 0
    ns = l // span
    nb = l // BLOCK
    edge = lambda f: pl.BlockSpec((1, BLOCK, KV_WIDTH), f)
    prev = lambda bi, i: (bi, jnp.maximum(i * ATTN_QBLOCKS - 1, 0), 0)
    cur = lambda bi, i: (bi, i, 0)
    nxt = lambda bi, i: (bi, jnp.minimum((i + 1) * ATTN_QBLOCKS, nb - 1), 0)
    mid = pl.BlockSpec((1, span, KV_WIDTH), cur)
    bias_spec = lambda f: pl.BlockSpec((1,) + bias.shape[1:], f)
    first = lambda bi, i: ((i == 0).astype(jnp.int32), 0, 0, 0)
    last = lambda bi, i: (2 * (i == ns - 1).astype(jnp.int32), 0, 0, 0)
    return pl.pallas_call(
        _attn_kernel,
        grid=(b, ns),
        in_specs=[
            pl.BlockSpec((1, span, ATTN_WIDTH), cur),
            edge(prev), mid, edge(nxt),
            edge(prev), mid, edge(nxt),
            bias_spec(first), bias_spec(lambda bi, i: (0, 0, 0, 0)), bias_spec(last),
            pl.BlockSpec(sink_row.shape, lambda bi, i: (0, 0, 0)),
            pl.BlockSpec(attn_g_col.shape, lambda bi, i: (0, 0)),
        ],
        out_specs=pl.BlockSpec((1, span, ATTN_WIDTH), cur),
        out_shape=jax.ShapeDtypeStruct((b, l, ATTN_WIDTH), BF16),
        compiler_params=_params("parallel", "parallel"),
        name="banded_attention",
    )(q, k, k, k, v, v, v, bias, bias, bias, sink_row, attn_g_col)


def _alibi_bias():
    qi = jnp.arange(BLOCK)[None, :]
    sj = jnp.arange(3 * BLOCK)[:, None]
    rel = sj - BLOCK - qi
    dist = jnp.abs(rel).astype(F32)
    slopes = jnp.asarray([2.0 ** (-8.0 * (h + 1) / N_Q_HEADS) for h in range(N_Q_HEADS)], F32)
    bias = jnp.where((jnp.abs(rel) <= BLOCK)[None], -slopes[:, None, None] * dist[None], MASKED)
    bias = bias.reshape(N_KV_HEADS, Q_PER_KV, 3 * BLOCK, BLOCK).transpose(0, 2, 1, 3)
    bias = bias.reshape(N_KV_HEADS, 3 * BLOCK, Q_PER_KV * BLOCK)
    no_prev = (sj < BLOCK)[None]
    no_next = (sj >= 2 * BLOCK)[None]
    return jnp.stack([
        bias,
        jnp.where(no_prev, MASKED, bias),
        jnp.where(no_next, MASKED, bias),
    ])


def _lru_kernel(x_ref, wa_ref, wx_ref, ba_ref, bx_ref, lam_ref, h_ref, a_scr, u_scr, h_scr,
                carry_ref, *, lc, reverse):
    pitch = lc + SUBLANES
    slabs = RNN_WIDTH // LANES

    @pl.when(pl.program_id(1) == 0)
    def _():
        carry_ref[...] = jnp.zeros_like(carry_ref)

    lam = lam_ref[...]
    softplus_neg_lam = jnp.maximum(-lam, 0.0) + jnp.log1p(jnp.exp(-jnp.abs(lam)))
    half_rate = (-0.5 * LRU_C) * softplus_neg_lam
    for b in range(LRU_BATCH):
        xc = x_ref[b]
        xcb = xc.astype(BF16)
        th_r = jnp.tanh(jnp.dot(xcb, wa_ref[...], preferred_element_type=F32) + ba_ref[...])
        th_i = jnp.tanh(jnp.dot(xcb, wx_ref[...], preferred_element_type=F32) + bx_ref[...])
        log_a = th_r * half_rate + half_rate
        a = jnp.exp(log_a)
        z = jnp.tanh(log_a) * (-1.0 - a * a)
        root = z * lax.rsqrt(jnp.maximum(z, TINY))
        u = root * ((0.5 * th_i + 0.5) * xc)
        for j in range(slabs):
            a_scr[j, b * pitch:b * pitch + lc, :] = a[:, j * LANES:(j + 1) * LANES]
            u_scr[j, b * pitch:b * pitch + lc, :] = u[:, j * LANES:(j + 1) * LANES]

    def step(i, hs):
        t = lc - 1 - i if reverse else i
        out = []
        for j in range(slabs):
            rows = pl.ds(t, LRU_BATCH, stride=pitch)
            h = a_scr[j, rows, :] * hs[j] + u_scr[j, rows, :]
            h_scr[j, rows, :] = h
            out.append(h)
        return tuple(out)

    hs = lax.fori_loop(0, lc, step, tuple(carry_ref[j] for j in range(slabs)), unroll=LRU_UNROLL)
    for j in range(slabs):
        carry_ref[j] = hs[j]
        for b in range(LRU_BATCH):
            h_ref[b, :, j * LANES:(j + 1) * LANES] = h_scr[j, b * pitch:b * pitch + lc, :]


def _lru(xc, wa, wx, ba, bx, lam, *, reverse):
    b, l, _ = xc.shape
    lc = LRU_CHUNK
    assert b % LRU_BATCH == 0 and l % lc == 0
    nc = l // lc
    pos = (lambda c: nc - 1 - c) if reverse else (lambda c: c)
    const = lambda bi, c: (0, 0)
    vec = pl.BlockSpec((1, RNN_WIDTH), const)
    mat = pl.BlockSpec((RNN_WIDTH, RNN_WIDTH), const)
    chunk = pl.BlockSpec((LRU_BATCH, lc, RNN_WIDTH), lambda bi, c: (bi, pos(c), 0))
    slabs = RNN_WIDTH // LANES
    scratch = pltpu.VMEM((slabs, LRU_BATCH * (lc + SUBLANES), LANES), F32)
    return pl.pallas_call(
        functools.partial(_lru_kernel, lc=lc, reverse=reverse),
        grid=(b // LRU_BATCH, nc),
        in_specs=[chunk, mat, mat, vec, vec, vec],
        out_specs=chunk,
        out_shape=jax.ShapeDtypeStruct((b, l, RNN_WIDTH), F32),
        scratch_shapes=[scratch, scratch, scratch, pltpu.VMEM((slabs, LRU_BATCH, LANES), F32)],
        compiler_params=_params("parallel", "arbitrary"),
        name="rg_lru_bwd" if reverse else "rg_lru_fwd",
    )(xc, wa, wx, ba, bx, lam)


def _block_diag(w):
    eye = jnp.eye(RNN_BLOCKS, dtype=w.dtype)
    return jnp.einsum('hij,hk->hikj', w, eye).reshape(RNN_WIDTH, RNN_WIDTH)


def _mix_out_kernel(attn_ref, hf_ref, hb_ref, gate_ref, x_ref, wo_ref, rg_ref, g1_ref, b1_ref,
                    wr_ref, h_ref, hb16_ref, aff_ref):
    gt = gate_ref[...]
    gelu = 0.5 * gt * (1.0 + jnp.tanh(math.sqrt(2.0 / math.pi) * (gt + 0.044715 * (gt * gt * gt))))
    yr = (hf_ref[...] + hb_ref[...]) * gelu
    ms = jnp.mean(yr * yr, axis=-1, keepdims=True)
    yn = (yr * lax.rsqrt(ms + RMS_EPS) * rg_ref[...]).astype(BF16)
    mix = jnp.dot(attn_ref[...], wo_ref[:ATTN_WIDTH, :], preferred_element_type=F32)
    mix = mix + jnp.dot(yn, wo_ref[ATTN_WIDTH:, :], preferred_element_type=F32)
    z = ALPHA * x_ref[...] + mix
    mu = jnp.mean(z, axis=-1, keepdims=True)
    zc = z - mu
    var = jnp.mean(zc * zc, axis=-1, keepdims=True)
    h = zc * lax.rsqrt(var + LN_EPS) * g1_ref[...] + b1_ref[...]
    h_ref[...] = h
    hb = h.astype(BF16)
    hb16_ref[...] = hb
    logits = lax.dot_general(wr_ref[...], hb, (((1,), (1,)), ((), ())),
                             preferred_element_type=F32)
    mx = jnp.max(logits, axis=0, keepdims=True)
    e = jnp.exp(logits - mx)
    aff_ref[...] = e / jnp.sum(e, axis=0, keepdims=True)


def _mix_out(attn, hf, hb, gate, x2, wo, rnn_g, g1, b1, wr_t):
    t = x2.shape[0]
    row = lambda w: pl.BlockSpec((ROW_TILE, w), lambda i: (i, 0))
    const = lambda i: (0, 0)
    return pl.pallas_call(
        _mix_out_kernel,
        grid=(t // ROW_TILE,),
        in_specs=[
            row(ATTN_WIDTH), row(RNN_WIDTH), row(RNN_WIDTH), row(RNN_WIDTH), row(D_MODEL),
            pl.BlockSpec((D_MODEL, D_MODEL), const),
            pl.BlockSpec((1, RNN_WIDTH), const),
            pl.BlockSpec((1, D_MODEL), const),
            pl.BlockSpec((1, D_MODEL), const),
            pl.BlockSpec((N_EXPERTS, D_MODEL), const),
        ],
        out_specs=[row(D_MODEL), row(D_MODEL), pl.BlockSpec((N_EXPERTS, ROW_TILE), lambda i: (0, i))],
        out_shape=[
            jax.ShapeDtypeStruct((t, D_MODEL), F32),
            jax.ShapeDtypeStruct((t, D_MODEL), BF16),
            jax.ShapeDtypeStruct((N_EXPERTS, t), F32),
        ],
        compiler_params=_params("parallel"),
        name="mix_out_ln1_router",
    )(attn, hf, hb, gate, x2, wo, rnn_g, g1, b1, wr_t)


def _load_rows(ref, start, n, lead=()):
    return jnp.concatenate(
        [ref[lead + (pl.ds(start * SUBLANES + j, n, stride=SUBLANES), slice(None))]
         for j in range(D_MODEL // LANES)], axis=1)


def _store_rows(ref, start, rows, lead=()):
    n = rows.shape[0]
    for j in range(D_MODEL // LANES):
        ref[lead + (pl.ds(start * SUBLANES + j, n, stride=SUBLANES), slice(None))] = (
            rows[:, j * LANES:(j + 1) * LANES])


def _ffn_kernel(x_ref, wg_ref, wu_ref, wd_ref, o_ref):
    x = _load_rows(x_ref, 0, FFN_ROWS).astype(BF16)
    acc = jnp.zeros((FFN_ROWS, D_MODEL), F32)
    for f in range(EXPERT_FF // FFN_FCHUNK):
        cols = slice(f * FFN_FCHUNK, (f + 1) * FFN_FCHUNK)
        gt = jnp.dot(x, wg_ref[0, :, cols], preferred_element_type=F32)
        up = jnp.dot(x, wu_ref[0, :, cols], preferred_element_type=F32)
        hid = (gt * jax.nn.sigmoid(gt) * up).astype(BF16)
        acc = acc + jnp.dot(hid, wd_ref[0, cols, :], preferred_element_type=F32)
    _store_rows(o_ref, 0, acc)


def _ffn(xe, wg, wu, wd, cap):
    e = wg.shape[0]
    per_e = cap // FFN_ROWS
    slots = pl.BlockSpec((FFN_ROWS * SUBLANES, LANES), lambda ei, j: (ei * per_e + j, 0))
    wspec = lambda a, b: pl.BlockSpec((1, a, b), lambda ei, j: (ei, 0, 0))
    return pl.pallas_call(
        _ffn_kernel,
        grid=(e, per_e),
        in_specs=[slots, wspec(D_MODEL, EXPERT_FF), wspec(D_MODEL, EXPERT_FF),
                  wspec(EXPERT_FF, D_MODEL)],
        out_specs=slots,
        out_shape=jax.ShapeDtypeStruct(xe.shape, F32),
        compiler_params=_params("parallel", "arbitrary"),
        name="expert_ffn",
    )(xe, wg, wu, wd)


def _threshold_kernel(aff_ref, thr_ref, need_ref, *, cap):
    def count(mask):
        return jnp.sum(jnp.where(mask, 1.0, 0.0), axis=1, keepdims=True)

    def count_ge(c):
        return count(aff_ref[...] >= c)

    p = jnp.full((N_EXPERTS, 1), 2.0, F32)
    for step in (64, 32, 16, 8, 4, 2, 1):
        trial = p * (2.0 ** -step)
        p = jnp.where((trial > 0.0) & (count_ge(trial) < cap), trial, p)
    base = p * 0.5

    def body(_, carry):
        thr, inc = carry
        cand = thr + inc
        return jnp.where(count_ge(cand) >= cap, cand, thr), inc * 0.5

    thr, _ = lax.fori_loop(0, 52, body, (base, base * 0.5))
    thr_ref[...] = thr
    need_ref[...] = cap - count(aff_ref[...] > thr)


def _positions_kernel(aff_ref, thr_ref, need_ref, pos_ref, tie_run_ref, sel_run_ref, *, block):
    @pl.when(pl.program_id(0) == 0)
    def _():
        tie_run_ref[...] = jnp.zeros_like(tie_run_ref)
        sel_run_ref[...] = jnp.zeros_like(sel_run_ref)

    thr = thr_ref[...]
    need = need_ref[...]
    r = lax.broadcasted_iota(jnp.int32, (LANES, LANES), 0)
    c = lax.broadcasted_iota(jnp.int32, (LANES, LANES), 1)
    before = jnp.where(r < c, 1.0, 0.0).astype(BF16)
    tie_run = tie_run_ref[...]
    sel_run = sel_run_ref[...]
    for s in range(block // LANES):
        lanes = slice(s * LANES, (s + 1) * LANES)
        aff = aff_ref[:, lanes]
        tie = jnp.where(aff == thr, 1.0, 0.0)
        tie_before = jnp.dot(tie.astype(BF16), before, preferred_element_type=F32) + tie_run
        sel = (aff > thr) | ((aff == thr) & (tie_before < need))
        picked = jnp.where(sel, 1.0, 0.0)
        sel_before = jnp.dot(picked.astype(BF16), before, preferred_element_type=F32) + sel_run
        pos_ref[:, lanes] = jnp.where(sel, sel_before.astype(jnp.int32), -1)
        tie_run = tie_run + jnp.sum(tie, axis=1, keepdims=True)
        sel_run = sel_run + jnp.sum(picked, axis=1, keepdims=True)
    tie_run_ref[...] = tie_run
    sel_run_ref[...] = sel_run


def _route(aff_t, cap):
    e, t = aff_t.shape
    col = jax.ShapeDtypeStruct((e, 1), F32)
    thr, need = pl.pallas_call(
        functools.partial(_threshold_kernel, cap=cap),
        out_shape=[col, jax.ShapeDtypeStruct((e, 1), F32)],
        compiler_params=_params(),
        name="route_threshold",
    )(aff_t)
    block = min(POS_BLOCK, t)
    cspec = pl.BlockSpec((e, 1), lambda i: (0, 0))
    return pl.pallas_call(
        functools.partial(_positions_kernel, block=block),
        grid=(t // block,),
        in_specs=[pl.BlockSpec((e, block), lambda i: (0, i)), cspec, cspec],
        out_specs=pl.BlockSpec((e, block), lambda i: (0, i)),
        out_shape=jax.ShapeDtypeStruct((e, t), jnp.int32),
        scratch_shapes=[pltpu.VMEM((e, 1), F32), pltpu.VMEM((e, 1), F32)],
        compiler_params=_params("arbitrary"),
        name="route_positions",
    )(aff_t, thr, need)


def _tile_tables(pos, tile):
    e, t = pos.shape
    n = jnp.sum((pos >= 0).reshape(e, t // tile, tile), axis=-1, dtype=jnp.int32)
    s0 = jnp.cumsum(n, axis=1) - n
    off = jnp.cumsum(n, axis=0) - n
    shift = (off - s0).T.reshape(t // tile, e, 1)
    return n.reshape(-1), s0.reshape(-1), shift


def _tile_counts(n_ref, tile, ntile):
    counts = [n_ref[e * ntile + tile] for e in range(N_EXPERTS)]
    offs = [jnp.int32(0)]
    for c in counts:
        offs.append(offs[-1] + c)
    return counts, offs


def _one_hot_t(stagepos, kbase, values=None):
    rel = stagepos - kbase
    rel = jnp.where((rel >= 0) & (rel < STAGE_CHUNK), rel, -1).astype(F32).astype(BF16)
    k = lax.broadcasted_iota(jnp.int32, (STAGE_CHUNK, stagepos.shape[1]), 0).astype(F32).astype(BF16)
    one = jnp.ones((1, stagepos.shape[1]), BF16)
    pt = jnp.zeros(k.shape, BF16)
    for e in range(N_EXPERTS):
        hit = rel[e:e + 1, :] == k
        pt = jnp.where(hit, one if values is None else values[e:e + 1, :], pt)
    return pt


def _stage_rows(e, tile):
    chunks = max(STATIC_CHUNKS, -(-e * tile // STAGE_CHUNK))
    return chunks * STAGE_CHUNK


def _for_each_chunk(total, body, init):
    carry = init
    for c in range(STATIC_CHUNKS):
        carry = body(c * STAGE_CHUNK, carry)
    return lax.fori_loop(
        STATIC_CHUNKS, (total + STAGE_CHUNK - 1) // STAGE_CHUNK,
        lambda c, carry: body(pl.multiple_of(c * STAGE_CHUNK, STAGE_CHUNK), carry), carry)


def _rows_copy(src, dst, sem, src_row, dst_row, n):
    return pltpu.make_async_copy(
        src.at[pl.ds(pl.multiple_of(src_row * SUBLANES, SUBLANES), n * SUBLANES)],
        dst.at[pl.ds(pl.multiple_of(dst_row * SUBLANES, SUBLANES), n * SUBLANES)], sem)


def _dispatch_kernel(n_ref, s0_ref, pos_ref, shift_ref, h_ref, xe_hbm, stage_ref, sem, *, ntile, cap):
    i = pl.program_id(0)
    slot = i % 2

    def wait_writes(tile, slot):
        _, offs = _tile_counts(n_ref, tile, ntile)

        @pl.when(offs[-1] > 0)
        def _():
            _rows_copy(stage_ref.at[slot], xe_hbm, sem.at[slot], 0, 0, offs[-1]).wait()

    @pl.when(i >= 2)
    def _():
        wait_writes(i - 2, slot)

    counts, offs = _tile_counts(n_ref, i, ntile)
    pos = pos_ref[...]
    stagepos = jnp.where(pos >= 0, pos + shift_ref[0], -1)

    def chunk(kbase, carry):
        rows = jnp.dot(_one_hot_t(stagepos, kbase), h_ref[...], preferred_element_type=F32)
        _store_rows(stage_ref, kbase, rows, lead=(slot,))
        return carry

    _for_each_chunk(offs[-1], chunk, 0)

    for e in range(N_EXPERTS):
        @pl.when(counts[e] > 0)
        def _(e=e):
            _rows_copy(stage_ref.at[slot], xe_hbm, sem.at[slot], offs[e],
                       e * cap + s0_ref[e * ntile + i], counts[e]).start(priority=e % 2)

    @pl.when(i == ntile - 1)
    def _():
        wait_writes(i, slot)
        if ntile > 1:
            wait_writes(i - 1, 1 - slot)


def _dispatch(n, s0, pos, shift, h1b, cap):
    e, t = pos.shape
    tile = ROUTE_TILE
    ntile = t // tile
    return pl.pallas_call(
        functools.partial(_dispatch_kernel, ntile=ntile, cap=cap),
        grid_spec=pltpu.PrefetchScalarGridSpec(
            num_scalar_prefetch=2,
            grid=(ntile,),
            in_specs=[
                pl.BlockSpec((e, tile), lambda i, *_: (0, i)),
                pl.BlockSpec((1, e, 1), lambda i, *_: (i, 0, 0)),
                pl.BlockSpec((tile, D_MODEL), lambda i, *_: (i, 0)),
            ],
            out_specs=pl.BlockSpec(memory_space=pl.ANY),
            scratch_shapes=[pltpu.VMEM((2, _stage_rows(e, tile) * SUBLANES, LANES), F32),
                            pltpu.SemaphoreType.DMA((2,))],
        ),
        out_shape=jax.ShapeDtypeStruct((e * cap * SUBLANES, LANES), F32),
        compiler_params=_params("arbitrary"),
        name="route_dispatch",
    )(n, s0, pos, shift, h1b)


def _combine_kernel(n_ref, s0_ref, pos_ref, aff_ref, shift_ref, h_ref, g_ref, b_ref, ye_hbm, o_ref,
                    stage_ref, sem, *, ntile, cap):
    i = pl.program_id(0)
    slot = i % 2

    def fetch(tile, slot):
        counts, offs = _tile_counts(n_ref, tile, ntile)
        for e in range(N_EXPERTS):
            @pl.when(counts[e] > 0)
            def _(e=e):
                _rows_copy(ye_hbm, stage_ref.at[slot], sem.at[slot],
                           e * cap + s0_ref[e * ntile + tile], offs[e], counts[e]).start(
                               priority=e % 2)

    @pl.when(i == 0)
    def _():
        stage_ref[...] = jnp.zeros_like(stage_ref)
        fetch(0, 0)

    @pl.when(i + 1 < ntile)
    def _():
        fetch(i + 1, 1 - slot)

    _, offs = _tile_counts(n_ref, i, ntile)

    @pl.when(offs[-1] > 0)
    def _():
        _rows_copy(ye_hbm, stage_ref.at[slot], sem.at[slot], 0, 0, offs[-1]).wait()

    pos = pos_ref[...]
    stagepos = jnp.where(pos >= 0, pos + shift_ref[0], -1)
    gates = aff_ref[...].astype(BF16)

    def chunk(kbase, acc):
        ye = _load_rows(stage_ref, kbase, STAGE_CHUNK, lead=(slot,)).astype(BF16)
        return acc + lax.dot_general(_one_hot_t(stagepos, kbase, gates), ye,
                                     (((0,), (0,)), ((), ())), preferred_element_type=F32)

    z = _for_each_chunk(offs[-1], chunk, ALPHA * h_ref[...])
    mu = jnp.mean(z, axis=-1, keepdims=True)
    zc = z - mu
    var = jnp.mean(zc * zc, axis=-1, keepdims=True)
    o_ref[...] = zc * lax.rsqrt(var + LN_EPS) * g_ref[...] + b_ref[...]


def _combine(n, s0, pos, aff_t, shift, h1, g2, b2, ye, cap):
    e, t = pos.shape
    tile = ROUTE_TILE
    ntile = t // tile
    lanes = pl.BlockSpec((e, tile), lambda i, *_: (0, i))
    vec = pl.BlockSpec((1, D_MODEL), lambda i, *_: (0, 0))
    row = pl.BlockSpec((tile, D_MODEL), lambda i, *_: (i, 0))
    return pl.pallas_call(
        functools.partial(_combine_kernel, ntile=ntile, cap=cap),
        grid_spec=pltpu.PrefetchScalarGridSpec(
            num_scalar_prefetch=2,
            grid=(ntile,),
            in_specs=[lanes, lanes, pl.BlockSpec((1, e, 1), lambda i, *_: (i, 0, 0)), row, vec, vec,
                      pl.BlockSpec(memory_space=pl.ANY)],
            out_specs=row,
            scratch_shapes=[pltpu.VMEM((2, _stage_rows(e, tile) * SUBLANES, LANES), F32),
                            pltpu.SemaphoreType.DMA((2,))],
        ),
        out_shape=jax.ShapeDtypeStruct((t, D_MODEL), F32),
        compiler_params=_params("arbitrary"),
        name="route_combine_ln2",
    )(n, s0, pos, aff_t, shift, h1, g2, b2, ye)


def _layer(x, p):
    b, l, _ = x.shape
    t = b * l
    cap = CAPACITY_FACTOR * t // N_EXPERTS
    x2 = x.reshape(t, D_MODEL)
    q, k, v, xc, gate = _in_proj(x2, p["w_in"], p["conv_w"], p["conv_b"], l)
    attn = _attention(q.reshape(b, l, -1), k.reshape(b, l, -1), v.reshape(b, l, -1),
                      p["bias"], p["sink_row"], p["attn_g_col"])
    xc3 = xc.reshape(b, l, RNN_WIDTH)
    hs = [_lru(xc3, p["wa"][d], p["wx"][d], p["ba"][d], p["bx"][d], p["lam"][d], reverse=bool(d))
          for d in range(2)]
    h1, h1b, aff_t = _mix_out(attn.reshape(t, -1), hs[0].reshape(t, -1), hs[1].reshape(t, -1), gate,
                              x2, p["w_out"], p["rnn_g"], p["ln1_g"], p["ln1_b"], p["wr_t"])
    pos = _route(aff_t, cap)
    n, s0, shift = _tile_tables(pos, ROUTE_TILE)
    xe = _dispatch(n, s0, pos, shift, h1b, cap)
    ye = _ffn(xe, p["wg"], p["wu"], p["wd"], cap)
    out = _combine(n, s0, pos, aff_t, shift, h1, p["ln2_g"], p["ln2_b"], ye, cap)
    return out.reshape(b, l, D_MODEL)


def _layer_params(li, w_in, attn_sink, attn_norm_g, rnn_norm_g, conv_w, conv_b, lru_w_a, lru_b_a,
                  lru_w_x, lru_b_x, lru_lambda, w_out, ln1_g, ln1_b, w_router, w_gate, w_up, w_down,
                  ln2_g, ln2_b):
    vec = lambda a: a[li].reshape(1, -1)
    per_dir = lambda f: [f(d) for d in range(2)]
    return dict(
        w_in=w_in[li].astype(BF16),
        bias=_alibi_bias(),
        sink_row=jnp.repeat(attn_sink[li].astype(F32), BLOCK).reshape(N_KV_HEADS, 1, Q_PER_KV * BLOCK),
        attn_g_col=jnp.broadcast_to(attn_norm_g[li].astype(F32)[:, None], (ATTN_WIDTH, BLOCK)),
        rnn_g=vec(rnn_norm_g),
        conv_w=conv_w[li], conv_b=vec(conv_b),
        wa=per_dir(lambda d: (0.5 * _block_diag(lru_w_a[li, d])).astype(BF16)),
        wx=per_dir(lambda d: (0.5 * _block_diag(lru_w_x[li, d])).astype(BF16)),
        ba=per_dir(lambda d: 0.5 * lru_b_a[li, d].reshape(1, -1)),
        bx=per_dir(lambda d: 0.5 * lru_b_x[li, d].reshape(1, -1)),
        lam=per_dir(lambda d: lru_lambda[li, d].reshape(1, -1)),
        w_out=w_out[li].astype(BF16),
        ln1_g=vec(ln1_g), ln1_b=vec(ln1_b),
        wr_t=w_router[li].T.astype(BF16),
        wg=w_gate[li].astype(BF16), wu=w_up[li].astype(BF16), wd=w_down[li].astype(BF16),
        ln2_g=vec(ln2_g), ln2_b=vec(ln2_b),
    )


def kernel(x_prompt, x_sample, w_in, attn_sink, attn_norm_g, rnn_norm_g, conv_w, conv_b, lru_w_a,
           lru_b_a, lru_w_x, lru_b_x, lru_lambda, w_out, ln1_g, ln1_b, w_router, w_gate, w_up,
           w_down, ln2_g, ln2_b):
    layers = [
        _layer_params(li, w_in, attn_sink, attn_norm_g, rnn_norm_g, conv_w, conv_b, lru_w_a, lru_b_a,
                      lru_w_x, lru_b_x, lru_lambda, w_out, ln1_g, ln1_b, w_router, w_gate, w_up,
                      w_down, ln2_g, ln2_b)
        for li in range(w_in.shape[0])]
    ys = []
    for x in (x_prompt, x_sample):
        for p in layers:
            x = _layer(x, p)
        ys.append(x)
    return tuple(ys)
```
